```python
import jax, jax.numpy as jnp
from jax import lax
import numpy as np

D_MODEL = 2048
BATCH = 2
SEQ = 8192
DEPTH = 1

GLA_HEADS = 4
GLA_DK = 128
GLA_DV = 256
GLA_GATE_RANK = 16
GLA_GATE_NORMALIZER = 16.0
HGRN_HEADS = 8
HGRN_DK = 128
HGRN_DV = 128
GLA_QK = GLA_HEADS * GLA_DK
GLA_V = GLA_HEADS * GLA_DV
HGRN_K = HGRN_HEADS * HGRN_DK
HGRN_V = HGRN_HEADS * HGRN_DV
MIX_WIDTH = GLA_V + HGRN_V
IN_SPLITS = (GLA_QK, GLA_QK, GLA_V, GLA_V, GLA_GATE_RANK, HGRN_K, HGRN_K, HGRN_V, HGRN_V)
IN_WIDTH = 2 * GLA_QK + 2 * GLA_V + GLA_GATE_RANK + 2 * HGRN_K + 2 * HGRN_V
CHUNK = 64
D_FF = 5504
MACARON_W = 0.5
N_MOD = 9
EPS = 1e-6

kernel_name = "hymba_gla_hgrn2_macaron_adaln"


def rmsnorm(x, w):
    xf = x.astype(jnp.float32)
    y = xf * lax.rsqrt(jnp.mean(xf * xf, axis=-1, keepdims=True) + EPS)
    return y.astype(x.dtype) * w


def modulate(h, shift, scale):
    return h * (1.0 + scale[:, None, :]) + shift[:, None, :]


def swiglu(h, wi, wo):
    a, b = jnp.split(h @ wi, 2, axis=-1)
    return (jax.nn.silu(a) * b) @ wo


def chunked_gated_linear_attention(q, k, v, log_a):
    B, H, T, dk = q.shape
    dv = v.shape[-1]
    n = T // CHUNK

    def to_chunks(t):
        t = t.astype(jnp.float32).reshape(B, H, n, CHUNK, t.shape[-1])
        return jnp.moveaxis(t, 2, 0)

    qc, kc, vc, ac = to_chunks(q), to_chunks(k), to_chunks(v), to_chunks(log_a)
    bc = jnp.cumsum(ac, axis=-2)
    causal = jnp.tril(jnp.ones((CHUNK, CHUNK), dtype=bool))[:, :, None]

    def step(S, inp):
        q_, k_, v_, b_ = inp
        o_inter = jnp.einsum('bhtk,bhkv->bhtv', q_ * jnp.exp(b_), S)
        rel = jnp.where(causal, b_[..., :, None, :] - b_[..., None, :, :], -jnp.inf)
        scores = jnp.einsum('bhtk,bhsk,bhtsk->bhts', q_, k_, jnp.exp(rel))
        o_intra = jnp.einsum('bhts,bhsv->bhtv', scores, v_)
        b_last = b_[..., -1:, :]
        S_new = jnp.exp(b_last[..., 0, :])[..., None] * S + jnp.einsum(
            'bhsk,bhsv->bhkv', k_ * jnp.exp(b_last - b_), v_)
        return S_new, o_inter + o_intra

    S0 = jnp.zeros((B, H, dk, dv), jnp.float32)
    _, o = lax.scan(step, S0, (qc, kc, vc, bc))
    return jnp.moveaxis(o, 0, 2).reshape(B, H, T, dv)


def head_rmsnorm(o, w):
    y = o * lax.rsqrt(jnp.mean(o * o, axis=-1, keepdims=True) + EPS)
    return y * w.astype(jnp.float32)


def token_mixer(h, w_in, gla_gate_w2, gla_gate_b2, gla_norm_w, hgrn_norm_w, lb, w_out):
    B, T, _ = h.shape
    z = h @ w_in
    split_at = [int(s) for s in np.cumsum(IN_SPLITS)[:-1]]
    gq, gk, gv, gg, gr, hq, hf, hi, hg = jnp.split(z, split_at, axis=-1)

    def heads(t, n_heads):
        return t.reshape(B, T, n_heads, -1).transpose(0, 2, 1, 3)

    def merge(o):
        return o.transpose(0, 2, 1, 3).reshape(B, T, -1).astype(h.dtype)

    g_log = jax.nn.log_sigmoid((gr @ gla_gate_w2 + gla_gate_b2).astype(jnp.float32)) / GLA_GATE_NORMALIZER
    o_gla = chunked_gated_linear_attention(
        heads(gq * (GLA_DK ** -0.5), GLA_HEADS), heads(gk, GLA_HEADS),
        heads(gv, GLA_HEADS), heads(g_log, GLA_HEADS))
    o_gla = merge(head_rmsnorm(o_gla, gla_norm_w)) * jax.nn.silu(gg)

    f_raw = hf.astype(jnp.float32)
    log_f = jnp.logaddexp(jnp.log(lb), jnp.log1p(-lb) + jax.nn.log_sigmoid(f_raw))
    k_in = -jnp.expm1(log_f)
    o_hg = chunked_gated_linear_attention(
        heads(jax.nn.silu(hq), HGRN_HEADS), heads(k_in, HGRN_HEADS),
        heads(hi, HGRN_HEADS), heads(log_f, HGRN_HEADS))
    o_hg = merge(head_rmsnorm(o_hg, hgrn_norm_w)) * jax.nn.silu(hg)

    return jnp.concatenate([o_gla, o_hg], axis=-1) @ w_out


def setup_inputs(seed: int = 0) -> dict:
    key = jax.random.key(seed)
    ks = jax.random.split(key, 24)
    f32 = jnp.float32

    def nrm(k, shape, scale):
        return jax.random.normal(k, shape, f32) * scale

    def gain(k, shape):
        return jnp.ones(shape, f32) + 0.02 * jax.random.normal(k, shape, f32)

    L, D = DEPTH, D_MODEL
    return {
        "x": nrm(ks[0], (BATCH, SEQ, D), 1.0),
        "c": nrm(ks[1], (BATCH, D), 1.0),
        "ada_w": nrm(ks[2], (L, D, N_MOD * D), 0.5 * D ** -0.5),
        "ada_b": nrm(ks[3], (L, N_MOD * D), 0.02),
        "norm_ffn1_w": gain(ks[4], (L, D)),
        "ffn1_wi": nrm(ks[5], (L, D, 2 * D_FF), D ** -0.5),
        "ffn1_wo": nrm(ks[6], (L, D_FF, D), D_FF ** -0.5),
        "norm_mix_w": gain(ks[7], (L, D)),
        "w_in": nrm(ks[8], (L, D, IN_WIDTH), D ** -0.5),
        "gla_gate_w2": nrm(ks[9], (L, GLA_GATE_RANK, GLA_QK), GLA_GATE_RANK ** -0.5),
        "gla_gate_b2": nrm(ks[10], (L, GLA_QK), 0.02),
        "gla_norm_w": gain(ks[11], (L, GLA_DV)),
        "hgrn_norm_w": gain(ks[12], (L, HGRN_DV)),
        "hgrn_lower_bounds": gain(ks[13], (L + 1, HGRN_K)),
        "w_out": nrm(ks[14], (L, MIX_WIDTH, D), MIX_WIDTH ** -0.5),
        "norm_ffn2_w": gain(ks[15], (L, D)),
        "ffn2_wi": nrm(ks[16], (L, D, 2 * D_FF), D ** -0.5),
        "ffn2_wo": nrm(ks[17], (L, D_FF, D), D_FF ** -0.5),
        "final_norm_w": gain(ks[18], (D,)),
    }


def reference(x, c, ada_w, ada_b, norm_ffn1_w, ffn1_wi, ffn1_wo, norm_mix_w, w_in,
              gla_gate_w2, gla_gate_b2, gla_norm_w, hgrn_norm_w, hgrn_lower_bounds,
              w_out, norm_ffn2_w, ffn2_wi, ffn2_wo, final_norm_w):
    lb_table = jnp.cumsum(jax.nn.softmax(hgrn_lower_bounds.astype(jnp.float32), axis=0), axis=0)
    c_act = jax.nn.silu(c)
    for l in range(DEPTH):
        mod = c_act @ ada_w[l] + ada_b[l]
        sh1, sc1, g1, sh2, sc2, g2, sh3, sc3, g3 = jnp.split(mod, N_MOD, axis=-1)
        h = modulate(rmsnorm(x, norm_ffn1_w[l]), sh1, sc1)
        x = x + MACARON_W * g1[:, None, :] * swiglu(h, ffn1_wi[l], ffn1_wo[l])
        h = modulate(rmsnorm(x, norm_mix_w[l]), sh2, sc2)
        x = x + g2[:, None, :] * token_mixer(h, w_in[l], gla_gate_w2[l], gla_gate_b2[l],
                                              gla_norm_w[l], hgrn_norm_w[l], lb_table[l], w_out[l])
        h = modulate(rmsnorm(x, norm_ffn2_w[l]), sh3, sc3)
        x = x + MACARON_W * g3[:, None, :] * swiglu(h, ffn2_wi[l], ffn2_wo[l])
    return rmsnorm(x, final_norm_w)
```

```python
import functools

import jax
import jax.numpy as jnp
import numpy as np
from jax import lax
from jax.experimental import pallas as pl
from jax.experimental.pallas import tpu as pltpu

F32 = jnp.float32
BF16 = jnp.bfloat16

GLA_HEADS = 4
GLA_DK = 128
GLA_DV = 256
GLA_GATE_RANK = 16
GLA_GATE_NORMALIZER = 16.0
HGRN_HEADS = 8
HGRN_DK = 128
HGRN_DV = 128
CHUNK = 64
MACARON_W = 0.5
N_MOD = 9
EPS = 1e-6

LANES = 128
SUBLANES = 8
VMEM_LIMIT_BYTES = 56 * 1024 * 1024

FFN_TM = 512
FFN_TF = 512
MIX_TC = 512
PROJ_TM = 512
ADALN_TN = 1024

_NT = (((1,), (1,)), ((), ()))
_TN = (((0,), (0,)), ((), ()))


def _sigmoid(x):
    return jax.nn.sigmoid(x)


def _rms(x):
    return x * lax.rsqrt(jnp.mean(x * x, axis=-1, keepdims=True) + EPS)


def _adaln_kernel(c_ref, w_ref, b_ref, o_ref):
    c = c_ref[...]
    ca = (c * _sigmoid(c)).astype(BF16)
    o_ref[...] = jnp.dot(ca, w_ref[...].astype(BF16), preferred_element_type=F32) + b_ref[...]


def _adaln(c_pad, w, b):
    rows, d = c_pad.shape
    n = w.shape[1]
    assert n % ADALN_TN == 0
    return pl.pallas_call(
        _adaln_kernel,
        grid=(n // ADALN_TN,),
        in_specs=[
            pl.BlockSpec((rows, d), lambda j: (0, 0)),
            pl.BlockSpec((d, ADALN_TN), lambda j: (0, j)),
            pl.BlockSpec((1, ADALN_TN), lambda j: (0, j)),
        ],
        out_specs=pl.BlockSpec((rows, ADALN_TN), lambda j: (0, j)),
        out_shape=jax.ShapeDtypeStruct((rows, n), F32),
        compiler_params=pltpu.CompilerParams(
            dimension_semantics=("arbitrary",), vmem_limit_bytes=VMEM_LIMIT_BYTES),
        name="adaln",
    )(c_pad, w, b)


def _ffn_kernel(x_ref, mod_ref, nw_ref, nw2_ref, wia_ref, wib_ref, wo_ref, *rest,
                mod_base, epilogue):
    if epilogue == "prenorm":
        xo_ref, ho_ref, h_scr, acc_scr = rest
    else:
        xo_ref, h_scr, acc_scr = rest
    f = pl.program_id(1)

    @pl.when(f == 0)
    def _():
        y = _rms(x_ref[...]) * nw_ref[...]
        shift = mod_ref[mod_base:mod_base + 1, :]
        scale = mod_ref[mod_base + 1:mod_base + 2, :]
        h_scr[...] = (y * (1.0 + scale) + shift).astype(BF16)
        acc_scr[...] = jnp.zeros_like(acc_scr)

    h = h_scr[...]
    a = jnp.dot(h, wia_ref[...], preferred_element_type=F32)
    b = jnp.dot(h, wib_ref[...], preferred_element_type=F32)
    act = (a * _sigmoid(a) * b).astype(BF16)
    acc_scr[...] += jnp.dot(act, wo_ref[...], preferred_element_type=F32)

    @pl.when(f == pl.num_programs(1) - 1)
    def _():
        gate = mod_ref[mod_base + 2:mod_base + 3, :]
        xn = x_ref[...] + MACARON_W * gate * acc_scr[...]
        if epilogue == "prenorm":
            xo_ref[...] = xn
            shift2 = mod_ref[mod_base + 3:mod_base + 4, :]
            scale2 = mod_ref[mod_base + 4:mod_base + 5, :]
            ho_ref[...] = (_rms(xn) * nw2_ref[...] * (1.0 + scale2) + shift2).astype(BF16)
        elif epilogue == "final":
            xo_ref[...] = _rms(xn) * nw2_ref[...]
        else:
            xo_ref[...] = xn


def _ffn(x2d, mod, nw, nw2, wia, wib, wo, *, tokens_per_batch, mod_base, epilogue):
    m, d = x2d.shape
    fpad = wia.shape[1]
    assert m % FFN_TM == 0 and tokens_per_batch % FFN_TM == 0 and fpad % FFN_TF == 0
    tiles_per_batch = tokens_per_batch // FFN_TM
    row_spec = pl.BlockSpec((FFN_TM, d), lambda i, f: (i, 0))
    vec_spec = pl.BlockSpec((1, d), lambda i, f: (0, 0))
    out_shape = [jax.ShapeDtypeStruct((m, d), F32)]
    out_specs = [row_spec]
    if epilogue == "prenorm":
        out_shape.append(jax.ShapeDtypeStruct((m, d), BF16))
        out_specs.append(row_spec)
    return pl.pallas_call(
        functools.partial(_ffn_kernel, mod_base=mod_base, epilogue=epilogue),
        grid=(m // FFN_TM, fpad // FFN_TF),
        in_specs=[
            row_spec,
            pl.BlockSpec((None, N_MOD, d), lambda i, f: (i // tiles_per_batch, 0, 0)),
            vec_spec,
            vec_spec,
            pl.BlockSpec((d, FFN_TF), lambda i, f: (0, f)),
            pl.BlockSpec((d, FFN_TF), lambda i, f: (0, f)),
            pl.BlockSpec((FFN_TF, d), lambda i, f: (f, 0)),
        ],
        out_specs=out_specs,
        out_shape=out_shape,
        scratch_shapes=[pltpu.VMEM((FFN_TM, d), BF16), pltpu.VMEM((FFN_TM, d), F32)],
        compiler_params=pltpu.CompilerParams(
            dimension_semantics=("arbitrary", "arbitrary"),
            vmem_limit_bytes=VMEM_LIMIT_BYTES),
        name="ffn_" + epilogue,
    )(x2d, mod, nw, nw2, wia, wib, wo)


_LEVEL_HALVES = tuple(CHUNK >> (j + 1) for j in range(CHUNK.bit_length() - 1))
_N_LEVELS = len(_LEVEL_HALVES)
_CS_ROWS = (_N_LEVELS + 2) * CHUNK


def _chunk_constants():
    t = np.arange(CHUNK)
    tri = (t[None, :] <= t[:, None]).astype(np.float32)
    blocks = [tri]
    masks = []
    for half in _LEVEL_HALVES:
        ref = (t // (2 * half)) * (2 * half) + half
        blocks.append(tri - tri[ref])
        same_block = (t[:, None] // (2 * half)) == (t[None, :] // (2 * half))
        is_query = (t % (2 * half)) >= half
        masks.append((same_block & is_query[:, None] & ~is_query[None, :]).astype(np.float32))
    blocks.append(1.0 - tri)
    masks.append(np.eye(CHUNK, dtype=np.float32))
    return (jnp.asarray(np.concatenate(blocks, axis=0), dtype=BF16),
            jnp.asarray(np.stack(masks, axis=0), dtype=F32))


def _mixer_kernel(*refs, kind, layer, dk, dv, tc):
    if kind == "gla":
        h_ref, w_ref, w2_ref, b2_ref, nw_ref, cm_ref, mk_ref, o_ref, z_scr, s_scr = refs
    else:
        h_ref, w_ref, lb_ref, nw_ref, cm_ref, mk_ref, o_ref, z_scr, s_scr = refs

    @pl.when(pl.program_id(2) == 0)
    def _():
        s_scr[...] = jnp.zeros_like(s_scr)

    z_scr[...] = jnp.dot(h_ref[...], w_ref[...], preferred_element_type=F32)

    if kind == "hgrn":
        raw = lb_ref[...]
        e = jnp.exp(raw - jnp.max(raw, axis=0, keepdims=True))
        p = e / jnp.sum(e, axis=0, keepdims=True)
        lb = jnp.sum(p[0:layer + 1, :], axis=0, keepdims=True)

    def chunk_body(c, carry):
        rows = pl.ds(pl.multiple_of(c * CHUNK, CHUNK), CHUNK)
        if kind == "gla":
            q = z_scr[rows, 0:dk] * (dk ** -0.5)
            k = z_scr[rows, dk:2 * dk]
            v = z_scr[rows, 2 * dk:2 * dk + dv]
            gate = z_scr[rows, 2 * dk + dv:2 * dk + 2 * dv]
            gr = z_scr[rows, 2 * dk + 2 * dv:2 * dk + 2 * dv + LANES]
            gp = jnp.dot(gr.astype(BF16), w2_ref[...], preferred_element_type=F32) + b2_ref[...]
            la = (jnp.minimum(gp, 0.0) - jnp.log(1.0 + jnp.exp(-jnp.abs(gp)))) * (
                1.0 / GLA_GATE_NORMALIZER)
        else:
            hq = z_scr[rows, 0:dk]
            fr = z_scr[rows, dk:2 * dk]
            v = z_scr[rows, 2 * dk:2 * dk + dv]
            gate = z_scr[rows, 2 * dk + dv:2 * dk + 2 * dv]
            q = hq * _sigmoid(hq)
            en = jnp.exp(-jnp.abs(fr))
            rr = 1.0 / (1.0 + en)
            sig_pos = jnp.where(fr >= 0.0, rr, en * rr)
            sig_neg = jnp.where(fr >= 0.0, en * rr, rr)
            la = jnp.log(lb + (1.0 - lb) * sig_pos)
            k = (1.0 - lb) * sig_neg

        la_hi = la.astype(BF16)
        la_lo = (la - la_hi.astype(F32)).astype(BF16)
        cm = cm_ref[...]
        cs = (jnp.dot(cm, la_hi, preferred_element_type=F32)
              + jnp.dot(cm, la_lo, preferred_element_type=F32))
        b = cs[0:CHUNK]

        attn = mk_ref[_N_LEVELS] * jnp.sum(q * k, axis=-1, keepdims=True)
        for j in range(_N_LEVELS):
            e = jnp.exp(-jnp.abs(cs[(j + 1) * CHUNK:(j + 2) * CHUNK]))
            s = lax.dot_general((q * e).astype(BF16), (k * e).astype(BF16), _NT,
                                preferred_element_type=F32)
            attn = attn + mk_ref[j] * s

        qb = (q * jnp.exp(b)).astype(BF16)
        kd = (k * jnp.exp(cs[(_N_LEVELS + 1) * CHUNK:(_N_LEVELS + 2) * CHUNK])).astype(BF16)
        vb = v.astype(BF16)
        st = s_scr[...]
        o = (lax.dot_general(qb, st.astype(BF16), _NT, preferred_element_type=F32)
             + jnp.dot(attn.astype(BF16), vb, preferred_element_type=F32))
        s_scr[...] = (st * jnp.exp(b[CHUNK - 1:CHUNK, :])
                      + lax.dot_general(vb, kd, _TN, preferred_element_type=F32))

        y = _rms(o) * nw_ref[...]
        o_ref[rows, :] = (y * (gate * _sigmoid(gate))).astype(BF16)
        return carry

    lax.fori_loop(0, tc // CHUNK, chunk_body, 0)


def _mixer(h2, w_heads, extra, nw, consts, *, kind, layer, batch, tokens_per_batch, dk, dv):
    m, d = h2.shape
    n_heads, _, ncols = w_heads.shape
    tc = MIX_TC
    assert tokens_per_batch % tc == 0
    nt = tokens_per_batch // tc
    cm, mk = consts
    const2 = lambda b, h, t: (0, 0)
    in_specs = [
        pl.BlockSpec((tc, d), lambda b, h, t: (b * nt + t, 0)),
        pl.BlockSpec((None, d, ncols), lambda b, h, t: (h, 0, 0)),
    ]
    if kind == "gla":
        w2, b2 = extra
        in_specs += [
            pl.BlockSpec((None, LANES, dk), lambda b, h, t: (h, 0, 0)),
            pl.BlockSpec((None, 1, dk), lambda b, h, t: (h, 0, 0)),
        ]
    else:
        (lbraw,) = extra
        in_specs += [pl.BlockSpec((None, lbraw.shape[1], dk), lambda b, h, t: (h, 0, 0))]
    in_specs += [
        pl.BlockSpec((1, dv), const2),
        pl.BlockSpec(cm.shape, const2),
        pl.BlockSpec(mk.shape, lambda b, h, t: (0, 0, 0)),
    ]
    return pl.pallas_call(
        functools.partial(_mixer_kernel, kind=kind, layer=layer, dk=dk, dv=dv, tc=tc),
        grid=(batch, n_heads, nt),
        in_specs=in_specs,
        out_specs=pl.BlockSpec((tc, dv), lambda b, h, t: (b * nt + t, h)),
        out_shape=jax.ShapeDtypeStruct((m, n_heads * dv), BF16),
        scratch_shapes=[pltpu.VMEM((tc, ncols), F32), pltpu.VMEM((dv, dk), F32)],
        compiler_params=pltpu.CompilerParams(
            dimension_semantics=("arbitrary", "arbitrary", "arbitrary"),
            vmem_limit_bytes=VMEM_LIMIT_BYTES),
        name="mixer_" + kind,
    )(h2, w_heads, *extra, nw, cm, mk)


def _outproj_kernel(x_ref, oa_ref, ob_ref, wa_ref, wb_ref, mod_ref, xo_ref, *, gate_row):
    y = (jnp.dot(oa_ref[...], wa_ref[...], preferred_element_type=F32)
         + jnp.dot(ob_ref[...], wb_ref[...], preferred_element_type=F32))
    xo_ref[...] = x_ref[...] + mod_ref[gate_row:gate_row + 1, :] * y


def _outproj(x2d, oa, ob, wa, wb, mod, *, tokens_per_batch, gate_row):
    m, d = x2d.shape
    tm = PROJ_TM
    tiles_per_batch = tokens_per_batch // tm
    return pl.pallas_call(
        functools.partial(_outproj_kernel, gate_row=gate_row),
        grid=(m // tm,),
        in_specs=[
            pl.BlockSpec((tm, d), lambda i: (i, 0)),
            pl.BlockSpec((tm, oa.shape[1]), lambda i: (i, 0)),
            pl.BlockSpec((tm, ob.shape[1]), lambda i: (i, 0)),
            pl.BlockSpec(wa.shape, lambda i: (0, 0)),
            pl.BlockSpec(wb.shape, lambda i: (0, 0)),
            pl.BlockSpec((None, N_MOD, d), lambda i: (i // tiles_per_batch, 0, 0)),
        ],
        out_specs=pl.BlockSpec((tm, d), lambda i: (i, 0)),
        out_shape=jax.ShapeDtypeStruct((m, d), F32),
        compiler_params=pltpu.CompilerParams(
            dimension_semantics=("arbitrary",), vmem_limit_bytes=VMEM_LIMIT_BYTES),
        name="outproj",
    )(x2d, oa, ob, wa, wb, mod)


def _prep_ffn_weights(wi, wo):
    d_ff = wo.shape[0]
    fpad = -(-d_ff // FFN_TF) * FFN_TF
    pad = fpad - d_ff
    wia = jnp.pad(wi[:, :d_ff].astype(BF16), ((0, 0), (0, pad)))
    wib = jnp.pad(wi[:, d_ff:].astype(BF16), ((0, 0), (0, pad)))
    wob = jnp.pad(wo.astype(BF16), ((0, pad), (0, 0)))
    return wia, wib, wob


def _prep_mixer_weights(w_in, w2, b2):
    d = w_in.shape[0]
    gla_qk = GLA_HEADS * GLA_DK
    gla_v = GLA_HEADS * GLA_DV
    hg_k = HGRN_HEADS * HGRN_DK
    hg_v = HGRN_HEADS * HGRN_DV
    offs = np.cumsum([0, gla_qk, gla_qk, gla_v, gla_v, GLA_GATE_RANK, hg_k, hg_k, hg_v, hg_v])
    wb = w_in.astype(BF16)

    def part(i, heads):
        return wb[:, offs[i]:offs[i + 1]].reshape(d, heads, -1).transpose(1, 0, 2)

    gr = jnp.pad(wb[:, offs[4]:offs[5]], ((0, 0), (0, LANES - GLA_GATE_RANK)))
    gr = jnp.broadcast_to(gr[None], (GLA_HEADS, d, LANES))
    w_gla = jnp.concatenate([part(0, GLA_HEADS), part(1, GLA_HEADS), part(2, GLA_HEADS),
                             part(3, GLA_HEADS), gr], axis=-1)
    w_hg = jnp.concatenate([part(5, HGRN_HEADS), part(6, HGRN_HEADS), part(7, HGRN_HEADS),
                            part(8, HGRN_HEADS)], axis=-1)
    w2h = jnp.pad(w2.astype(BF16), ((0, LANES - GLA_GATE_RANK), (0, 0)))
    w2h = w2h.reshape(LANES, GLA_HEADS, GLA_DK).transpose(1, 0, 2)
    b2h = b2.reshape(GLA_HEADS, 1, GLA_DK)
    return w_gla, w_hg, w2h, b2h


def kernel(x, c, ada_w, ada_b, norm_ffn1_w, ffn1_wi, ffn1_wo, norm_mix_w, w_in, gla_gate_w2,
           gla_gate_b2, gla_norm_w, hgrn_norm_w, hgrn_lower_bounds, w_out, norm_ffn2_w, ffn2_wi,
           ffn2_wo, final_norm_w):
    batch, seq, d = x.shape
    depth = ada_w.shape[0]
    m = batch * seq
    consts = _chunk_constants()
    xc = x.reshape(m, d)
    c_pad = jnp.pad(c, ((0, SUBLANES - batch % SUBLANES), (0, 0))) if batch % SUBLANES else c
    gla_v = GLA_HEADS * GLA_DV
    lb_heads = hgrn_lower_bounds.astype(F32).reshape(depth + 1, HGRN_HEADS, HGRN_DK).transpose(1, 0, 2)

    for l in range(depth):
        mod = _adaln(c_pad, ada_w[l], ada_b[l][None, :])[:batch].reshape(batch, N_MOD, d)
        wia1, wib1, wo1 = _prep_ffn_weights(ffn1_wi[l], ffn1_wo[l])
        wia2, wib2, wo2 = _prep_ffn_weights(ffn2_wi[l], ffn2_wo[l])
        w_gla, w_hg, w2h, b2h = _prep_mixer_weights(w_in[l], gla_gate_w2[l], gla_gate_b2[l])
        wout = w_out[l].astype(BF16)

        x1, h2 = _ffn(xc, mod, norm_ffn1_w[l][None, :], norm_mix_w[l][None, :], wia1, wib1, wo1,
                      tokens_per_batch=seq, mod_base=0, epilogue="prenorm")
        o_gla = _mixer(h2, w_gla, (w2h, b2h), gla_norm_w[l][None, :], consts, kind="gla",
                       layer=l, batch=batch, tokens_per_batch=seq, dk=GLA_DK, dv=GLA_DV)
        o_hg = _mixer(h2, w_hg, (lb_heads,), hgrn_norm_w[l][None, :], consts, kind="hgrn",
                      layer=l, batch=batch, tokens_per_batch=seq, dk=HGRN_DK, dv=HGRN_DV)
        x2 = _outproj(x1, o_gla, o_hg, wout[:gla_v], wout[gla_v:], mod,
                      tokens_per_batch=seq, gate_row=5)
        last = l == depth - 1
        nw2 = final_norm_w[None, :] if last else norm_ffn2_w[l][None, :]
        (xc,) = _ffn(x2, mod, norm_ffn2_w[l][None, :], nw2, wia2, wib2, wo2,
                     tokens_per_batch=seq, mod_base=6, epilogue="final" if last else "none")
    return xc.reshape(batch, seq, d)
```

```python
import functools

import jax
import jax.numpy as jnp
import numpy as np
from jax import lax
from jax.experimental import pallas as pl
from jax.experimental.pallas import tpu as pltpu

F32 = jnp.float32
BF16 = jnp.bfloat16

GLA_HEADS = 4
GLA_DK = 128
GLA_DV = 256
GLA_GATE_RANK = 16
GLA_GATE_NORMALIZER = 16.0
HGRN_HEADS = 8
HGRN_DK = 128
HGRN_DV = 128
CHUNK = 64
MACARON_W = 0.5
N_MOD = 9
EPS = 1e-6

LANES = 128
SUBLANES = 8
VMEM_LIMIT_BYTES = 56 * 1024 * 1024

FFN_TM = 512
FFN_TF = 512
MIX_TC = 512
PROJ_TM = 512
ADALN_TN = 1024

_NT = (((1,), (1,)), ((), ()))
_TN = (((0,), (0,)), ((), ()))


def _sigmoid(x):
    return jax.nn.sigmoid(x)


def _rms(x):
    return x * lax.rsqrt(jnp.mean(x * x, axis=-1, keepdims=True) + EPS)


def _adaln_kernel(c_ref, w_ref, b_ref, o_ref):
    c = c_ref[...]
    ca = (c * _sigmoid(c)).astype(BF16)
    o_ref[...] = jnp.dot(ca, w_ref[...].astype(BF16), preferred_element_type=F32) + b_ref[...]


def _adaln(c_pad, w, b):
    rows, d = c_pad.shape
    n = w.shape[1]
    assert n % ADALN_TN == 0
    return pl.pallas_call(
        _adaln_kernel,
        grid=(n // ADALN_TN,),
        in_specs=[
            pl.BlockSpec((rows, d), lambda j: (0, 0)),
            pl.BlockSpec((d, ADALN_TN), lambda j: (0, j)),
            pl.BlockSpec((1, ADALN_TN), lambda j: (0, j)),
        ],
        out_specs=pl.BlockSpec((rows, ADALN_TN), lambda j: (0, j)),
        out_shape=jax.ShapeDtypeStruct((rows, n), F32),
        compiler_params=pltpu.CompilerParams(
            dimension_semantics=("arbitrary",), vmem_limit_bytes=VMEM_LIMIT_BYTES),
        name="adaln",
    )(c_pad, w, b)


def _ffn_kernel(x_ref, mod_ref, nw_ref, nw2_ref, wia_ref, wib_ref, wo_ref, *rest,
                mod_base, epilogue):
    if epilogue == "prenorm":
        xo_ref, ho_ref, h_scr, acc_scr = rest
    else:
        xo_ref, h_scr, acc_scr = rest
    f = pl.program_id(1)

    @pl.when(f == 0)
    def _():
        y = _rms(x_ref[...]) * nw_ref[...]
        shift = mod_ref[mod_base:mod_base + 1, :]
        scale = mod_ref[mod_base + 1:mod_base + 2, :]
        h_scr[...] = (y * (1.0 + scale) + shift).astype(BF16)
        acc_scr[...] = jnp.zeros_like(acc_scr)

    h = h_scr[...]
    a = jnp.dot(h, wia_ref[...], preferred_element_type=F32)
    b = jnp.dot(h, wib_ref[...], preferred_element_type=F32)
    act = (a * _sigmoid(a) * b).astype(BF16)
    acc_scr[...] += jnp.dot(act, wo_ref[...], preferred_element_type=F32)

    @pl.when(f == pl.num_programs(1) - 1)
    def _():
        gate = mod_ref[mod_base + 2:mod_base + 3, :]
        xn = x_ref[...] + MACARON_W * gate * acc_scr[...]
        if epilogue == "prenorm":
            xo_ref[...] = xn
            shift2 = mod_ref[mod_base + 3:mod_base + 4, :]
            scale2 = mod_ref[mod_base + 4:mod_base + 5, :]
            ho_ref[...] = (_rms(xn) * nw2_ref[...] * (1.0 + scale2) + shift2).astype(BF16)
        elif epilogue == "final":
            xo_ref[...] = _rms(xn) * nw2_ref[...]
        else:
            xo_ref[...] = xn


def _ffn(x2d, mod, nw, nw2, wia, wib, wo, *, tokens_per_batch, mod_base, epilogue):
    m, d = x2d.shape
    fpad = wia.shape[1]
    assert m % FFN_TM == 0 and tokens_per_batch % FFN_TM == 0 and fpad % FFN_TF == 0
    tiles_per_batch = tokens_per_batch // FFN_TM
    row_spec = pl.BlockSpec((FFN_TM, d), lambda i, f: (i, 0))
    vec_spec = pl.BlockSpec((1, d), lambda i, f: (0, 0))
    out_shape = [jax.ShapeDtypeStruct((m, d), F32)]
    out_specs = [row_spec]
    if epilogue == "prenorm":
        out_shape.append(jax.ShapeDtypeStruct((m, d), BF16))
        out_specs.append(row_spec)
    return pl.pallas_call(
        functools.partial(_ffn_kernel, mod_base=mod_base, epilogue=epilogue),
        grid=(m // FFN_TM, fpad // FFN_TF),
        in_specs=[
            row_spec,
            pl.BlockSpec((None, N_MOD, d), lambda i, f: (i // tiles_per_batch, 0, 0)),
            vec_spec,
            vec_spec,
            pl.BlockSpec((d, FFN_TF), lambda i, f: (0, f)),
            pl.BlockSpec((d, FFN_TF), lambda i, f: (0, f)),
            pl.BlockSpec((FFN_TF, d), lambda i, f: (f, 0)),
        ],
        out_specs=out_specs,
        out_shape=out_shape,
        scratch_shapes=[pltpu.VMEM((FFN_TM, d), BF16), pltpu.VMEM((FFN_TM, d), F32)],
        compiler_params=pltpu.CompilerParams(
            dimension_semantics=("arbitrary", "arbitrary"),
            vmem_limit_bytes=VMEM_LIMIT_BYTES),
        name="ffn_" + epilogue,
    )(x2d, mod, nw, nw2, wia, wib, wo)


_LEVEL_HALVES = tuple(CHUNK >> (j + 1) for j in range(CHUNK.bit_length() - 1))
_N_LEVELS = len(_LEVEL_HALVES)
_CS_BLOCKS = _N_LEVELS + 2
_GROUP = 4
_GROUP_ROWS = _GROUP * CHUNK


def _chunk_constants():
    t = np.arange(CHUNK)
    tri = (t[None, :] <= t[:, None]).astype(np.float32)
    blocks = [tri]
    masks = []
    for half in _LEVEL_HALVES:
        ref = (t // (2 * half)) * (2 * half) + half
        blocks.append(tri - tri[ref])
        same_block = (t[:, None] // (2 * half)) == (t[None, :] // (2 * half))
        is_query = (t % (2 * half)) >= half
        masks.append((same_block & is_query[:, None] & ~is_query[None, :]).astype(np.float32))
    blocks.append(1.0 - tri)
    masks.append(np.eye(CHUNK, dtype=np.float32))
    cm = np.concatenate(blocks, axis=0)
    cm2 = np.concatenate([cm, cm], axis=1)
    group_masks = np.stack([np.kron(np.eye(_GROUP, dtype=np.float32), m) for m in masks], axis=0)
    return jnp.asarray(cm2, dtype=BF16), jnp.asarray(group_masks, dtype=F32)


def _mixer_kernel(*refs, kind, layer, dk, dv, tc):
    if kind == "gla":
        (h_ref, w_ref, w2_ref, b2_ref, nw_ref, cm_ref, mk_ref, o_ref,
         z_scr, q_scr, k_scr, cs_scr, oi_scr, u_scr, s_scr) = refs
    else:
        (h_ref, w_ref, lb_ref, nw_ref, cm_ref, mk_ref, o_ref,
         z_scr, q_scr, k_scr, cs_scr, oi_scr, u_scr, s_scr) = refs
    n_chunks = tc // CHUNK
    v_cols = slice(2 * dk, 2 * dk + dv)
    gate_cols = slice(2 * dk + dv, 2 * dk + 2 * dv)

    @pl.when(pl.program_id(2) == 0)
    def _():
        s_scr[...] = jnp.zeros_like(s_scr)

    z_scr[...] = jnp.dot(h_ref[...], w_ref[...], preferred_element_type=F32)

    if kind == "gla":
        q_scr[...] = z_scr[:, 0:dk] * (dk ** -0.5)
        k_scr[...] = z_scr[:, dk:2 * dk]
        gr = z_scr[:, 2 * dk + 2 * dv:2 * dk + 2 * dv + LANES]
        gp = jnp.dot(gr.astype(BF16), w2_ref[...], preferred_element_type=F32) + b2_ref[...]
        la = (jnp.minimum(gp, 0.0) - jnp.log(1.0 + jnp.exp(-jnp.abs(gp)))) * (
            1.0 / GLA_GATE_NORMALIZER)
    else:
        raw = lb_ref[...]
        ex = jnp.exp(raw - jnp.max(raw, axis=0, keepdims=True))
        p = ex / jnp.sum(ex, axis=0, keepdims=True)
        lb = jnp.sum(p[0:layer + 1, :], axis=0, keepdims=True)
        hq = z_scr[:, 0:dk]
        fr = z_scr[:, dk:2 * dk]
        q_scr[...] = hq * _sigmoid(hq)
        en = jnp.exp(-jnp.abs(fr))
        rr = 1.0 / (1.0 + en)
        sig_pos = jnp.where(fr >= 0.0, rr, en * rr)
        sig_neg = jnp.where(fr >= 0.0, en * rr, rr)
        la = jnp.log(lb + (1.0 - lb) * sig_pos)
        k_scr[...] = (1.0 - lb) * sig_neg

    la_wide = jnp.concatenate([la[c * CHUNK:(c + 1) * CHUNK] for c in range(n_chunks)], axis=1)
    la_hi = la_wide.astype(BF16)
    la_lo = (la_wide - la_hi.astype(F32)).astype(BF16)
    cs_scr[...] = jnp.dot(cm_ref[...], jnp.concatenate([la_hi, la_lo], axis=0),
                          preferred_element_type=F32)

    def cs_block(block, c):
        return cs_scr[block * CHUNK:(block + 1) * CHUNK, c * dk:(c + 1) * dk]

    for g in range(n_chunks // _GROUP):
        chunks = range(g * _GROUP, (g + 1) * _GROUP)
        rows = slice(g * _GROUP_ROWS, (g + 1) * _GROUP_ROWS)
        q = q_scr[rows, :]
        k = k_scr[rows, :]
        attn = mk_ref[_N_LEVELS] * jnp.sum(q * k, axis=-1, keepdims=True)
        for j in range(_N_LEVELS):
            d = jnp.concatenate([cs_block(j + 1, c) for c in chunks], axis=0)
            e = jnp.exp(-jnp.abs(d))
            s = lax.dot_general((q * e).astype(BF16), (k * e).astype(BF16), _NT,
                                preferred_element_type=F32)
            attn = attn + mk_ref[j] * s
        vb = z_scr[rows, v_cols].astype(BF16)
        oi_scr[rows, :] = jnp.dot(attn.astype(BF16), vb, preferred_element_type=F32)
        for i, c in enumerate(chunks):
            crow = slice(c * CHUNK, (c + 1) * CHUNK)
            kd = (k_scr[crow, :] * jnp.exp(cs_block(_N_LEVELS + 1, c))).astype(BF16)
            u_scr[c] = lax.dot_general(vb[i * CHUNK:(i + 1) * CHUNK], kd, _TN,
                                       preferred_element_type=F32)

    st = s_scr[...]
    for c in range(n_chunks):
        rows = slice(c * CHUNK, (c + 1) * CHUNK)
        b = cs_block(0, c)
        qb = (q_scr[rows, :] * jnp.exp(b)).astype(BF16)
        o = lax.dot_general(qb, st.astype(BF16), _NT, preferred_element_type=F32) + oi_scr[rows, :]
        st = st * jnp.exp(b[CHUNK - 1:CHUNK, :]) + u_scr[c]
        gate = z_scr[rows, gate_cols]
        o_ref[rows, :] = (_rms(o) * nw_ref[...] * (gate * _sigmoid(gate))).astype(BF16)
    s_scr[...] = st


def _mixer(h2, w_heads, extra, nw, consts, *, kind, layer, batch, tokens_per_batch, dk, dv):
    m, d = h2.shape
    n_heads, _, ncols = w_heads.shape
    tc = MIX_TC
    assert tokens_per_batch % tc == 0 and tc % _GROUP_ROWS == 0
    nt = tokens_per_batch // tc
    cm, mk = consts
    const2 = lambda b, h, t: (0, 0)
    in_specs = [
        pl.BlockSpec((tc, d), lambda b, h, t: (b * nt + t, 0)),
        pl.BlockSpec((None, d, ncols), lambda b, h, t: (h, 0, 0)),
    ]
    if kind == "gla":
        w2, b2 = extra
        in_specs += [
            pl.BlockSpec((None, LANES, dk), lambda b, h, t: (h, 0, 0)),
            pl.BlockSpec((None, 1, dk), lambda b, h, t: (h, 0, 0)),
        ]
    else:
        (lbraw,) = extra
        in_specs += [pl.BlockSpec((None, lbraw.shape[1], dk), lambda b, h, t: (h, 0, 0))]
    in_specs += [
        pl.BlockSpec((1, dv), const2),
        pl.BlockSpec(cm.shape, const2),
        pl.BlockSpec(mk.shape, lambda b, h, t: (0, 0, 0)),
    ]
    return pl.pallas_call(
        functools.partial(_mixer_kernel, kind=kind, layer=layer, dk=dk, dv=dv, tc=tc),
        grid=(batch, n_heads, nt),
        in_specs=in_specs,
        out_specs=pl.BlockSpec((tc, dv), lambda b, h, t: (b * nt + t, h)),
        out_shape=jax.ShapeDtypeStruct((m, n_heads * dv), BF16),
        scratch_shapes=[
            pltpu.VMEM((tc, ncols), F32),
            pltpu.VMEM((tc, dk), F32),
            pltpu.VMEM((tc, dk), F32),
            pltpu.VMEM((_CS_BLOCKS * CHUNK, (tc // CHUNK) * dk), F32),
            pltpu.VMEM((tc, dv), F32),
            pltpu.VMEM((tc // CHUNK, dv, dk), F32),
            pltpu.VMEM((dv, dk), F32),
        ],
        compiler_params=pltpu.CompilerParams(
            dimension_semantics=("arbitrary", "arbitrary", "arbitrary"),
            vmem_limit_bytes=VMEM_LIMIT_BYTES),
        name="mixer_" + kind,
    )(h2, w_heads, *extra, nw, cm, mk)


def _outproj_kernel(x_ref, oa_ref, ob_ref, wa_ref, wb_ref, mod_ref, xo_ref, *, gate_row):
    y = (jnp.dot(oa_ref[...], wa_ref[...], preferred_element_type=F32)
         + jnp.dot(ob_ref[...], wb_ref[...], preferred_element_type=F32))
    xo_ref[...] = x_ref[...] + mod_ref[gate_row:gate_row + 1, :] * y


def _outproj(x2d, oa, ob, wa, wb, mod, *, tokens_per_batch, gate_row):
    m, d = x2d.shape
    tm = PROJ_TM
    tiles_per_batch = tokens_per_batch // tm
    return pl.pallas_call(
        functools.partial(_outproj_kernel, gate_row=gate_row),
        grid=(m // tm,),
        in_specs=[
            pl.BlockSpec((tm, d), lambda i: (i, 0)),
            pl.BlockSpec((tm, oa.shape[1]), lambda i: (i, 0)),
            pl.BlockSpec((tm, ob.shape[1]), lambda i: (i, 0)),
            pl.BlockSpec(wa.shape, lambda i: (0, 0)),
            pl.BlockSpec(wb.shape, lambda i: (0, 0)),
            pl.BlockSpec((None, N_MOD, d), lambda i: (i // tiles_per_batch, 0, 0)),
        ],
        out_specs=pl.BlockSpec((tm, d), lambda i: (i, 0)),
        out_shape=jax.ShapeDtypeStruct((m, d), F32),
        compiler_params=pltpu.CompilerParams(
            dimension_semantics=("arbitrary",), vmem_limit_bytes=VMEM_LIMIT_BYTES),
        name="outproj",
    )(x2d, oa, ob, wa, wb, mod)


def _prep_ffn_weights(wi, wo):
    d_ff = wo.shape[0]
    fpad = -(-d_ff // FFN_TF) * FFN_TF
    pad = fpad - d_ff
    wia = jnp.pad(wi[:, :d_ff].astype(BF16), ((0, 0), (0, pad)))
    wib = jnp.pad(wi[:, d_ff:].astype(BF16), ((0, 0), (0, pad)))
    wob = jnp.pad(wo.astype(BF16), ((0, pad), (0, 0)))
    return wia, wib, wob


def _prep_mixer_weights(w_in, w2, b2):
    d = w_in.shape[0]
    gla_qk = GLA_HEADS * GLA_DK
    gla_v = GLA_HEADS * GLA_DV
    hg_k = HGRN_HEADS * HGRN_DK
    hg_v = HGRN_HEADS * HGRN_DV
    offs = np.cumsum([0, gla_qk, gla_qk, gla_v, gla_v, GLA_GATE_RANK, hg_k, hg_k, hg_v, hg_v])
    wb = w_in.astype(BF16)

    def part(i, heads):
        return wb[:, offs[i]:offs[i + 1]].reshape(d, heads, -1).transpose(1, 0, 2)

    gr = jnp.pad(wb[:, offs[4]:offs[5]], ((0, 0), (0, LANES - GLA_GATE_RANK)))
    gr = jnp.broadcast_to(gr[None], (GLA_HEADS, d, LANES))
    w_gla = jnp.concatenate([part(0, GLA_HEADS), part(1, GLA_HEADS), part(2, GLA_HEADS),
                             part(3, GLA_HEADS), gr], axis=-1)
    w_hg = jnp.concatenate([part(5, HGRN_HEADS), part(6, HGRN_HEADS), part(7, HGRN_HEADS),
                            part(8, HGRN_HEADS)], axis=-1)
    w2h = jnp.pad(w2.astype(BF16), ((0, LANES - GLA_GATE_RANK), (0, 0)))
    w2h = w2h.reshape(LANES, GLA_HEADS, GLA_DK).transpose(1, 0, 2)
    b2h = b2.reshape(GLA_HEADS, 1, GLA_DK)
    return w_gla, w_hg, w2h, b2h


def kernel(x, c, ada_w, ada_b, norm_ffn1_w, ffn1_wi, ffn1_wo, norm_mix_w, w_in, gla_gate_w2,
           gla_gate_b2, gla_norm_w, hgrn_norm_w, hgrn_lower_bounds, w_out, norm_ffn2_w, ffn2_wi,
           ffn2_wo, final_norm_w):
    batch, seq, d = x.shape
    depth = ada_w.shape[0]
    m = batch * seq
    consts = _chunk_constants()
    xc = x.reshape(m, d)
    c_pad = jnp.pad(c, ((0, SUBLANES - batch % SUBLANES), (0, 0))) if batch % SUBLANES else c
    gla_v = GLA_HEADS * GLA_DV
    lb_heads = hgrn_lower_bounds.astype(F32).reshape(depth + 1, HGRN_HEADS, HGRN_DK).transpose(1, 0, 2)

    for l in range(depth):
        mod = _adaln(c_pad, ada_w[l], ada_b[l][None, :])[:batch].reshape(batch, N_MOD, d)
        wia1, wib1, wo1 = _prep_ffn_weights(ffn1_wi[l], ffn1_wo[l])
        wia2, wib2, wo2 = _prep_ffn_weights(ffn2_wi[l], ffn2_wo[l])
        w_gla, w_hg, w2h, b2h = _prep_mixer_weights(w_in[l], gla_gate_w2[l], gla_gate_b2[l])
        wout = w_out[l].astype(BF16)

        x1, h2 = _ffn(xc, mod, norm_ffn1_w[l][None, :], norm_mix_w[l][None, :], wia1, wib1, wo1,
                      tokens_per_batch=seq, mod_base=0, epilogue="prenorm")
        o_gla = _mixer(h2, w_gla, (w2h, b2h), gla_norm_w[l][None, :], consts, kind="gla",
                       layer=l, batch=batch, tokens_per_batch=seq, dk=GLA_DK, dv=GLA_DV)
        o_hg = _mixer(h2, w_hg, (lb_heads,), hgrn_norm_w[l][None, :], consts, kind="hgrn",
                      layer=l, batch=batch, tokens_per_batch=seq, dk=HGRN_DK, dv=HGRN_DV)
        x2 = _outproj(x1, o_gla, o_hg, wout[:gla_v], wout[gla_v:], mod,
                      tokens_per_batch=seq, gate_row=5)
        last = l == depth - 1
        nw2 = final_norm_w[None, :] if last else norm_ffn2_w[l][None, :]
        (xc,) = _ffn(x2, mod, norm_ffn2_w[l][None, :], nw2, wia2, wib2, wo2,
                     tokens_per_batch=seq, mod_base=6, epilogue="final" if last else "none")
    return xc.reshape(batch, seq, d)
```

```python
import functools

import jax
import jax.numpy as jnp
import numpy as np
from jax import lax
from jax.experimental import pallas as pl
from jax.experimental.pallas import tpu as pltpu

F32 = jnp.float32
BF16 = jnp.bfloat16

GLA_HEADS = 4
GLA_DK = 128
GLA_DV = 256
GLA_GATE_RANK = 16
GLA_GATE_NORMALIZER = 16.0
HGRN_HEADS = 8
HGRN_DK = 128
HGRN_DV = 128
CHUNK = 64
MACARON_W = 0.5
N_MOD = 9
EPS = 1e-6

LANES = 128
SUBLANES = 8
MXU_N = 256
VMEM_LIMIT_BYTES = 56 * 1024 * 1024

FFN_TM = 512
FFN_TF = 512
PROLOGUE_ROWS = 256
MIX_TC = 512
ADALN_TN = 1024
WPREP_ROWS = 256

_NT = (((1,), (1,)), ((), ()))
_TN = (((0,), (0,)), ((), ()))


def _sigmoid(x):
    return jax.nn.sigmoid(x)


def _rms(x):
    return x * lax.rsqrt(jnp.mean(x * x, axis=-1, keepdims=True) + EPS)


def _adaln_kernel(c_ref, w_ref, b_ref, o_ref):
    c = c_ref[...]
    ca = (c * _sigmoid(c)).astype(BF16)
    o_ref[...] = jnp.dot(ca, w_ref[...].astype(BF16), preferred_element_type=F32) + b_ref[...]


def _adaln(c_pad, w, b):
    rows, d = c_pad.shape
    n = w.shape[1]
    assert n % ADALN_TN == 0
    return pl.pallas_call(
        _adaln_kernel,
        grid=(n // ADALN_TN,),
        in_specs=[
            pl.BlockSpec((rows, d), lambda j: (0, 0)),
            pl.BlockSpec((d, ADALN_TN), lambda j: (0, j)),
            pl.BlockSpec((1, ADALN_TN), lambda j: (0, j)),
        ],
        out_specs=pl.BlockSpec((rows, ADALN_TN), lambda j: (0, j)),
        out_shape=jax.ShapeDtypeStruct((rows, n), F32),
        compiler_params=pltpu.CompilerParams(
            dimension_semantics=("arbitrary",), vmem_limit_bytes=VMEM_LIMIT_BYTES),
        name="adaln",
    )(c_pad, w, b)


def _ffn_kernel(*refs, mod_base, epilogue, mix_gate_row):
    x_ref, mod_ref, nw_ref, nw2_ref, wab_ref, wo_ref, *rest = refs
    if mix_gate_row is not None:
        oa_ref, ob_ref, wa_ref, wb_ref, *rest = rest
    if epilogue == "prenorm":
        xo_ref, ho_ref, h_scr, *rest = rest
    else:
        xo_ref, h_scr, *rest = rest
    xin_ref = rest[0] if mix_gate_row is not None else x_ref
    f = pl.program_id(1)

    @pl.when(f == 0)
    def _():
        shift = mod_ref[mod_base:mod_base + 1, :]
        gain = nw_ref[...] * (1.0 + mod_ref[mod_base + 1:mod_base + 2, :])
        for r0 in range(0, FFN_TM, PROLOGUE_ROWS):
            rows = slice(r0, r0 + PROLOGUE_ROWS)
            if mix_gate_row is not None:
                y = (jnp.dot(oa_ref[rows, :], wa_ref[...], preferred_element_type=F32)
                     + jnp.dot(ob_ref[rows, :], wb_ref[...], preferred_element_type=F32))
                xin_ref[rows, :] = x_ref[rows, :] + mod_ref[mix_gate_row:mix_gate_row + 1, :] * y
            h_scr[rows, :] = (_rms(xin_ref[rows, :]) * gain + shift).astype(BF16)
        xo_ref[...] = jnp.zeros_like(xo_ref)

    zab = jnp.dot(h_scr[...], wab_ref[...], preferred_element_type=F32)
    pieces = FFN_TF // MXU_N
    a = jnp.concatenate([zab[:, (2 * i) * MXU_N:(2 * i + 1) * MXU_N] for i in range(pieces)], axis=1)
    b = jnp.concatenate([zab[:, (2 * i + 1) * MXU_N:(2 * i + 2) * MXU_N] for i in range(pieces)],
                        axis=1)
    act = (a * _sigmoid(a) * b).astype(BF16)
    xo_ref[...] += jnp.dot(act, wo_ref[...], preferred_element_type=F32)

    @pl.when(f == pl.num_programs(1) - 1)
    def _():
        gate = mod_ref[mod_base + 2:mod_base + 3, :]
        xn = xin_ref[...] + MACARON_W * gate * xo_ref[...]
        if epilogue == "prenorm":
            xo_ref[...] = xn
            shift2 = mod_ref[mod_base + 3:mod_base + 4, :]
            scale2 = mod_ref[mod_base + 4:mod_base + 5, :]
            ho_ref[...] = (_rms(xn) * nw2_ref[...] * (1.0 + scale2) + shift2).astype(BF16)
        elif epilogue == "final":
            xo_ref[...] = _rms(xn) * nw2_ref[...]
        else:
            xo_ref[...] = xn


def _ffn(x2d, mod, nw, nw2, wab, wo, *, tokens_per_batch, mod_base, epilogue, mix=None,
         mix_gate_row=None):
    m, d = x2d.shape
    n_f = wab.shape[0]
    assert m % FFN_TM == 0 and tokens_per_batch % FFN_TM == 0 and wo.shape[0] == n_f * FFN_TF
    assert (mix is None) == (mix_gate_row is None)
    tiles_per_batch = tokens_per_batch // FFN_TM
    row_spec = pl.BlockSpec((FFN_TM, d), lambda i, f: (i, 0))
    vec_spec = pl.BlockSpec((1, d), lambda i, f: (0, 0))
    in_specs = [
        row_spec,
        pl.BlockSpec((None, N_MOD, d), lambda i, f: (i // tiles_per_batch, 0, 0)),
        vec_spec,
        vec_spec,
        pl.BlockSpec((None, d, 2 * FFN_TF), lambda i, f: (f, 0, 0)),
        pl.BlockSpec((FFN_TF, d), lambda i, f: (f, 0)),
    ]
    operands = [x2d, mod, nw, nw2, wab, wo]
    scratch = [pltpu.VMEM((FFN_TM, d), BF16)]
    if mix is not None:
        oa, ob, wa, wb = mix
        in_specs += [
            pl.BlockSpec((FFN_TM, oa.shape[1]), lambda i, f: (i, 0)),
            pl.BlockSpec((FFN_TM, ob.shape[1]), lambda i, f: (i, 0)),
            pl.BlockSpec(wa.shape, lambda i, f: (0, 0), pipeline_mode=pl.Buffered(1)),
            pl.BlockSpec(wb.shape, lambda i, f: (0, 0), pipeline_mode=pl.Buffered(1)),
        ]
        operands += [oa, ob, wa, wb]
        scratch.append(pltpu.VMEM((FFN_TM, d), F32))
    out_shape = [jax.ShapeDtypeStruct((m, d), F32)]
    out_specs = [row_spec]
    if epilogue == "prenorm":
        out_shape.append(jax.ShapeDtypeStruct((m, d), BF16))
        out_specs.append(row_spec)
    return pl.pallas_call(
        functools.partial(_ffn_kernel, mod_base=mod_base, epilogue=epilogue,
                          mix_gate_row=mix_gate_row),
        grid=(m // FFN_TM, n_f),
        in_specs=in_specs,
        out_specs=out_specs,
        out_shape=out_shape,
        scratch_shapes=scratch,
        compiler_params=pltpu.CompilerParams(
            dimension_semantics=("arbitrary", "arbitrary"),
            vmem_limit_bytes=VMEM_LIMIT_BYTES),
        name="ffn_" + epilogue,
    )(*operands)


_LEVEL_HALVES = tuple(CHUNK >> (j + 1) for j in range(CHUNK.bit_length() - 1))
_N_LEVELS = len(_LEVEL_HALVES)
_CS_BLOCKS = _N_LEVELS + 2
_GROUP = 4
_GROUP_ROWS = _GROUP * CHUNK


def _chunk_constants():
    t = np.arange(CHUNK)
    tri = (t[None, :] <= t[:, None]).astype(np.float32)
    blocks = [tri]
    masks = []
    for half in _LEVEL_HALVES:
        ref = (t // (2 * half)) * (2 * half) + half
        blocks.append(tri - tri[ref])
        same_block = (t[:, None] // (2 * half)) == (t[None, :] // (2 * half))
        is_query = (t % (2 * half)) >= half
        masks.append((same_block & is_query[:, None] & ~is_query[None, :]).astype(np.float32))
    blocks.append(1.0 - tri)
    masks.append(np.eye(CHUNK, dtype=np.float32))
    cm = np.concatenate(blocks, axis=0)
    cm2 = np.concatenate([cm, cm], axis=1)
    group_masks = np.stack([np.kron(np.eye(_GROUP, dtype=np.float32), m) for m in masks], axis=0)
    return jnp.asarray(cm2, dtype=BF16), jnp.asarray(group_masks, dtype=F32)


def _mixer_kernel(*refs, tiles_per_head, **static):
    *io_refs, z_a, z_b, q_scr, k_scr, cs_scr, oi_scr, u_scr, s_scr = refs
    scratch = (q_scr, k_scr, cs_scr, oi_scr, u_scr, s_scr)
    s = pl.program_id(0)

    @pl.when(s == 0)
    def _():
        z_b[...] = jnp.zeros_like(z_b)

    @pl.when(lax.rem(jnp.maximum(s - 1, 0), tiles_per_head) == 0)
    def _():
        s_scr[...] = jnp.zeros_like(s_scr)

    @pl.when(lax.rem(s, 2) == 0)
    def _():
        _mixer_tile(io_refs, scratch, z_b, z_a, **static)

    @pl.when(lax.rem(s, 2) == 1)
    def _():
        _mixer_tile(io_refs, scratch, z_a, z_b, **static)


def _mixer_tile(io_refs, scratch, z_scr, z_next, *, kind, layer, dk, dv, tc):
    if kind == "gla":
        h_ref, w_ref, w2_ref, b2_ref, nw_ref, cm_ref, mk_ref, o_ref = io_refs
    else:
        h_ref, w_ref, lb_ref, nw_ref, cm_ref, mk_ref, o_ref = io_refs
    q_scr, k_scr, cs_scr, oi_scr, u_scr, s_scr = scratch
    n_chunks = tc // CHUNK
    v_cols = slice(2 * dk, 2 * dk + dv)
    gate_cols = slice(2 * dk + dv, 2 * dk + 2 * dv)

    ncols = w_ref.shape[1]
    pieces = [slice(c0, min(c0 + MXU_N, ncols)) for c0 in range(0, ncols, MXU_N)]
    n_slots = n_chunks // _GROUP + 2

    def project_pieces(slot):
        lo = slot * len(pieces) // n_slots
        hi = (slot + 1) * len(pieces) // n_slots
        for cols in pieces[lo:hi]:
            z_next[:, cols] = jnp.dot(h_ref[...], w_ref[:, cols], preferred_element_type=F32)

    project_pieces(0)

    if kind == "gla":
        q_scr[...] = z_scr[:, 0:dk] * (dk ** -0.5)
        k_scr[...] = z_scr[:, dk:2 * dk]
        gr = z_scr[:, 2 * dk + 2 * dv:2 * dk + 2 * dv + LANES]
        gp = jnp.dot(gr.astype(BF16), w2_ref[...], preferred_element_type=F32) + b2_ref[...]
        la = (jnp.minimum(gp, 0.0) - jnp.log(1.0 + jnp.exp(-jnp.abs(gp)))) * (
            1.0 / GLA_GATE_NORMALIZER)
    else:
        raw = lb_ref[...]
        ex = jnp.exp(raw - jnp.max(raw, axis=0, keepdims=True))
        p = ex / jnp.sum(ex, axis=0, keepdims=True)
        lb = jnp.sum(p[0:layer + 1, :], axis=0, keepdims=True)
        hq = z_scr[:, 0:dk]
        fr = z_scr[:, dk:2 * dk]
        q_scr[...] = hq * _sigmoid(hq)
        en = jnp.exp(-jnp.abs(fr))
        rr = 1.0 / (1.0 + en)
        sig_pos = jnp.where(fr >= 0.0, rr, en * rr)
        sig_neg = jnp.where(fr >= 0.0, en * rr, rr)
        la = jnp.log(lb + (1.0 - lb) * sig_pos)
        k_scr[...] = (1.0 - lb) * sig_neg

    la_wide = jnp.concatenate([la[c * CHUNK:(c + 1) * CHUNK] for c in range(n_chunks)], axis=1)
    la_hi = la_wide.astype(BF16)
    la_lo = (la_wide - la_hi.astype(F32)).astype(BF16)
    cs_scr[...] = jnp.dot(cm_ref[...], jnp.concatenate([la_hi, la_lo], axis=0),
                          preferred_element_type=F32)

    def cs_block(block, c):
        return cs_scr[block * CHUNK:(block + 1) * CHUNK, c * dk:(c + 1) * dk]

    for g in range(n_chunks // _GROUP):
        project_pieces(g + 1)
        chunks = range(g * _GROUP, (g + 1) * _GROUP)
        rows = slice(g * _GROUP_ROWS, (g + 1) * _GROUP_ROWS)
        q = q_scr[rows, :]
        k = k_scr[rows, :]
        attn = mk_ref[_N_LEVELS] * jnp.sum(q * k, axis=-1, keepdims=True)
        for j in range(_N_LEVELS):
            d = jnp.concatenate([cs_block(j + 1, c) for c in chunks], axis=0)
            e = jnp.exp(-jnp.abs(d))
            s = lax.dot_general((q * e).astype(BF16), (k * e).astype(BF16), _NT,
                                preferred_element_type=F32)
            attn = attn + mk_ref[j] * s
        vb = z_scr[rows, v_cols].astype(BF16)
        oi_scr[rows, :] = jnp.dot(attn.astype(BF16), vb, preferred_element_type=F32)
        for i, c in enumerate(chunks):
            crow = slice(c * CHUNK, (c + 1) * CHUNK)
            kd = (k_scr[crow, :] * jnp.exp(cs_block(_N_LEVELS + 1, c))).astype(BF16)
            u_scr[c] = lax.dot_general(vb[i * CHUNK:(i + 1) * CHUNK], kd, _TN,
                                       preferred_element_type=F32)

    project_pieces(n_slots - 1)
    st = s_scr[...]
    for c in range(n_chunks):
        rows = slice(c * CHUNK, (c + 1) * CHUNK)
        b = cs_block(0, c)
        qb = (q_scr[rows, :] * jnp.exp(b)).astype(BF16)
        o = lax.dot_general(qb, st.astype(BF16), _NT, preferred_element_type=F32) + oi_scr[rows, :]
        st = st * jnp.exp(b[CHUNK - 1:CHUNK, :]) + u_scr[c]
        gate = z_scr[rows, gate_cols]
        o_ref[rows, :] = (_rms(o) * nw_ref[...] * (gate * _sigmoid(gate))).astype(BF16)
    s_scr[...] = st


def _mixer(h2, w_heads, extra, nw, consts, *, kind, layer, batch, tokens_per_batch, dk, dv):
    m, d = h2.shape
    n_heads, _, ncols = w_heads.shape
    tc = MIX_TC
    assert tokens_per_batch % tc == 0 and tc % _GROUP_ROWS == 0
    nt = tokens_per_batch // tc
    n_tiles = batch * n_heads * nt
    cm, mk = consts

    def coords(tile):
        bb = tile // (n_heads * nt)
        hh = lax.rem(tile // nt, n_heads)
        return bb * nt + lax.rem(tile, nt), hh

    def projected(s):
        return coords(jnp.minimum(s, n_tiles - 1))

    def consumed(s):
        return coords(jnp.maximum(s - 1, 0))

    const2 = lambda s: (0, 0)
    in_specs = [
        pl.BlockSpec((tc, d), lambda s: (projected(s)[0], 0)),
        pl.BlockSpec((None, d, ncols), lambda s: (projected(s)[1], 0, 0)),
    ]
    if kind == "gla":
        w2, b2 = extra
        in_specs += [
            pl.BlockSpec((None, LANES, dk), lambda s: (consumed(s)[1], 0, 0)),
            pl.BlockSpec((None, 1, dk), lambda s: (consumed(s)[1], 0, 0)),
        ]
    else:
        (lbraw,) = extra
        in_specs += [pl.BlockSpec((None, lbraw.shape[1], dk), lambda s: (consumed(s)[1], 0, 0))]
    in_specs += [
        pl.BlockSpec((1, dv), const2),
        pl.BlockSpec(cm.shape, const2),
        pl.BlockSpec(mk.shape, lambda s: (0, 0, 0)),
    ]
    return pl.pallas_call(
        functools.partial(_mixer_kernel, tiles_per_head=nt, kind=kind, layer=layer, dk=dk, dv=dv,
                          tc=tc),
        grid=(n_tiles + 1,),
        in_specs=in_specs,
        out_specs=pl.BlockSpec((tc, dv), lambda s: consumed(s)),
        out_shape=jax.ShapeDtypeStruct((m, n_heads * dv), BF16),
        scratch_shapes=[
            pltpu.VMEM((tc, ncols), F32),
            pltpu.VMEM((tc, ncols), F32),
            pltpu.VMEM((tc, dk), F32),
            pltpu.VMEM((tc, dk), F32),
            pltpu.VMEM((_CS_BLOCKS * CHUNK, (tc // CHUNK) * dk), F32),
            pltpu.VMEM((tc, dv), F32),
            pltpu.VMEM((tc // CHUNK, dv, dk), F32),
            pltpu.VMEM((dv, dk), F32),
        ],
        compiler_params=pltpu.CompilerParams(
            dimension_semantics=("arbitrary",), vmem_limit_bytes=VMEM_LIMIT_BYTES),
        name="mixer_" + kind,
    )(h2, w_heads, *extra, nw, cm, mk)


def _wi_prep_kernel(w_ref, o_ref, *, d_ff, n_f):
    rows = w_ref.shape[0]
    for j in range(n_f):
        for i in range(FFN_TF // MXU_N):
            c0 = j * FFN_TF + i * MXU_N
            valid = max(0, min(MXU_N, d_ff - c0))
            for half in range(2):
                src = half * d_ff + c0
                parts = []
                if valid:
                    parts.append(w_ref[:, src:src + valid].astype(BF16))
                if valid < MXU_N:
                    parts.append(jnp.zeros((rows, MXU_N - valid), BF16))
                piece = parts[0] if len(parts) == 1 else jnp.concatenate(parts, axis=1)
                o_ref[j, :, (2 * i + half) * MXU_N:(2 * i + half + 1) * MXU_N] = piece


def _prep_ffn_weights(wi, wo):
    d, d_ff = wi.shape[0], wo.shape[0]
    assert d_ff % LANES == 0 and d % WPREP_ROWS == 0
    n_f = -(-d_ff // FFN_TF)
    wab = pl.pallas_call(
        functools.partial(_wi_prep_kernel, d_ff=d_ff, n_f=n_f),
        grid=(d // WPREP_ROWS,),
        in_specs=[pl.BlockSpec((WPREP_ROWS, 2 * d_ff), lambda r: (r, 0))],
        out_specs=pl.BlockSpec((n_f, WPREP_ROWS, 2 * FFN_TF), lambda r: (0, r, 0)),
        out_shape=jax.ShapeDtypeStruct((n_f, d, 2 * FFN_TF), BF16),
        compiler_params=pltpu.CompilerParams(
            dimension_semantics=("arbitrary",), vmem_limit_bytes=VMEM_LIMIT_BYTES),
        name="wi_prep",
    )(wi)
    wob = jnp.pad(wo.astype(BF16), ((0, n_f * FFN_TF - d_ff), (0, 0)))
    return wab, wob


def _prep_mixer_weights(w_in, w2, b2):
    d = w_in.shape[0]
    gla_qk = GLA_HEADS * GLA_DK
    gla_v = GLA_HEADS * GLA_DV
    hg_k = HGRN_HEADS * HGRN_DK
    hg_v = HGRN_HEADS * HGRN_DV
    offs = np.cumsum([0, gla_qk, gla_qk, gla_v, gla_v, GLA_GATE_RANK, hg_k, hg_k, hg_v, hg_v])
    wb = w_in.astype(BF16)

    def part(i, heads):
        return wb[:, offs[i]:offs[i + 1]].reshape(d, heads, -1).transpose(1, 0, 2)

    gr = jnp.pad(wb[:, offs[4]:offs[5]], ((0, 0), (0, LANES - GLA_GATE_RANK)))
    gr = jnp.broadcast_to(gr[None], (GLA_HEADS, d, LANES))
    w_gla = jnp.concatenate([part(0, GLA_HEADS), part(1, GLA_HEADS), part(2, GLA_HEADS),
                             part(3, GLA_HEADS), gr], axis=-1)
    w_hg = jnp.concatenate([part(5, HGRN_HEADS), part(6, HGRN_HEADS), part(7, HGRN_HEADS),
                            part(8, HGRN_HEADS)], axis=-1)
    w2h = jnp.pad(w2.astype(BF16), ((0, LANES - GLA_GATE_RANK), (0, 0)))
    w2h = w2h.reshape(LANES, GLA_HEADS, GLA_DK).transpose(1, 0, 2)
    b2h = b2.reshape(GLA_HEADS, 1, GLA_DK)
    return w_gla, w_hg, w2h, b2h


def kernel(x, c, ada_w, ada_b, norm_ffn1_w, ffn1_wi, ffn1_wo, norm_mix_w, w_in, gla_gate_w2,
           gla_gate_b2, gla_norm_w, hgrn_norm_w, hgrn_lower_bounds, w_out, norm_ffn2_w, ffn2_wi,
           ffn2_wo, final_norm_w):
    batch, seq, d = x.shape
    depth = ada_w.shape[0]
    m = batch * seq
    consts = _chunk_constants()
    xc = x.reshape(m, d)
    c_pad = jnp.pad(c, ((0, SUBLANES - batch % SUBLANES), (0, 0))) if batch % SUBLANES else c
    gla_v = GLA_HEADS * GLA_DV
    lb_heads = hgrn_lower_bounds.astype(F32).reshape(depth + 1, HGRN_HEADS, HGRN_DK).transpose(1, 0, 2)

    for l in range(depth):
        mod = _adaln(c_pad, ada_w[l], ada_b[l][None, :])[:batch].reshape(batch, N_MOD, d)
        wab1, wo1 = _prep_ffn_weights(ffn1_wi[l], ffn1_wo[l])
        wab2, wo2 = _prep_ffn_weights(ffn2_wi[l], ffn2_wo[l])
        w_gla, w_hg, w2h, b2h = _prep_mixer_weights(w_in[l], gla_gate_w2[l], gla_gate_b2[l])
        wout = w_out[l].astype(BF16)

        x1, h2 = _ffn(xc, mod, norm_ffn1_w[l][None, :], norm_mix_w[l][None, :], wab1, wo1,
                      tokens_per_batch=seq, mod_base=0, epilogue="prenorm")
        o_gla = _mixer(h2, w_gla, (w2h, b2h), gla_norm_w[l][None, :], consts, kind="gla",
                       layer=l, batch=batch, tokens_per_batch=seq, dk=GLA_DK, dv=GLA_DV)
        o_hg = _mixer(h2, w_hg, (lb_heads,), hgrn_norm_w[l][None, :], consts, kind="hgrn",
                      layer=l, batch=batch, tokens_per_batch=seq, dk=HGRN_DK, dv=HGRN_DV)
        last = l == depth - 1
        nw2 = final_norm_w[None, :] if last else norm_ffn2_w[l][None, :]
        (xc,) = _ffn(x1, mod, norm_ffn2_w[l][None, :], nw2, wab2, wo2,
                     tokens_per_batch=seq, mod_base=6, epilogue="final" if last else "none",
                     mix=(o_gla, o_hg, wout[:gla_v], wout[gla_v:]), mix_gate_row=5)
    return xc.reshape(batch, seq, d)
```

```python
import functools

import jax
import jax.numpy as jnp
import numpy as np
from jax import lax
from jax.experimental import pallas as pl
from jax.experimental.pallas import tpu as pltpu

F32 = jnp.float32
BF16 = jnp.bfloat16

GLA_HEADS = 4
GLA_DK = 128
GLA_DV = 256
GLA_GATE_RANK = 16
GLA_GATE_NORMALIZER = 16.0
HGRN_HEADS = 8
HGRN_DK = 128
HGRN_DV = 128
CHUNK = 64
MACARON_W = 0.5
N_MOD = 9
EPS = 1e-6

LANES = 128
SUBLANES = 8
MXU_N = 256
VMEM_LIMIT_BYTES = 56 * 1024 * 1024

FFN_TM = 512
FFN_TF = 512
PROLOGUE_ROWS = 256
MIX_TC = 512
ADALN_TN = 1024
WPREP_ROWS = 256

_NT = (((1,), (1,)), ((), ()))
_TN = (((0,), (0,)), ((), ()))


def _sigmoid(x):
    return jax.nn.sigmoid(x)


def _rms(x):
    return x * lax.rsqrt(jnp.mean(x * x, axis=-1, keepdims=True) + EPS)


def _adaln_kernel(c_ref, w_ref, b_ref, o_ref):
    c = c_ref[...]
    ca = (c * _sigmoid(c)).astype(BF16)
    o_ref[...] = jnp.dot(ca, w_ref[...].astype(BF16), preferred_element_type=F32) + b_ref[...]


def _adaln(c_pad, w, b):
    rows, d = c_pad.shape
    n = w.shape[1]
    assert n % ADALN_TN == 0
    return pl.pallas_call(
        _adaln_kernel,
        grid=(n // ADALN_TN,),
        in_specs=[
            pl.BlockSpec((rows, d), lambda j: (0, 0)),
            pl.BlockSpec((d, ADALN_TN), lambda j: (0, j)),
            pl.BlockSpec((1, ADALN_TN), lambda j: (0, j)),
        ],
        out_specs=pl.BlockSpec((rows, ADALN_TN), lambda j: (0, j)),
        out_shape=jax.ShapeDtypeStruct((rows, n), F32),
        compiler_params=pltpu.CompilerParams(
            dimension_semantics=("arbitrary",), vmem_limit_bytes=VMEM_LIMIT_BYTES),
        name="adaln",
    )(c_pad, w, b)


def _ffn_kernel(*refs, mod_base, epilogue, mix_gate_row):
    x_ref, mod_ref, nw_ref, nw2_ref, wab_ref, wo_ref, *rest = refs
    if mix_gate_row is not None:
        oa_ref, ob_ref, wa_ref, wb_ref, *rest = rest
    if epilogue == "prenorm":
        xo_ref, ho_ref, h_scr, *rest = rest
    else:
        xo_ref, h_scr, *rest = rest
    xin_ref = rest[0] if mix_gate_row is not None else x_ref
    f = pl.program_id(1)

    @pl.when(f == 0)
    def _():
        shift = mod_ref[mod_base:mod_base + 1, :]
        gain = nw_ref[...] * (1.0 + mod_ref[mod_base + 1:mod_base + 2, :])
        for r0 in range(0, FFN_TM, PROLOGUE_ROWS):
            rows = slice(r0, r0 + PROLOGUE_ROWS)
            if mix_gate_row is not None:
                y = (jnp.dot(oa_ref[rows, :], wa_ref[...], preferred_element_type=F32)
                     + jnp.dot(ob_ref[rows, :], wb_ref[...], preferred_element_type=F32))
                xin_ref[rows, :] = x_ref[rows, :] + mod_ref[mix_gate_row:mix_gate_row + 1, :] * y
            h_scr[rows, :] = (_rms(xin_ref[rows, :]) * gain + shift).astype(BF16)
        xo_ref[...] = jnp.zeros_like(xo_ref)

    zab = jnp.dot(h_scr[...], wab_ref[...], preferred_element_type=F32)
    pieces = FFN_TF // MXU_N
    a = jnp.concatenate([zab[:, (2 * i) * MXU_N:(2 * i + 1) * MXU_N] for i in range(pieces)], axis=1)
    b = jnp.concatenate([zab[:, (2 * i + 1) * MXU_N:(2 * i + 2) * MXU_N] for i in range(pieces)],
                        axis=1)
    act = (a * _sigmoid(a) * b).astype(BF16)
    xo_ref[...] += jnp.dot(act, wo_ref[...], preferred_element_type=F32)

    @pl.when(f == pl.num_programs(1) - 1)
    def _():
        gate = mod_ref[mod_base + 2:mod_base + 3, :]
        xn = xin_ref[...] + MACARON_W * gate * xo_ref[...]
        if epilogue == "prenorm":
            xo_ref[...] = xn
            shift2 = mod_ref[mod_base + 3:mod_base + 4, :]
            scale2 = mod_ref[mod_base + 4:mod_base + 5, :]
            ho_ref[...] = (_rms(xn) * nw2_ref[...] * (1.0 + scale2) + shift2).astype(BF16)
        elif epilogue == "final":
            xo_ref[...] = _rms(xn) * nw2_ref[...]
        else:
            xo_ref[...] = xn


def _ffn(x2d, mod, nw, nw2, wab, wo, *, tokens_per_batch, mod_base, epilogue, mix=None,
         mix_gate_row=None):
    m, d = x2d.shape
    n_f = wab.shape[0]
    assert m % FFN_TM == 0 and tokens_per_batch % FFN_TM == 0 and wo.shape[0] == n_f * FFN_TF
    assert (mix is None) == (mix_gate_row is None)
    tiles_per_batch = tokens_per_batch // FFN_TM
    row_spec = pl.BlockSpec((FFN_TM, d), lambda i, f: (i, 0))
    vec_spec = pl.BlockSpec((1, d), lambda i, f: (0, 0))
    in_specs = [
        row_spec,
        pl.BlockSpec((None, N_MOD, d), lambda i, f: (i // tiles_per_batch, 0, 0)),
        vec_spec,
        vec_spec,
        pl.BlockSpec((None, d, 2 * FFN_TF), lambda i, f: (f, 0, 0)),
        pl.BlockSpec((FFN_TF, d), lambda i, f: (f, 0)),
    ]
    operands = [x2d, mod, nw, nw2, wab, wo]
    scratch = [pltpu.VMEM((FFN_TM, d), BF16)]
    if mix is not None:
        oa, ob, wa, wb = mix
        in_specs += [
            pl.BlockSpec((FFN_TM, oa.shape[1]), lambda i, f: (i, 0)),
            pl.BlockSpec((FFN_TM, ob.shape[1]), lambda i, f: (i, 0)),
            pl.BlockSpec(wa.shape, lambda i, f: (0, 0), pipeline_mode=pl.Buffered(1)),
            pl.BlockSpec(wb.shape, lambda i, f: (0, 0), pipeline_mode=pl.Buffered(1)),
        ]
        operands += [oa, ob, wa, wb]
        scratch.append(pltpu.VMEM((FFN_TM, d), F32))
    out_shape = [jax.ShapeDtypeStruct((m, d), F32)]
    out_specs = [row_spec]
    if epilogue == "prenorm":
        out_shape.append(jax.ShapeDtypeStruct((m, d), BF16))
        out_specs.append(row_spec)
    return pl.pallas_call(
        functools.partial(_ffn_kernel, mod_base=mod_base, epilogue=epilogue,
                          mix_gate_row=mix_gate_row),
        grid=(m // FFN_TM, n_f),
        in_specs=in_specs,
        out_specs=out_specs,
        out_shape=out_shape,
        scratch_shapes=scratch,
        compiler_params=pltpu.CompilerParams(
            dimension_semantics=("arbitrary", "arbitrary"),
            vmem_limit_bytes=VMEM_LIMIT_BYTES),
        name="ffn_" + epilogue,
    )(*operands)


_LEVEL_HALVES = tuple(CHUNK >> (j + 1) for j in range(CHUNK.bit_length() - 1))
_N_LEVELS = len(_LEVEL_HALVES)
_CS_BLOCKS = _N_LEVELS + 2
_GROUP = 4
_GROUP_ROWS = _GROUP * CHUNK


def _chunk_constants():
    t = np.arange(CHUNK)
    tri = (t[None, :] <= t[:, None]).astype(np.float32)
    blocks = [tri]
    masks = []
    for half in _LEVEL_HALVES:
        ref = (t // (2 * half)) * (2 * half) + half
        blocks.append(tri - tri[ref])
        same_block = (t[:, None] // (2 * half)) == (t[None, :] // (2 * half))
        is_query = (t % (2 * half)) >= half
        masks.append((same_block & is_query[:, None] & ~is_query[None, :]).astype(np.float32))
    blocks.append(1.0 - tri)
    masks.append(np.eye(CHUNK, dtype=np.float32))
    cm = np.concatenate(blocks, axis=0)
    cm2 = np.concatenate([cm, cm], axis=1)
    group_masks = np.stack([np.kron(np.eye(_GROUP, dtype=np.float32), m) for m in masks], axis=0)
    return jnp.asarray(cm2, dtype=BF16), jnp.asarray(group_masks, dtype=F32)


def _mixer_kernel(*refs, tiles_per_head, **static):
    *io_refs, z_a, z_b, q_scr, k_scr, cs_scr, oi_scr, u_scr, s_scr = refs
    scratch = (q_scr, k_scr, cs_scr, oi_scr, u_scr, s_scr)
    s = pl.program_id(0)

    @pl.when(s == 0)
    def _():
        z_b[...] = jnp.zeros_like(z_b)

    @pl.when(lax.rem(jnp.maximum(s - 1, 0), tiles_per_head) == 0)
    def _():
        s_scr[...] = jnp.zeros_like(s_scr)

    @pl.when(lax.rem(s, 2) == 0)
    def _():
        _mixer_tile(io_refs, scratch, z_b, z_a, **static)

    @pl.when(lax.rem(s, 2) == 1)
    def _():
        _mixer_tile(io_refs, scratch, z_a, z_b, **static)


def _mixer_tile(io_refs, scratch, z_scr, z_next, *, kind, layer, dk, dv, tc):
    if kind == "gla":
        h_ref, w_ref, w2_ref, b2_ref, nw_ref, cm_ref, mk_ref, o_ref = io_refs
    else:
        h_ref, w_ref, lb_ref, nw_ref, cm_ref, mk_ref, o_ref = io_refs
    q_scr, k_scr, cs_scr, oi_scr, u_scr, s_scr = scratch
    n_chunks = tc // CHUNK
    v_cols = slice(2 * dk, 2 * dk + dv)
    gate_cols = slice(2 * dk + dv, 2 * dk + 2 * dv)

    ncols = w_ref.shape[1]
    pieces = [slice(c0, min(c0 + MXU_N, ncols)) for c0 in range(0, ncols, MXU_N)]
    n_slots = n_chunks // _GROUP + 2

    def project_pieces(slot):
        lo = slot * len(pieces) // n_slots
        hi = (slot + 1) * len(pieces) // n_slots
        for cols in pieces[lo:hi]:
            z_next[:, cols] = jnp.dot(h_ref[...], w_ref[:, cols], preferred_element_type=F32)

    project_pieces(0)

    if kind == "gla":
        q_scr[...] = z_scr[:, 0:dk] * (dk ** -0.5)
        k_scr[...] = z_scr[:, dk:2 * dk]
        gr = z_scr[:, 2 * dk + 2 * dv:2 * dk + 2 * dv + LANES]
        gp = jnp.dot(gr.astype(BF16), w2_ref[...], preferred_element_type=F32) + b2_ref[...]
        la = (jnp.minimum(gp, 0.0) - jnp.log(1.0 + jnp.exp(-jnp.abs(gp)))) * (
            1.0 / GLA_GATE_NORMALIZER)
    else:
        raw = lb_ref[...]
        ex = jnp.exp(raw - jnp.max(raw, axis=0, keepdims=True))
        p = ex / jnp.sum(ex, axis=0, keepdims=True)
        lb = jnp.sum(p[0:layer + 1, :], axis=0, keepdims=True)
        hq = z_scr[:, 0:dk]
        fr = z_scr[:, dk:2 * dk]
        q_scr[...] = hq * _sigmoid(hq)
        en = jnp.exp(-jnp.abs(fr))
        rr = 1.0 / (1.0 + en)
        sig_pos = jnp.where(fr >= 0.0, rr, en * rr)
        sig_neg = jnp.where(fr >= 0.0, en * rr, rr)
        la = jnp.log(lb + (1.0 - lb) * sig_pos)
        k_scr[...] = (1.0 - lb) * sig_neg

    la_wide = jnp.concatenate([la[c * CHUNK:(c + 1) * CHUNK] for c in range(n_chunks)], axis=1)
    la_hi = la_wide.astype(BF16)
    la_lo = (la_wide - la_hi.astype(F32)).astype(BF16)
    cs_scr[...] = jnp.dot(cm_ref[...], jnp.concatenate([la_hi, la_lo], axis=0),
                          preferred_element_type=F32)

    def cs_block(block, c):
        return cs_scr[block * CHUNK:(block + 1) * CHUNK, c * dk:(c + 1) * dk]

    for g in range(n_chunks // _GROUP):
        project_pieces(g + 1)
        chunks = range(g * _GROUP, (g + 1) * _GROUP)
        rows = slice(g * _GROUP_ROWS, (g + 1) * _GROUP_ROWS)
        q = q_scr[rows, :]
        k = k_scr[rows, :]
        attn = mk_ref[_N_LEVELS] * jnp.sum(q * k, axis=-1, keepdims=True)
        for j in range(_N_LEVELS):
            d = jnp.concatenate([cs_block(j + 1, c) for c in chunks], axis=0)
            e = jnp.exp(-jnp.abs(d))
            s = lax.dot_general((q * e).astype(BF16), (k * e).astype(BF16), _NT,
                                preferred_element_type=F32)
            attn = attn + mk_ref[j] * s
        vb = z_scr[rows, v_cols].astype(BF16)
        oi_scr[rows, :] = jnp.dot(attn.astype(BF16), vb, preferred_element_type=F32)
        for i, c in enumerate(chunks):
            crow = slice(c * CHUNK, (c + 1) * CHUNK)
            kd = (k_scr[crow, :] * jnp.exp(cs_block(_N_LEVELS + 1, c))).astype(BF16)
            u_scr[c] = lax.dot_general(vb[i * CHUNK:(i + 1) * CHUNK], kd, _TN,
                                       preferred_element_type=F32)

    project_pieces(n_slots - 1)
    st = s_scr[...]
    for c in range(n_chunks):
        rows = slice(c * CHUNK, (c + 1) * CHUNK)
        b = cs_block(0, c)
        qb = (q_scr[rows, :] * jnp.exp(b)).astype(BF16)
        o = lax.dot_general(qb, st.astype(BF16), _NT, preferred_element_type=F32) + oi_scr[rows, :]
        st = st * jnp.exp(b[CHUNK - 1:CHUNK, :]) + u_scr[c]
        gate = z_scr[rows, gate_cols]
        o_ref[rows, :] = (_rms(o) * nw_ref[...] * (gate * _sigmoid(gate))).astype(BF16)
    s_scr[...] = st


def _mixer(h2, w_heads, extra, nw, consts, *, kind, layer, batch, tokens_per_batch, dk, dv):
    m, d = h2.shape
    n_heads, _, ncols = w_heads.shape
    tc = MIX_TC
    assert tokens_per_batch % tc == 0 and tc % _GROUP_ROWS == 0
    nt = tokens_per_batch // tc
    n_tiles = batch * n_heads * nt
    cm, mk = consts

    def coords(tile):
        bb = tile // (n_heads * nt)
        hh = lax.rem(tile // nt, n_heads)
        return bb * nt + lax.rem(tile, nt), hh

    def projected(s):
        return coords(jnp.minimum(s, n_tiles - 1))

    def consumed(s):
        return coords(jnp.maximum(s - 1, 0))

    const2 = lambda s: (0, 0)
    in_specs = [
        pl.BlockSpec((tc, d), lambda s: (projected(s)[0], 0)),
        pl.BlockSpec((None, d, ncols), lambda s: (projected(s)[1], 0, 0)),
    ]
    if kind == "gla":
        w2, b2 = extra
        in_specs += [
            pl.BlockSpec((None, LANES, dk), lambda s: (consumed(s)[1], 0, 0)),
            pl.BlockSpec((None, 1, dk), lambda s: (consumed(s)[1], 0, 0)),
        ]
    else:
        (lbraw,) = extra
        in_specs += [pl.BlockSpec((None, lbraw.shape[1], dk), lambda s: (consumed(s)[1], 0, 0))]
    in_specs += [
        pl.BlockSpec((1, dv), const2),
        pl.BlockSpec(cm.shape, const2),
        pl.BlockSpec(mk.shape, lambda s: (0, 0, 0)),
    ]
    return pl.pallas_call(
        functools.partial(_mixer_kernel, tiles_per_head=nt, kind=kind, layer=layer, dk=dk, dv=dv,
                          tc=tc),
        grid=(n_tiles + 1,),
        in_specs=in_specs,
        out_specs=pl.BlockSpec((tc, dv), lambda s: consumed(s)),
        out_shape=jax.ShapeDtypeStruct((m, n_heads * dv), BF16),
        scratch_shapes=[
            pltpu.VMEM((tc, ncols), F32),
            pltpu.VMEM((tc, ncols), F32),
            pltpu.VMEM((tc, dk), F32),
            pltpu.VMEM((tc, dk), F32),
            pltpu.VMEM((_CS_BLOCKS * CHUNK, (tc // CHUNK) * dk), F32),
            pltpu.VMEM((tc, dv), F32),
            pltpu.VMEM((tc // CHUNK, dv, dk), F32),
            pltpu.VMEM((dv, dk), F32),
        ],
        compiler_params=pltpu.CompilerParams(
            dimension_semantics=("arbitrary",), vmem_limit_bytes=VMEM_LIMIT_BYTES),
        name="mixer_" + kind,
    )(h2, w_heads, *extra, nw, cm, mk)


def _wi_prep_kernel(w_ref, o_ref, *, d_ff, n_f):
    rows = w_ref.shape[0]
    for j in range(n_f):
        for i in range(FFN_TF // MXU_N):
            c0 = j * FFN_TF + i * MXU_N
            valid = max(0, min(MXU_N, d_ff - c0))
            for half in range(2):
                src = half * d_ff + c0
                parts = []
                if valid:
                    parts.append(w_ref[:, src:src + valid].astype(BF16))
                if valid < MXU_N:
                    parts.append(jnp.zeros((rows, MXU_N - valid), BF16))
                piece = parts[0] if len(parts) == 1 else jnp.concatenate(parts, axis=1)
                o_ref[j, :, (2 * i + half) * MXU_N:(2 * i + half + 1) * MXU_N] = piece


def _wo_prep_kernel(w_ref, o_ref, *, n_src):
    r = pl.program_id(0)

    @pl.when(r < n_src)
    def _():
        o_ref[...] = w_ref[...].astype(BF16)

    @pl.when(r >= n_src)
    def _():
        o_ref[...] = jnp.zeros_like(o_ref)


def _prep_ffn_weights(wi, wo):
    d, d_ff = wi.shape[0], wo.shape[0]
    assert d_ff % LANES == 0 and d % WPREP_ROWS == 0
    n_f = -(-d_ff // FFN_TF)
    wab = pl.pallas_call(
        functools.partial(_wi_prep_kernel, d_ff=d_ff, n_f=n_f),
        grid=(d // WPREP_ROWS,),
        in_specs=[pl.BlockSpec((WPREP_ROWS, 2 * d_ff), lambda r: (r, 0))],
        out_specs=pl.BlockSpec((n_f, WPREP_ROWS, 2 * FFN_TF), lambda r: (0, r, 0)),
        out_shape=jax.ShapeDtypeStruct((n_f, d, 2 * FFN_TF), BF16),
        compiler_params=pltpu.CompilerParams(
            dimension_semantics=("arbitrary",), vmem_limit_bytes=VMEM_LIMIT_BYTES),
        name="wi_prep",
    )(wi)
    n_src = d_ff // LANES
    wob = pl.pallas_call(
        functools.partial(_wo_prep_kernel, n_src=n_src),
        grid=(n_f * FFN_TF // LANES,),
        in_specs=[pl.BlockSpec((LANES, d), lambda r: (jnp.minimum(r, n_src - 1), 0))],
        out_specs=pl.BlockSpec((LANES, d), lambda r: (r, 0)),
        out_shape=jax.ShapeDtypeStruct((n_f * FFN_TF, d), BF16),
        compiler_params=pltpu.CompilerParams(
            dimension_semantics=("arbitrary",), vmem_limit_bytes=VMEM_LIMIT_BYTES),
        name="wo_prep",
    )(wo)
    return wab, wob


_GLA_QK = GLA_HEADS * GLA_DK
_GLA_V = GLA_HEADS * GLA_DV
_HGRN_K = HGRN_HEADS * HGRN_DK
_HGRN_V = HGRN_HEADS * HGRN_DV
_IN_OFFS = tuple(int(o) for o in np.cumsum(
    [0, _GLA_QK, _GLA_QK, _GLA_V, _GLA_V, GLA_GATE_RANK, _HGRN_K, _HGRN_K, _HGRN_V, _HGRN_V]))


def _win_prep_kernel(w_ref, gr_ref, og_ref, oh_ref):
    def cols(part, h, width):
        c0 = _IN_OFFS[part] + h * width
        return w_ref[:, c0:c0 + width].astype(BF16)

    for h in range(GLA_HEADS):
        og_ref[h] = jnp.concatenate(
            [cols(0, h, GLA_DK), cols(1, h, GLA_DK), cols(2, h, GLA_DV), cols(3, h, GLA_DV),
             gr_ref[...]], axis=1)
    for h in range(HGRN_HEADS):
        oh_ref[h] = jnp.concatenate(
            [cols(5, h, HGRN_DK), cols(6, h, HGRN_DK), cols(7, h, HGRN_DV), cols(8, h, HGRN_DV)],
            axis=1)


def _prep_mixer_weights(w_in, w2, b2):
    d, in_width = w_in.shape
    assert in_width == _IN_OFFS[-1] and d % WPREP_ROWS == 0
    gla_cols = 2 * GLA_DK + 2 * GLA_DV + LANES
    hgrn_cols = 2 * HGRN_DK + 2 * HGRN_DV
    gr = jnp.pad(w_in[:, _IN_OFFS[4]:_IN_OFFS[5]].astype(BF16),
                 ((0, 0), (0, LANES - GLA_GATE_RANK)))
    w_gla, w_hg = pl.pallas_call(
        _win_prep_kernel,
        grid=(d // WPREP_ROWS,),
        in_specs=[pl.BlockSpec((WPREP_ROWS, in_width), lambda r: (r, 0)),
                  pl.BlockSpec((WPREP_ROWS, LANES), lambda r: (r, 0))],
        out_specs=[pl.BlockSpec((GLA_HEADS, WPREP_ROWS, gla_cols), lambda r: (0, r, 0)),
                   pl.BlockSpec((HGRN_HEADS, WPREP_ROWS, hgrn_cols), lambda r: (0, r, 0))],
        out_shape=[jax.ShapeDtypeStruct((GLA_HEADS, d, gla_cols), BF16),
                   jax.ShapeDtypeStruct((HGRN_HEADS, d, hgrn_cols), BF16)],
        compiler_params=pltpu.CompilerParams(
            dimension_semantics=("arbitrary",), vmem_limit_bytes=VMEM_LIMIT_BYTES),
        name="win_prep",
    )(w_in, gr)
    w2h = jnp.pad(w2.astype(BF16), ((0, LANES - GLA_GATE_RANK), (0, 0)))
    w2h = w2h.reshape(LANES, GLA_HEADS, GLA_DK).transpose(1, 0, 2)
    b2h = b2.reshape(GLA_HEADS, 1, GLA_DK)
    return w_gla, w_hg, w2h, b2h


def kernel(x, c, ada_w, ada_b, norm_ffn1_w, ffn1_wi, ffn1_wo, norm_mix_w, w_in, gla_gate_w2,
           gla_gate_b2, gla_norm_w, hgrn_norm_w, hgrn_lower_bounds, w_out, norm_ffn2_w, ffn2_wi,
           ffn2_wo, final_norm_w):
    batch, seq, d = x.shape
    depth = ada_w.shape[0]
    m = batch * seq
    consts = _chunk_constants()
    xc = x.reshape(m, d)
    c_pad = jnp.pad(c, ((0, SUBLANES - batch % SUBLANES), (0, 0))) if batch % SUBLANES else c
    gla_v = GLA_HEADS * GLA_DV
    lb_heads = hgrn_lower_bounds.astype(F32).reshape(depth + 1, HGRN_HEADS, HGRN_DK).transpose(1, 0, 2)

    for l in range(depth):
        mod = _adaln(c_pad, ada_w[l], ada_b[l][None, :])[:batch].reshape(batch, N_MOD, d)
        wab1, wo1 = _prep_ffn_weights(ffn1_wi[l], ffn1_wo[l])
        wab2, wo2 = _prep_ffn_weights(ffn2_wi[l], ffn2_wo[l])
        w_gla, w_hg, w2h, b2h = _prep_mixer_weights(w_in[l], gla_gate_w2[l], gla_gate_b2[l])
        wout = w_out[l].astype(BF16)

        x1, h2 = _ffn(xc, mod, norm_ffn1_w[l][None, :], norm_mix_w[l][None, :], wab1, wo1,
                      tokens_per_batch=seq, mod_base=0, epilogue="prenorm")
        o_gla = _mixer(h2, w_gla, (w2h, b2h), gla_norm_w[l][None, :], consts, kind="gla",
                       layer=l, batch=batch, tokens_per_batch=seq, dk=GLA_DK, dv=GLA_DV)
        o_hg = _mixer(h2, w_hg, (lb_heads,), hgrn_norm_w[l][None, :], consts, kind="hgrn",
                      layer=l, batch=batch, tokens_per_batch=seq, dk=HGRN_DK, dv=HGRN_DV)
        last = l == depth - 1
        nw2 = final_norm_w[None, :] if last else norm_ffn2_w[l][None, :]
        (xc,) = _ffn(x1, mod, norm_ffn2_w[l][None, :], nw2, wab2, wo2,
                     tokens_per_batch=seq, mod_base=6, epilogue="final" if last else "none",
                     mix=(o_gla, o_hg, wout[:gla_v], wout[gla_v:]), mix_gate_row=5)
    return xc.reshape(batch, seq, d)
```

```python
import functools

import jax
import jax.numpy as jnp
import numpy as np
from jax import lax
from jax.experimental import pallas as pl
from jax.experimental.pallas import tpu as pltpu

F32 = jnp.float32
BF16 = jnp.bfloat16

GLA_HEADS = 4
GLA_DK = 128
GLA_DV = 256
GLA_GATE_RANK = 16
GLA_GATE_NORMALIZER = 16.0
HGRN_HEADS = 8
HGRN_DK = 128
HGRN_DV = 128
CHUNK = 64
MACARON_W = 0.5
N_MOD = 9
EPS = 1e-6

LANES = 128
SUBLANES = 8
MXU_N = 256
VMEM_LIMIT_BYTES = 56 * 1024 * 1024

FFN_TM = 512
FFN_TF = 512
PROLOGUE_ROWS = 256
NORM_ROWS = 16
MIX_TC = 1024
ADALN_TN = 1024
WPREP_ROWS = 256

_NT = (((1,), (1,)), ((), ()))
_TN = (((0,), (0,)), ((), ()))


def _sigmoid(x):
    return jax.nn.sigmoid(x)


def _rms(x):
    return x * lax.rsqrt(jnp.mean(x * x, axis=-1, keepdims=True) + EPS)


def _adaln_kernel(c_ref, w_ref, b_ref, o_ref):
    c = c_ref[...]
    ca = (c * _sigmoid(c)).astype(BF16)
    o_ref[...] = jnp.dot(ca, w_ref[...].astype(BF16), preferred_element_type=F32) + b_ref[...]


def _adaln(c_pad, w, b):
    rows, d = c_pad.shape
    n = w.shape[1]
    assert n % ADALN_TN == 0
    return pl.pallas_call(
        _adaln_kernel,
        grid=(n // ADALN_TN,),
        in_specs=[
            pl.BlockSpec((rows, d), lambda j: (0, 0)),
            pl.BlockSpec((d, ADALN_TN), lambda j: (0, j)),
            pl.BlockSpec((1, ADALN_TN), lambda j: (0, j)),
        ],
        out_specs=pl.BlockSpec((rows, ADALN_TN), lambda j: (0, j)),
        out_shape=jax.ShapeDtypeStruct((rows, n), F32),
        compiler_params=pltpu.CompilerParams(
            dimension_semantics=("arbitrary",), vmem_limit_bytes=VMEM_LIMIT_BYTES),
        name="adaln",
    )(c_pad, w, b)


def _ffn_kernel(*refs, mod_base, epilogue, mix_gate_row):
    x_ref, mod_ref, nw_ref, nw2_ref, wab_ref, wo_ref, *rest = refs
    if mix_gate_row is not None:
        oa_ref, ob_ref, wa_ref, wb_ref, *rest = rest
    if epilogue == "prenorm":
        xo_ref, ho_ref, h_scr, *rest = rest
    else:
        xo_ref, h_scr, *rest = rest
    xin_ref = rest[0] if mix_gate_row is not None else x_ref
    f = pl.program_id(1)

    @pl.when(f == 0)
    def _():
        shift = mod_ref[mod_base:mod_base + 1, :]
        gain = nw_ref[...] * (1.0 + mod_ref[mod_base + 1:mod_base + 2, :])
        for r0 in range(0, FFN_TM, PROLOGUE_ROWS):
            rows = slice(r0, r0 + PROLOGUE_ROWS)
            if mix_gate_row is not None:
                y = (jnp.dot(oa_ref[rows, :], wa_ref[...], preferred_element_type=F32)
                     + jnp.dot(ob_ref[rows, :], wb_ref[...], preferred_element_type=F32))
                xin_ref[rows, :] = x_ref[rows, :] + mod_ref[mix_gate_row:mix_gate_row + 1, :] * y
            for c0 in range(r0, r0 + PROLOGUE_ROWS, NORM_ROWS):
                chunk = slice(c0, c0 + NORM_ROWS)
                h_scr[chunk, :] = (_rms(xin_ref[chunk, :]) * gain + shift).astype(BF16)
        xo_ref[...] = jnp.zeros_like(xo_ref)

    zab = jnp.dot(h_scr[...], wab_ref[...], preferred_element_type=F32)
    pieces = FFN_TF // MXU_N
    a = jnp.concatenate([zab[:, (2 * i) * MXU_N:(2 * i + 1) * MXU_N] for i in range(pieces)], axis=1)
    b = jnp.concatenate([zab[:, (2 * i + 1) * MXU_N:(2 * i + 2) * MXU_N] for i in range(pieces)],
                        axis=1)
    act = (a * _sigmoid(a) * b).astype(BF16)
    xo_ref[...] += jnp.dot(act, wo_ref[...], preferred_element_type=F32)

    @pl.when(f == pl.num_programs(1) - 1)
    def _():
        gate = MACARON_W * mod_ref[mod_base + 2:mod_base + 3, :]
        if epilogue == "prenorm":
            shift2 = mod_ref[mod_base + 3:mod_base + 4, :]
            gain2 = nw2_ref[...] * (1.0 + mod_ref[mod_base + 4:mod_base + 5, :])
        for c0 in range(0, FFN_TM, NORM_ROWS):
            chunk = slice(c0, c0 + NORM_ROWS)
            xn = xin_ref[chunk, :] + gate * xo_ref[chunk, :]
            if epilogue == "prenorm":
                xo_ref[chunk, :] = xn
                ho_ref[chunk, :] = (_rms(xn) * gain2 + shift2).astype(BF16)
            elif epilogue == "final":
                xo_ref[chunk, :] = _rms(xn) * nw2_ref[...]
            else:
                xo_ref[chunk, :] = xn


def _ffn(x2d, mod, nw, nw2, wab, wo, *, tokens_per_batch, mod_base, epilogue, mix=None,
         mix_gate_row=None):
    m, d = x2d.shape
    n_f = wab.shape[0]
    assert m % FFN_TM == 0 and tokens_per_batch % FFN_TM == 0 and wo.shape[0] == n_f * FFN_TF
    assert (mix is None) == (mix_gate_row is None)
    tiles_per_batch = tokens_per_batch // FFN_TM
    row_spec = pl.BlockSpec((FFN_TM, d), lambda i, f: (i, 0))
    vec_spec = pl.BlockSpec((1, d), lambda i, f: (0, 0))
    in_specs = [
        row_spec,
        pl.BlockSpec((None, N_MOD, d), lambda i, f: (i // tiles_per_batch, 0, 0)),
        vec_spec,
        vec_spec,
        pl.BlockSpec((None, d, 2 * FFN_TF), lambda i, f: (f, 0, 0)),
        pl.BlockSpec((FFN_TF, d), lambda i, f: (f, 0)),
    ]
    operands = [x2d, mod, nw, nw2, wab, wo]
    scratch = [pltpu.VMEM((FFN_TM, d), BF16)]
    if mix is not None:
        oa, ob, wa, wb = mix
        in_specs += [
            pl.BlockSpec((FFN_TM, oa.shape[1]), lambda i, f: (i, 0)),
            pl.BlockSpec((FFN_TM, ob.shape[1]), lambda i, f: (i, 0)),
            pl.BlockSpec(wa.shape, lambda i, f: (0, 0), pipeline_mode=pl.Buffered(1)),
            pl.BlockSpec(wb.shape, lambda i, f: (0, 0), pipeline_mode=pl.Buffered(1)),
        ]
        operands += [oa, ob, wa, wb]
        scratch.append(pltpu.VMEM((FFN_TM, d), F32))
    out_shape = [jax.ShapeDtypeStruct((m, d), F32)]
    out_specs = [row_spec]
    if epilogue == "prenorm":
        out_shape.append(jax.ShapeDtypeStruct((m, d), BF16))
        out_specs.append(row_spec)
    return pl.pallas_call(
        functools.partial(_ffn_kernel, mod_base=mod_base, epilogue=epilogue,
                          mix_gate_row=mix_gate_row),
        grid=(m // FFN_TM, n_f),
        in_specs=in_specs,
        out_specs=out_specs,
        out_shape=out_shape,
        scratch_shapes=scratch,
        compiler_params=pltpu.CompilerParams(
            dimension_semantics=("arbitrary", "arbitrary"),
            vmem_limit_bytes=VMEM_LIMIT_BYTES),
        name="ffn_" + epilogue,
    )(*operands)


_LEVEL_HALVES = tuple(CHUNK >> (j + 1) for j in range(CHUNK.bit_length() - 1))
_N_LEVELS = len(_LEVEL_HALVES)
_CS_BLOCKS = _N_LEVELS + 2
_GROUP = 4
_GROUP_ROWS = _GROUP * CHUNK


def _chunk_constants():
    t = np.arange(CHUNK)
    tri = (t[None, :] <= t[:, None]).astype(np.float32)
    blocks = [tri]
    masks = []
    for half in _LEVEL_HALVES:
        ref = (t // (2 * half)) * (2 * half) + half
        blocks.append(tri - tri[ref])
        same_block = (t[:, None] // (2 * half)) == (t[None, :] // (2 * half))
        is_query = (t % (2 * half)) >= half
        masks.append((same_block & is_query[:, None] & ~is_query[None, :]).astype(np.float32))
    blocks.append(1.0 - tri)
    masks.append(np.eye(CHUNK, dtype=np.float32))
    cm = np.concatenate(blocks, axis=0)
    cm2 = np.concatenate([cm, cm], axis=1)
    group_masks = np.stack([np.kron(np.eye(_GROUP, dtype=np.float32), m) for m in masks], axis=0)
    return jnp.asarray(cm2, dtype=BF16), jnp.asarray(group_masks, dtype=F32)


def _mixer_kernel(*refs, tiles_per_head, **static):
    *io_refs, z_a, z_b, q_scr, k_scr, cs_scr, oi_scr, u_scr, s_scr = refs
    scratch = (q_scr, k_scr, cs_scr, oi_scr, u_scr, s_scr)
    s = pl.program_id(0)

    @pl.when(s == 0)
    def _():
        z_b[...] = jnp.zeros_like(z_b)

    @pl.when(lax.rem(jnp.maximum(s - 1, 0), tiles_per_head) == 0)
    def _():
        s_scr[...] = jnp.zeros_like(s_scr)

    @pl.when(lax.rem(s, 2) == 0)
    def _():
        _mixer_tile(io_refs, scratch, z_b, z_a, **static)

    @pl.when(lax.rem(s, 2) == 1)
    def _():
        _mixer_tile(io_refs, scratch, z_a, z_b, **static)


def _mixer_tile(io_refs, scratch, z_scr, z_next, *, kind, layer, dk, dv, tc):
    if kind == "gla":
        h_ref, w_ref, w2_ref, b2_ref, nw_ref, cm_ref, mk_ref, o_ref = io_refs
    else:
        h_ref, w_ref, lb_ref, nw_ref, cm_ref, mk_ref, o_ref = io_refs
    q_scr, k_scr, cs_scr, oi_scr, u_scr, s_scr = scratch
    n_chunks = tc // CHUNK
    v_cols = slice(2 * dk, 2 * dk + dv)
    gate_cols = slice(2 * dk + dv, 2 * dk + 2 * dv)

    ncols = w_ref.shape[1]
    pieces = [slice(c0, min(c0 + MXU_N, ncols)) for c0 in range(0, ncols, MXU_N)]
    n_slots = n_chunks // _GROUP + 2

    def project_pieces(slot):
        lo = slot * len(pieces) // n_slots
        hi = (slot + 1) * len(pieces) // n_slots
        for cols in pieces[lo:hi]:
            z_next[:, cols] = jnp.dot(h_ref[...], w_ref[:, cols], preferred_element_type=F32)

    project_pieces(0)

    if kind == "gla":
        q_scr[...] = z_scr[:, 0:dk] * (dk ** -0.5)
        k_scr[...] = z_scr[:, dk:2 * dk]
        gr = z_scr[:, 2 * dk + 2 * dv:2 * dk + 2 * dv + LANES]
        gp = jnp.dot(gr.astype(BF16), w2_ref[...], preferred_element_type=F32) + b2_ref[...]
        la = (jnp.minimum(gp, 0.0) - jnp.log(1.0 + jnp.exp(-jnp.abs(gp)))) * (
            1.0 / GLA_GATE_NORMALIZER)
    else:
        raw = lb_ref[...]
        ex = jnp.exp(raw - jnp.max(raw, axis=0, keepdims=True))
        p = ex / jnp.sum(ex, axis=0, keepdims=True)
        lb = jnp.sum(p[0:layer + 1, :], axis=0, keepdims=True)
        hq = z_scr[:, 0:dk]
        fr = z_scr[:, dk:2 * dk]
        q_scr[...] = hq * _sigmoid(hq)
        en = jnp.exp(-jnp.abs(fr))
        rr = 1.0 / (1.0 + en)
        sig_pos = jnp.where(fr >= 0.0, rr, en * rr)
        sig_neg = jnp.where(fr >= 0.0, en * rr, rr)
        la = jnp.log(lb + (1.0 - lb) * sig_pos)
        k_scr[...] = (1.0 - lb) * sig_neg

    la_wide = jnp.concatenate([la[c * CHUNK:(c + 1) * CHUNK] for c in range(n_chunks)], axis=1)
    la_hi = la_wide.astype(BF16)
    la_lo = (la_wide - la_hi.astype(F32)).astype(BF16)
    cs_scr[...] = jnp.dot(cm_ref[...], jnp.concatenate([la_hi, la_lo], axis=0),
                          preferred_element_type=F32)

    def cs_block(block, c):
        return cs_scr[block * CHUNK:(block + 1) * CHUNK, c * dk:(c + 1) * dk]

    for g in range(n_chunks // _GROUP):
        project_pieces(g + 1)
        chunks = range(g * _GROUP, (g + 1) * _GROUP)
        rows = slice(g * _GROUP_ROWS, (g + 1) * _GROUP_ROWS)
        q = q_scr[rows, :]
        k = k_scr[rows, :]
        attn = mk_ref[_N_LEVELS] * jnp.sum(q * k, axis=-1, keepdims=True)
        for j in range(_N_LEVELS):
            d = jnp.concatenate([cs_block(j + 1, c) for c in chunks], axis=0)
            e = jnp.exp(-jnp.abs(d))
            s = lax.dot_general((q * e).astype(BF16), (k * e).astype(BF16), _NT,
                                preferred_element_type=F32)
            attn = attn + mk_ref[j] * s
        vb = z_scr[rows, v_cols].astype(BF16)
        oi_scr[rows, :] = jnp.dot(attn.astype(BF16), vb, preferred_element_type=F32)
        for i, c in enumerate(chunks):
            crow = slice(c * CHUNK, (c + 1) * CHUNK)
            kd = (k_scr[crow, :] * jnp.exp(cs_block(_N_LEVELS + 1, c))).astype(BF16)
            u_scr[c] = lax.dot_general(vb[i * CHUNK:(i + 1) * CHUNK], kd, _TN,
                                       preferred_element_type=F32)

    project_pieces(n_slots - 1)
    st = s_scr[...]
    for c in range(n_chunks):
        rows = slice(c * CHUNK, (c + 1) * CHUNK)
        b = cs_block(0, c)
        qb = (q_scr[rows, :] * jnp.exp(b)).astype(BF16)
        o = lax.dot_general(qb, st.astype(BF16), _NT, preferred_element_type=F32) + oi_scr[rows, :]
        st = st * jnp.exp(b[CHUNK - 1:CHUNK, :]) + u_scr[c]
        gate = z_scr[rows, gate_cols]
        o_ref[rows, :] = (_rms(o) * nw_ref[...] * (gate * _sigmoid(gate))).astype(BF16)
    s_scr[...] = st


def _mixer(h2, w_heads, extra, nw, consts, *, kind, layer, batch, tokens_per_batch, dk, dv):
    m, d = h2.shape
    n_heads, _, ncols = w_heads.shape
    tc = MIX_TC
    assert tokens_per_batch % tc == 0 and tc % _GROUP_ROWS == 0
    nt = tokens_per_batch // tc
    n_tiles = batch * n_heads * nt
    cm, mk = consts

    def coords(tile):
        bb = tile // (n_heads * nt)
        hh = lax.rem(tile // nt, n_heads)
        return bb * nt + lax.rem(tile, nt), hh

    def projected(s):
        return coords(jnp.minimum(s, n_tiles - 1))

    def consumed(s):
        return coords(jnp.maximum(s - 1, 0))

    const2 = lambda s: (0, 0)
    in_specs = [
        pl.BlockSpec((tc, d), lambda s: (projected(s)[0], 0)),
        pl.BlockSpec((None, d, ncols), lambda s: (projected(s)[1], 0, 0)),
    ]
    if kind == "gla":
        w2, b2 = extra
        in_specs += [
            pl.BlockSpec((None, LANES, dk), lambda s: (consumed(s)[1], 0, 0)),
            pl.BlockSpec((None, 1, dk), lambda s: (consumed(s)[1], 0, 0)),
        ]
    else:
        (lbraw,) = extra
        in_specs += [pl.BlockSpec((None, lbraw.shape[1], dk), lambda s: (consumed(s)[1], 0, 0))]
    in_specs += [
        pl.BlockSpec((1, dv), const2),
        pl.BlockSpec(cm.shape, const2),
        pl.BlockSpec(mk.shape, lambda s: (0, 0, 0)),
    ]
    return pl.pallas_call(
        functools.partial(_mixer_kernel, tiles_per_head=nt, kind=kind, layer=layer, dk=dk, dv=dv,
                          tc=tc),
        grid=(n_tiles + 1,),
        in_specs=in_specs,
        out_specs=pl.BlockSpec((tc, dv), lambda s: consumed(s)),
        out_shape=jax.ShapeDtypeStruct((m, n_heads * dv), BF16),
        scratch_shapes=[
            pltpu.VMEM((tc, ncols), F32),
            pltpu.VMEM((tc, ncols), F32),
            pltpu.VMEM((tc, dk), F32),
            pltpu.VMEM((tc, dk), F32),
            pltpu.VMEM((_CS_BLOCKS * CHUNK, (tc // CHUNK) * dk), F32),
            pltpu.VMEM((tc, dv), F32),
            pltpu.VMEM((tc // CHUNK, dv, dk), F32),
            pltpu.VMEM((dv, dk), F32),
        ],
        compiler_params=pltpu.CompilerParams(
            dimension_semantics=("arbitrary",), vmem_limit_bytes=VMEM_LIMIT_BYTES),
        name="mixer_" + kind,
    )(h2, w_heads, *extra, nw, cm, mk)


def _wi_prep_kernel(w_ref, o_ref, *, d_ff, n_f):
    rows = w_ref.shape[0]
    for j in range(n_f):
        for i in range(FFN_TF // MXU_N):
            c0 = j * FFN_TF + i * MXU_N
            valid = max(0, min(MXU_N, d_ff - c0))
            for half in range(2):
                src = half * d_ff + c0
                parts = []
                if valid:
                    parts.append(w_ref[:, src:src + valid].astype(BF16))
                if valid < MXU_N:
                    parts.append(jnp.zeros((rows, MXU_N - valid), BF16))
                piece = parts[0] if len(parts) == 1 else jnp.concatenate(parts, axis=1)
                o_ref[j, :, (2 * i + half) * MXU_N:(2 * i + half + 1) * MXU_N] = piece


def _prep_ffn_weights(wi, wo):
    d, d_ff = wi.shape[0], wo.shape[0]
    assert d_ff % LANES == 0 and d % WPREP_ROWS == 0
    n_f = -(-d_ff // FFN_TF)
    wab = pl.pallas_call(
        functools.partial(_wi_prep_kernel, d_ff=d_ff, n_f=n_f),
        grid=(d // WPREP_ROWS,),
        in_specs=[pl.BlockSpec((WPREP_ROWS, 2 * d_ff), lambda r: (r, 0))],
        out_specs=pl.BlockSpec((n_f, WPREP_ROWS, 2 * FFN_TF), lambda r: (0, r, 0)),
        out_shape=jax.ShapeDtypeStruct((n_f, d, 2 * FFN_TF), BF16),
        compiler_params=pltpu.CompilerParams(
            dimension_semantics=("arbitrary",), vmem_limit_bytes=VMEM_LIMIT_BYTES),
        name="wi_prep",
    )(wi)
    wob = jnp.pad(wo.astype(BF16), ((0, n_f * FFN_TF - d_ff), (0, 0)))
    return wab, wob


_GLA_QK = GLA_HEADS * GLA_DK
_GLA_V = GLA_HEADS * GLA_DV
_HGRN_K = HGRN_HEADS * HGRN_DK
_HGRN_V = HGRN_HEADS * HGRN_DV
_IN_OFFS = tuple(int(o) for o in np.cumsum(
    [0, _GLA_QK, _GLA_QK, _GLA_V, _GLA_V, GLA_GATE_RANK, _HGRN_K, _HGRN_K, _HGRN_V, _HGRN_V]))


def _win_prep_kernel(w_ref, gr_ref, og_ref, oh_ref):
    def cols(part, h, width):
        c0 = _IN_OFFS[part] + h * width
        return w_ref[:, c0:c0 + width].astype(BF16)

    for h in range(GLA_HEADS):
        og_ref[h] = jnp.concatenate(
            [cols(0, h, GLA_DK), cols(1, h, GLA_DK), cols(2, h, GLA_DV), cols(3, h, GLA_DV),
             gr_ref[...]], axis=1)
    for h in range(HGRN_HEADS):
        oh_ref[h] = jnp.concatenate(
            [cols(5, h, HGRN_DK), cols(6, h, HGRN_DK), cols(7, h, HGRN_DV), cols(8, h, HGRN_DV)],
            axis=1)


def _prep_mixer_weights(w_in, layer, w2, b2):
    _, d, in_width = w_in.shape
    assert in_width == _IN_OFFS[-1] and d % WPREP_ROWS == 0
    gla_cols = 2 * GLA_DK + 2 * GLA_DV + LANES
    hgrn_cols = 2 * HGRN_DK + 2 * HGRN_DV
    gr = jnp.pad(w_in[layer, :, _IN_OFFS[4]:_IN_OFFS[5]].astype(BF16),
                 ((0, 0), (0, LANES - GLA_GATE_RANK)))
    w_gla, w_hg = pl.pallas_call(
        _win_prep_kernel,
        grid=(d // WPREP_ROWS,),
        in_specs=[pl.BlockSpec((None, WPREP_ROWS, in_width), lambda r: (layer, r, 0)),
                  pl.BlockSpec((WPREP_ROWS, LANES), lambda r: (r, 0))],
        out_specs=[pl.BlockSpec((GLA_HEADS, WPREP_ROWS, gla_cols), lambda r: (0, r, 0)),
                   pl.BlockSpec((HGRN_HEADS, WPREP_ROWS, hgrn_cols), lambda r: (0, r, 0))],
        out_shape=[jax.ShapeDtypeStruct((GLA_HEADS, d, gla_cols), BF16),
                   jax.ShapeDtypeStruct((HGRN_HEADS, d, hgrn_cols), BF16)],
        compiler_params=pltpu.CompilerParams(
            dimension_semantics=("arbitrary",), vmem_limit_bytes=VMEM_LIMIT_BYTES),
        name="win_prep",
    )(w_in, gr)
    w2h = jnp.pad(w2.astype(BF16), ((0, LANES - GLA_GATE_RANK), (0, 0)))
    w2h = w2h.reshape(LANES, GLA_HEADS, GLA_DK).transpose(1, 0, 2)
    b2h = b2.reshape(GLA_HEADS, 1, GLA_DK)
    return w_gla, w_hg, w2h, b2h


def kernel(x, c, ada_w, ada_b, norm_ffn1_w, ffn1_wi, ffn1_wo, norm_mix_w, w_in, gla_gate_w2,
           gla_gate_b2, gla_norm_w, hgrn_norm_w, hgrn_lower_bounds, w_out, norm_ffn2_w, ffn2_wi,
           ffn2_wo, final_norm_w):
    batch, seq, d = x.shape
    depth = ada_w.shape[0]
    m = batch * seq
    consts = _chunk_constants()
    xc = x.reshape(m, d)
    c_pad = jnp.pad(c, ((0, SUBLANES - batch % SUBLANES), (0, 0))) if batch % SUBLANES else c
    gla_v = GLA_HEADS * GLA_DV
    lb_heads = hgrn_lower_bounds.astype(F32).reshape(depth + 1, HGRN_HEADS, HGRN_DK).transpose(1, 0, 2)

    for l in range(depth):
        mod = _adaln(c_pad, ada_w[l], ada_b[l][None, :])[:batch].reshape(batch, N_MOD, d)
        wab1, wo1 = _prep_ffn_weights(ffn1_wi[l], ffn1_wo[l])
        wab2, wo2 = _prep_ffn_weights(ffn2_wi[l], ffn2_wo[l])
        w_gla, w_hg, w2h, b2h = _prep_mixer_weights(w_in, l, gla_gate_w2[l], gla_gate_b2[l])
        wout = w_out[l].astype(BF16)

        x1, h2 = _ffn(xc, mod, norm_ffn1_w[l][None, :], norm_mix_w[l][None, :], wab1, wo1,
                      tokens_per_batch=seq, mod_base=0, epilogue="prenorm")
        o_gla = _mixer(h2, w_gla, (w2h, b2h), gla_norm_w[l][None, :], consts, kind="gla",
                       layer=l, batch=batch, tokens_per_batch=seq, dk=GLA_DK, dv=GLA_DV)
        o_hg = _mixer(h2, w_hg, (lb_heads,), hgrn_norm_w[l][None, :], consts, kind="hgrn",
                      layer=l, batch=batch, tokens_per_batch=seq, dk=HGRN_DK, dv=HGRN_DV)
        last = l == depth - 1
        nw2 = final_norm_w[None, :] if last else norm_ffn2_w[l][None, :]
        (xc,) = _ffn(x1, mod, norm_ffn2_w[l][None, :], nw2, wab2, wo2,
                     tokens_per_batch=seq, mod_base=6, epilogue="final" if last else "none",
                     mix=(o_gla, o_hg, wout[:gla_v], wout[gla_v:]), mix_gate_row=5)
    return xc.reshape(batch, seq, d)
```

```python
import functools

import jax
import jax.numpy as jnp
import numpy as np
from jax import lax
from jax.experimental import pallas as pl
from jax.experimental.pallas import tpu as pltpu

F32 = jnp.float32
BF16 = jnp.bfloat16

GLA_HEADS = 4
GLA_DK = 128
GLA_DV = 256
GLA_GATE_RANK = 16
GLA_GATE_NORMALIZER = 16.0
HGRN_HEADS = 8
HGRN_DK = 128
HGRN_DV = 128
CHUNK = 64
MACARON_W = 0.5
N_MOD = 9
EPS = 1e-6

LANES = 128
SUBLANES = 8
MXU_N = 256
VMEM_LIMIT_BYTES = 56 * 1024 * 1024

FFN_TM = 512
FFN_TF = 512
PROLOGUE_ROWS = 256
NORM_ROWS = 16
MIX_TC = 1024
ADALN_TN = 1024
WPREP_ROWS = 256

_NT = (((1,), (1,)), ((), ()))
_TN = (((0,), (0,)), ((), ()))


def _sigmoid(x):
    return jax.nn.sigmoid(x)


def _rms(x):
    return x * lax.rsqrt(jnp.mean(x * x, axis=-1, keepdims=True) + EPS)


def _adaln_kernel(c_ref, w_ref, b_ref, o_ref):
    c = c_ref[...]
    ca = (c * _sigmoid(c)).astype(BF16)
    o_ref[...] = jnp.dot(ca, w_ref[...].astype(BF16), preferred_element_type=F32) + b_ref[...]


def _adaln(c_pad, w, b):
    rows, d = c_pad.shape
    n = w.shape[1]
    assert n % ADALN_TN == 0
    return pl.pallas_call(
        _adaln_kernel,
        grid=(n // ADALN_TN,),
        in_specs=[
            pl.BlockSpec((rows, d), lambda j: (0, 0)),
            pl.BlockSpec((d, ADALN_TN), lambda j: (0, j)),
            pl.BlockSpec((1, ADALN_TN), lambda j: (0, j)),
        ],
        out_specs=pl.BlockSpec((rows, ADALN_TN), lambda j: (0, j)),
        out_shape=jax.ShapeDtypeStruct((rows, n), F32),
        compiler_params=pltpu.CompilerParams(
            dimension_semantics=("arbitrary",), vmem_limit_bytes=VMEM_LIMIT_BYTES),
        name="adaln",
    )(c_pad, w, b)


def _ffn_kernel(*refs, mod_base, epilogue, mix_gate_row):
    x_ref, mod_ref, nw_ref, nw2_ref, wab_ref, wo_ref, *rest = refs
    if mix_gate_row is not None:
        oa_ref, ob_ref, wa_ref, wb_ref, *rest = rest
    if epilogue == "prenorm":
        xo_ref, ho_ref, h_scr, *rest = rest
    else:
        xo_ref, h_scr, *rest = rest
    xin_ref = rest[0] if mix_gate_row is not None else x_ref
    f = pl.program_id(1)
    last = pl.num_programs(1) - 1

    def prologue():
        shift = mod_ref[mod_base:mod_base + 1, :]
        gain = nw_ref[...] * (1.0 + mod_ref[mod_base + 1:mod_base + 2, :])
        for r0 in range(0, FFN_TM, PROLOGUE_ROWS):
            rows = slice(r0, r0 + PROLOGUE_ROWS)
            if mix_gate_row is not None:
                y = (jnp.dot(oa_ref[rows, :], wa_ref[...], preferred_element_type=F32)
                     + jnp.dot(ob_ref[rows, :], wb_ref[...], preferred_element_type=F32))
                xin_ref[rows, :] = x_ref[rows, :] + mod_ref[mix_gate_row:mix_gate_row + 1, :] * y
            for c0 in range(r0, r0 + PROLOGUE_ROWS, NORM_ROWS):
                chunk = slice(c0, c0 + NORM_ROWS)
                h_scr[chunk, :] = (_rms(xin_ref[chunk, :]) * gain + shift).astype(BF16)

    def swiglu_step(first):
        zab = jnp.dot(h_scr[...], wab_ref[...], preferred_element_type=F32)
        pieces = FFN_TF // MXU_N
        a = jnp.concatenate([zab[:, (2 * i) * MXU_N:(2 * i + 1) * MXU_N] for i in range(pieces)],
                            axis=1)
        b = jnp.concatenate([zab[:, (2 * i + 1) * MXU_N:(2 * i + 2) * MXU_N]
                             for i in range(pieces)], axis=1)
        act = (a * _sigmoid(a) * b).astype(BF16)
        update = jnp.dot(act, wo_ref[...], preferred_element_type=F32)
        if first:
            xo_ref[...] = update
        else:
            xo_ref[...] += update

    def finish():
        gate = MACARON_W * mod_ref[mod_base + 2:mod_base + 3, :]
        if epilogue == "prenorm":
            shift2 = mod_ref[mod_base + 3:mod_base + 4, :]
            gain2 = nw2_ref[...] * (1.0 + mod_ref[mod_base + 4:mod_base + 5, :])
        for c0 in range(0, FFN_TM, NORM_ROWS):
            chunk = slice(c0, c0 + NORM_ROWS)
            xn = xin_ref[chunk, :] + gate * xo_ref[chunk, :]
            if epilogue == "prenorm":
                xo_ref[chunk, :] = xn
                ho_ref[chunk, :] = (_rms(xn) * gain2 + shift2).astype(BF16)
            elif epilogue == "final":
                xo_ref[chunk, :] = _rms(xn) * nw2_ref[...]
            else:
                xo_ref[chunk, :] = xn

    @pl.when(f == 0)
    def _():
        prologue()
        swiglu_step(first=True)

    @pl.when(jnp.logical_and(f > 0, f < last))
    def _():
        swiglu_step(first=False)

    @pl.when(f == last)
    def _():
        swiglu_step(first=False)
        finish()


def _ffn(x2d, mod, nw, nw2, wab, wo, *, tokens_per_batch, mod_base, epilogue, mix=None,
         mix_gate_row=None):
    m, d = x2d.shape
    n_f = wab.shape[0]
    assert m % FFN_TM == 0 and tokens_per_batch % FFN_TM == 0 and wo.shape[0] == n_f * FFN_TF
    assert (mix is None) == (mix_gate_row is None) and n_f >= 2
    tiles_per_batch = tokens_per_batch // FFN_TM
    row_spec = pl.BlockSpec((FFN_TM, d), lambda i, f: (i, 0))
    vec_spec = pl.BlockSpec((1, d), lambda i, f: (0, 0))
    in_specs = [
        row_spec,
        pl.BlockSpec((None, N_MOD, d), lambda i, f: (i // tiles_per_batch, 0, 0)),
        vec_spec,
        vec_spec,
        pl.BlockSpec((None, d, 2 * FFN_TF), lambda i, f: (f, 0, 0)),
        pl.BlockSpec((FFN_TF, d), lambda i, f: (f, 0)),
    ]
    operands = [x2d, mod, nw, nw2, wab, wo]
    scratch = [pltpu.VMEM((FFN_TM, d), BF16)]
    if mix is not None:
        oa, ob, wa, wb = mix
        in_specs += [
            pl.BlockSpec((FFN_TM, oa.shape[1]), lambda i, f: (i, 0)),
            pl.BlockSpec((FFN_TM, ob.shape[1]), lambda i, f: (i, 0)),
            pl.BlockSpec(wa.shape, lambda i, f: (0, 0), pipeline_mode=pl.Buffered(1)),
            pl.BlockSpec(wb.shape, lambda i, f: (0, 0), pipeline_mode=pl.Buffered(1)),
        ]
        operands += [oa, ob, wa, wb]
        scratch.append(pltpu.VMEM((FFN_TM, d), F32))
    out_shape = [jax.ShapeDtypeStruct((m, d), F32)]
    out_specs = [row_spec]
    if epilogue == "prenorm":
        out_shape.append(jax.ShapeDtypeStruct((m, d), BF16))
        out_specs.append(row_spec)
    return pl.pallas_call(
        functools.partial(_ffn_kernel, mod_base=mod_base, epilogue=epilogue,
                          mix_gate_row=mix_gate_row),
        grid=(m // FFN_TM, n_f),
        in_specs=in_specs,
        out_specs=out_specs,
        out_shape=out_shape,
        scratch_shapes=scratch,
        compiler_params=pltpu.CompilerParams(
            dimension_semantics=("arbitrary", "arbitrary"),
            vmem_limit_bytes=VMEM_LIMIT_BYTES),
        name="ffn_" + epilogue,
    )(*operands)


_LEVEL_HALVES = tuple(CHUNK >> (j + 1) for j in range(CHUNK.bit_length() - 1))
_N_LEVELS = len(_LEVEL_HALVES)
_CS_BLOCKS = _N_LEVELS + 2
_GROUP = 4
_GROUP_ROWS = _GROUP * CHUNK


def _chunk_constants():
    t = np.arange(CHUNK)
    tri = (t[None, :] <= t[:, None]).astype(np.float32)
    blocks = [tri]
    masks = []
    for half in _LEVEL_HALVES:
        ref = (t // (2 * half)) * (2 * half) + half
        blocks.append(tri - tri[ref])
        same_block = (t[:, None] // (2 * half)) == (t[None, :] // (2 * half))
        is_query = (t % (2 * half)) >= half
        masks.append((same_block & is_query[:, None] & ~is_query[None, :]).astype(np.float32))
    blocks.append(1.0 - tri)
    masks.append(np.eye(CHUNK, dtype=np.float32))
    cm = np.concatenate(blocks, axis=0)
    cm2 = np.concatenate([cm, cm], axis=1)
    group_masks = np.stack([np.kron(np.eye(_GROUP, dtype=np.float32), m) for m in masks], axis=0)
    return jnp.asarray(cm2, dtype=BF16), jnp.asarray(group_masks, dtype=F32)


def _mixer_kernel(*refs, tiles_per_head, **static):
    *io_refs, z_a, z_b, q_scr, k_scr, cs_scr, oi_scr, u_scr, s_scr = refs
    scratch = (q_scr, k_scr, cs_scr, oi_scr, u_scr, s_scr)
    s = pl.program_id(0)

    @pl.when(s == 0)
    def _():
        z_b[...] = jnp.zeros_like(z_b)

    @pl.when(lax.rem(jnp.maximum(s - 1, 0), tiles_per_head) == 0)
    def _():
        s_scr[...] = jnp.zeros_like(s_scr)

    @pl.when(lax.rem(s, 2) == 0)
    def _():
        _mixer_tile(io_refs, scratch, z_b, z_a, **static)

    @pl.when(lax.rem(s, 2) == 1)
    def _():
        _mixer_tile(io_refs, scratch, z_a, z_b, **static)


def _mixer_tile(io_refs, scratch, z_scr, z_next, *, kind, layer, dk, dv, tc):
    if kind == "gla":
        h_ref, w_ref, w2_ref, b2_ref, nw_ref, cm_ref, mk_ref, o_ref = io_refs
    else:
        h_ref, w_ref, lb_ref, nw_ref, cm_ref, mk_ref, o_ref = io_refs
    q_scr, k_scr, cs_scr, oi_scr, u_scr, s_scr = scratch
    n_chunks = tc // CHUNK
    v_cols = slice(2 * dk, 2 * dk + dv)
    gate_cols = slice(2 * dk + dv, 2 * dk + 2 * dv)

    ncols = w_ref.shape[1]
    pieces = [slice(c0, min(c0 + MXU_N, ncols)) for c0 in range(0, ncols, MXU_N)]
    n_slots = n_chunks // _GROUP + 2

    def project_pieces(slot):
        lo = slot * len(pieces) // n_slots
        hi = (slot + 1) * len(pieces) // n_slots
        for cols in pieces[lo:hi]:
            z_next[:, cols] = jnp.dot(h_ref[...], w_ref[:, cols], preferred_element_type=F32)

    project_pieces(0)

    if kind == "gla":
        q_scr[...] = z_scr[:, 0:dk] * (dk ** -0.5)
        k_scr[...] = z_scr[:, dk:2 * dk]
        gr = z_scr[:, 2 * dk + 2 * dv:2 * dk + 2 * dv + LANES]
        gp = jnp.dot(gr.astype(BF16), w2_ref[...], preferred_element_type=F32) + b2_ref[...]
        la = (jnp.minimum(gp, 0.0) - jnp.log(1.0 + jnp.exp(-jnp.abs(gp)))) * (
            1.0 / GLA_GATE_NORMALIZER)
    else:
        raw = lb_ref[...]
        ex = jnp.exp(raw - jnp.max(raw, axis=0, keepdims=True))
        p = ex / jnp.sum(ex, axis=0, keepdims=True)
        lb = jnp.sum(p[0:layer + 1, :], axis=0, keepdims=True)
        hq = z_scr[:, 0:dk]
        fr = z_scr[:, dk:2 * dk]
        q_scr[...] = hq * _sigmoid(hq)
        en = jnp.exp(-jnp.abs(fr))
        rr = 1.0 / (1.0 + en)
        sig_pos = jnp.where(fr >= 0.0, rr, en * rr)
        sig_neg = jnp.where(fr >= 0.0, en * rr, rr)
        la = jnp.log(lb + (1.0 - lb) * sig_pos)
        k_scr[...] = (1.0 - lb) * sig_neg

    la_wide = jnp.concatenate([la[c * CHUNK:(c + 1) * CHUNK] for c in range(n_chunks)], axis=1)
    la_hi = la_wide.astype(BF16)
    la_lo = (la_wide - la_hi.astype(F32)).astype(BF16)
    cs_scr[...] = jnp.dot(cm_ref[...], jnp.concatenate([la_hi, la_lo], axis=0),
                          preferred_element_type=F32)

    def cs_block(block, c):
        return cs_scr[block * CHUNK:(block + 1) * CHUNK, c * dk:(c + 1) * dk]

    for g in range(n_chunks // _GROUP):
        project_pieces(g + 1)
        chunks = range(g * _GROUP, (g + 1) * _GROUP)
        rows = slice(g * _GROUP_ROWS, (g + 1) * _GROUP_ROWS)
        q = q_scr[rows, :]
        k = k_scr[rows, :]
        attn = mk_ref[_N_LEVELS] * jnp.sum(q * k, axis=-1, keepdims=True)
        for j in range(_N_LEVELS):
            d = jnp.concatenate([cs_block(j + 1, c) for c in chunks], axis=0)
            e = jnp.exp(-jnp.abs(d))
            s = lax.dot_general((q * e).astype(BF16), (k * e).astype(BF16), _NT,
                                preferred_element_type=F32)
            attn = attn + mk_ref[j] * s
        vb = z_scr[rows, v_cols].astype(BF16)
        oi_scr[rows, :] = jnp.dot(attn.astype(BF16), vb, preferred_element_type=F32)
        for i, c in enumerate(chunks):
            crow = slice(c * CHUNK, (c + 1) * CHUNK)
            kd = (k_scr[crow, :] * jnp.exp(cs_block(_N_LEVELS + 1, c))).astype(BF16)
            u_scr[c] = lax.dot_general(vb[i * CHUNK:(i + 1) * CHUNK], kd, _TN,
                                       preferred_element_type=F32)

    project_pieces(n_slots - 1)
    st = s_scr[...]
    for c in range(n_chunks):
        rows = slice(c * CHUNK, (c + 1) * CHUNK)
        b = cs_block(0, c)
        qb = (q_scr[rows, :] * jnp.exp(b)).astype(BF16)
        o = lax.dot_general(qb, st.astype(BF16), _NT, preferred_element_type=F32) + oi_scr[rows, :]
        st = st * jnp.exp(b[CHUNK - 1:CHUNK, :]) + u_scr[c]
        gate = z_scr[rows, gate_cols]
        o_ref[rows, :] = (_rms(o) * nw_ref[...] * (gate * _sigmoid(gate))).astype(BF16)
    s_scr[...] = st


def _mixer(h2, w_heads, extra, nw, consts, *, kind, layer, batch, tokens_per_batch, dk, dv):
    m, d = h2.shape
    n_heads, _, ncols = w_heads.shape
    tc = MIX_TC
    assert tokens_per_batch % tc == 0 and tc % _GROUP_ROWS == 0
    nt = tokens_per_batch // tc
    n_tiles = batch * n_heads * nt
    cm, mk = consts

    def coords(tile):
        bb = tile // (n_heads * nt)
        hh = lax.rem(tile // nt, n_heads)
        return bb * nt + lax.rem(tile, nt), hh

    def projected(s):
        return coords(jnp.minimum(s, n_tiles - 1))

    def consumed(s):
        return coords(jnp.maximum(s - 1, 0))

    const2 = lambda s: (0, 0)
    in_specs = [
        pl.BlockSpec((tc, d), lambda s: (projected(s)[0], 0)),
        pl.BlockSpec((None, d, ncols), lambda s: (projected(s)[1], 0, 0)),
    ]
    if kind == "gla":
        w2, b2 = extra
        in_specs += [
            pl.BlockSpec((None, LANES, dk), lambda s: (consumed(s)[1], 0, 0)),
            pl.BlockSpec((None, 1, dk), lambda s: (consumed(s)[1], 0, 0)),
        ]
    else:
        (lbraw,) = extra
        in_specs += [pl.BlockSpec((None, lbraw.shape[1], dk), lambda s: (consumed(s)[1], 0, 0))]
    in_specs += [
        pl.BlockSpec((1, dv), const2),
        pl.BlockSpec(cm.shape, const2),
        pl.BlockSpec(mk.shape, lambda s: (0, 0, 0)),
    ]
    return pl.pallas_call(
        functools.partial(_mixer_kernel, tiles_per_head=nt, kind=kind, layer=layer, dk=dk, dv=dv,
                          tc=tc),
        grid=(n_tiles + 1,),
        in_specs=in_specs,
        out_specs=pl.BlockSpec((tc, dv), lambda s: consumed(s)),
        out_shape=jax.ShapeDtypeStruct((m, n_heads * dv), BF16),
        scratch_shapes=[
            pltpu.VMEM((tc, ncols), F32),
            pltpu.VMEM((tc, ncols), F32),
            pltpu.VMEM((tc, dk), F32),
            pltpu.VMEM((tc, dk), F32),
            pltpu.VMEM((_CS_BLOCKS * CHUNK, (tc // CHUNK) * dk), F32),
            pltpu.VMEM((tc, dv), F32),
            pltpu.VMEM((tc // CHUNK, dv, dk), F32),
            pltpu.VMEM((dv, dk), F32),
        ],
        compiler_params=pltpu.CompilerParams(
            dimension_semantics=("arbitrary",), vmem_limit_bytes=VMEM_LIMIT_BYTES),
        name="mixer_" + kind,
    )(h2, w_heads, *extra, nw, cm, mk)


def _wi_prep_kernel(w_ref, o_ref, *, d_ff, n_f):
    rows = w_ref.shape[0]
    for j in range(n_f):
        for i in range(FFN_TF // MXU_N):
            c0 = j * FFN_TF + i * MXU_N
            valid = max(0, min(MXU_N, d_ff - c0))
            for half in range(2):
                src = half * d_ff + c0
                parts = []
                if valid:
                    parts.append(w_ref[:, src:src + valid].astype(BF16))
                if valid < MXU_N:
                    parts.append(jnp.zeros((rows, MXU_N - valid), BF16))
                piece = parts[0] if len(parts) == 1 else jnp.concatenate(parts, axis=1)
                o_ref[j, :, (2 * i + half) * MXU_N:(2 * i + half + 1) * MXU_N] = piece


def _wo_prep_kernel(w_ref, *rest, n_full):
    *tail_refs, o_ref = rest
    r = pl.program_id(0)

    @pl.when(r < n_full)
    def _():
        o_ref[...] = w_ref[...].astype(BF16)

    @pl.when(r >= n_full)
    def _():
        rows = 0
        for t_ref in tail_refs:
            o_ref[rows:rows + LANES, :] = t_ref[...].astype(BF16)
            rows += LANES
        o_ref[rows:, :] = jnp.zeros((FFN_TF - rows, o_ref.shape[1]), BF16)


def _prep_ffn_weights(wi, wo):
    d, d_ff = wi.shape[0], wo.shape[0]
    assert d_ff % LANES == 0 and d % WPREP_ROWS == 0
    n_f = -(-d_ff // FFN_TF)
    wab = pl.pallas_call(
        functools.partial(_wi_prep_kernel, d_ff=d_ff, n_f=n_f),
        grid=(d // WPREP_ROWS,),
        in_specs=[pl.BlockSpec((WPREP_ROWS, 2 * d_ff), lambda r: (r, 0))],
        out_specs=pl.BlockSpec((n_f, WPREP_ROWS, 2 * FFN_TF), lambda r: (0, r, 0)),
        out_shape=jax.ShapeDtypeStruct((n_f, d, 2 * FFN_TF), BF16),
        compiler_params=pltpu.CompilerParams(
            dimension_semantics=("arbitrary",), vmem_limit_bytes=VMEM_LIMIT_BYTES),
        name="wi_prep",
    )(wi)
    n_full = d_ff // FFN_TF
    n_tail = (d_ff - n_full * FFN_TF) // LANES
    tail0 = n_full * FFN_TF // LANES
    tail_specs = [pl.BlockSpec((LANES, d), functools.partial(lambda r, k: (tail0 + k, 0), k=k))
                  for k in range(n_tail)]
    wob = pl.pallas_call(
        functools.partial(_wo_prep_kernel, n_full=n_full),
        grid=(n_f,),
        in_specs=[pl.BlockSpec((FFN_TF, d), lambda r: (jnp.minimum(r, n_full - 1), 0))]
        + tail_specs,
        out_specs=pl.BlockSpec((FFN_TF, d), lambda r: (r, 0)),
        out_shape=jax.ShapeDtypeStruct((n_f * FFN_TF, d), BF16),
        compiler_params=pltpu.CompilerParams(
            dimension_semantics=("arbitrary",), vmem_limit_bytes=VMEM_LIMIT_BYTES),
        name="wo_prep",
    )(wo, *([wo] * n_tail))
    return wab, wob


_GLA_QK = GLA_HEADS * GLA_DK
_GLA_V = GLA_HEADS * GLA_DV
_HGRN_K = HGRN_HEADS * HGRN_DK
_HGRN_V = HGRN_HEADS * HGRN_DV
_IN_OFFS = tuple(int(o) for o in np.cumsum(
    [0, _GLA_QK, _GLA_QK, _GLA_V, _GLA_V, GLA_GATE_RANK, _HGRN_K, _HGRN_K, _HGRN_V, _HGRN_V]))


def _win_prep_kernel(wt_ref, og_ref, oh_ref):
    def cols(part, h, width):
        f0 = _IN_OFFS[part] + h * width
        return wt_ref[f0:f0 + width, :].T.astype(BF16)

    gr = wt_ref[_IN_OFFS[4]:_IN_OFFS[5], :]
    gr = jnp.concatenate([gr, jnp.zeros((LANES - GLA_GATE_RANK, gr.shape[1]), F32)], axis=0)
    gr = gr.T.astype(BF16)
    for h in range(GLA_HEADS):
        og_ref[h] = jnp.concatenate(
            [cols(0, h, GLA_DK), cols(1, h, GLA_DK), cols(2, h, GLA_DV), cols(3, h, GLA_DV), gr],
            axis=1)
    for h in range(HGRN_HEADS):
        oh_ref[h] = jnp.concatenate(
            [cols(5, h, HGRN_DK), cols(6, h, HGRN_DK), cols(7, h, HGRN_DV), cols(8, h, HGRN_DV)],
            axis=1)


def _prep_mixer_weights(w_in, layer, w2, b2):
    _, d, in_width = w_in.shape
    assert in_width == _IN_OFFS[-1] and d % WPREP_ROWS == 0
    gla_cols = 2 * GLA_DK + 2 * GLA_DV + LANES
    hgrn_cols = 2 * HGRN_DK + 2 * HGRN_DV
    wt = jnp.swapaxes(w_in, 1, 2)
    w_gla, w_hg = pl.pallas_call(
        _win_prep_kernel,
        grid=(d // WPREP_ROWS,),
        in_specs=[pl.BlockSpec((None, in_width, WPREP_ROWS), lambda r: (layer, 0, r))],
        out_specs=[pl.BlockSpec((GLA_HEADS, WPREP_ROWS, gla_cols), lambda r: (0, r, 0)),
                   pl.BlockSpec((HGRN_HEADS, WPREP_ROWS, hgrn_cols), lambda r: (0, r, 0))],
        out_shape=[jax.ShapeDtypeStruct((GLA_HEADS, d, gla_cols), BF16),
                   jax.ShapeDtypeStruct((HGRN_HEADS, d, hgrn_cols), BF16)],
        compiler_params=pltpu.CompilerParams(
            dimension_semantics=("arbitrary",), vmem_limit_bytes=VMEM_LIMIT_BYTES),
        name="win_prep",
    )(wt)
    w2h = jnp.pad(w2.astype(BF16), ((0, LANES - GLA_GATE_RANK), (0, 0)))
    w2h = w2h.reshape(LANES, GLA_HEADS, GLA_DK).transpose(1, 0, 2)
    b2h = b2.reshape(GLA_HEADS, 1, GLA_DK)
    return w_gla, w_hg, w2h, b2h


def kernel(x, c, ada_w, ada_b, norm_ffn1_w, ffn1_wi, ffn1_wo, norm_mix_w, w_in, gla_gate_w2,
           gla_gate_b2, gla_norm_w, hgrn_norm_w, hgrn_lower_bounds, w_out, norm_ffn2_w, ffn2_wi,
           ffn2_wo, final_norm_w):
    batch, seq, d = x.shape
    depth = ada_w.shape[0]
    m = batch * seq
    consts = _chunk_constants()
    xc = x.reshape(m, d)
    c_pad = jnp.pad(c, ((0, SUBLANES - batch % SUBLANES), (0, 0))) if batch % SUBLANES else c
    gla_v = GLA_HEADS * GLA_DV
    lb_heads = hgrn_lower_bounds.astype(F32).reshape(depth + 1, HGRN_HEADS, HGRN_DK).transpose(1, 0, 2)

    for l in range(depth):
        mod = _adaln(c_pad, ada_w[l], ada_b[l][None, :])[:batch].reshape(batch, N_MOD, d)
        wab1, wo1 = _prep_ffn_weights(ffn1_wi[l], ffn1_wo[l])
        wab2, wo2 = _prep_ffn_weights(ffn2_wi[l], ffn2_wo[l])
        w_gla, w_hg, w2h, b2h = _prep_mixer_weights(w_in, l, gla_gate_w2[l], gla_gate_b2[l])
        wout = w_out[l].astype(BF16)

        x1, h2 = _ffn(xc, mod, norm_ffn1_w[l][None, :], norm_mix_w[l][None, :], wab1, wo1,
                      tokens_per_batch=seq, mod_base=0, epilogue="prenorm")
        o_gla = _mixer(h2, w_gla, (w2h, b2h), gla_norm_w[l][None, :], consts, kind="gla",
                       layer=l, batch=batch, tokens_per_batch=seq, dk=GLA_DK, dv=GLA_DV)
        o_hg = _mixer(h2, w_hg, (lb_heads,), hgrn_norm_w[l][None, :], consts, kind="hgrn",
                      layer=l, batch=batch, tokens_per_batch=seq, dk=HGRN_DK, dv=HGRN_DV)
        last = l == depth - 1
        nw2 = final_norm_w[None, :] if last else norm_ffn2_w[l][None, :]
        (xc,) = _ffn(x1, mod, norm_ffn2_w[l][None, :], nw2, wab2, wo2,
                     tokens_per_batch=seq, mod_base=6, epilogue="final" if last else "none",
                     mix=(o_gla, o_hg, wout[:gla_v], wout[gla_v:]), mix_gate_row=5)
    return xc.reshape(batch, seq, d)
```

```python
import functools

import jax
import jax.numpy as jnp
import numpy as np
from jax import lax
from jax.experimental import pallas as pl
from jax.experimental.pallas import tpu as pltpu

F32 = jnp.float32
BF16 = jnp.bfloat16

GLA_HEADS = 4
GLA_DK = 128
GLA_DV = 256
GLA_GATE_RANK = 16
GLA_GATE_NORMALIZER = 16.0
HGRN_HEADS = 8
HGRN_DK = 128
HGRN_DV = 128
CHUNK = 64
MACARON_W = 0.5
N_MOD = 9
EPS = 1e-6

LANES = 128
SUBLANES = 8
MXU_N = 256
VMEM_LIMIT_BYTES = 56 * 1024 * 1024

FFN_TM = 512
FFN_TF = 512
PROLOGUE_ROWS = 256
NORM_ROWS = 16
AUX_ROWS = 128
MIX_TC = 1024
ADALN_TN = 1024
WPREP_ROWS = 256

_NT = (((1,), (1,)), ((), ()))
_TN = (((0,), (0,)), ((), ()))


def _sigmoid(x):
    return jax.nn.sigmoid(x)


def _rms(x):
    return x * lax.rsqrt(jnp.mean(x * x, axis=-1, keepdims=True) + EPS)


def _adaln_kernel(c_ref, w_ref, b_ref, o_ref):
    c = c_ref[...]
    ca = (c * _sigmoid(c)).astype(BF16)
    o_ref[...] = jnp.dot(ca, w_ref[...].astype(BF16), preferred_element_type=F32) + b_ref[...]


def _adaln(c_pad, w, b):
    rows, d = c_pad.shape
    n = w.shape[1]
    assert n % ADALN_TN == 0
    return pl.pallas_call(
        _adaln_kernel,
        grid=(n // ADALN_TN,),
        in_specs=[
            pl.BlockSpec((rows, d), lambda j: (0, 0)),
            pl.BlockSpec((d, ADALN_TN), lambda j: (0, j)),
            pl.BlockSpec((1, ADALN_TN), lambda j: (0, j)),
        ],
        out_specs=pl.BlockSpec((rows, ADALN_TN), lambda j: (0, j)),
        out_shape=jax.ShapeDtypeStruct((rows, n), F32),
        compiler_params=pltpu.CompilerParams(
            dimension_semantics=("arbitrary",), vmem_limit_bytes=VMEM_LIMIT_BYTES),
        name="adaln",
    )(c_pad, w, b)


def _ffn_kernel(*refs, mod_base, epilogue, mix_gate_row):
    x_ref, mod_ref, nw_ref, nw2_ref, wab_ref, wo_ref, *rest = refs
    if mix_gate_row is not None:
        oa_ref, ob_ref, wa_ref, wb_ref, *rest = rest
    if epilogue == "prenorm":
        waux_ref, *rest = rest
        xo_ref, ho_ref, aux_ref, h_scr, *rest = rest
    else:
        xo_ref, h_scr, *rest = rest
    xin_ref = rest[0] if mix_gate_row is not None else x_ref
    f = pl.program_id(1)
    last = pl.num_programs(1) - 1

    def prologue():
        shift = mod_ref[mod_base:mod_base + 1, :]
        gain = nw_ref[...] * (1.0 + mod_ref[mod_base + 1:mod_base + 2, :])
        for r0 in range(0, FFN_TM, PROLOGUE_ROWS):
            rows = slice(r0, r0 + PROLOGUE_ROWS)
            if mix_gate_row is not None:
                y = (jnp.dot(oa_ref[rows, :], wa_ref[...], preferred_element_type=F32)
                     + jnp.dot(ob_ref[rows, :], wb_ref[...], preferred_element_type=F32))
                xin_ref[rows, :] = x_ref[rows, :] + mod_ref[mix_gate_row:mix_gate_row + 1, :] * y
            for c0 in range(r0, r0 + PROLOGUE_ROWS, NORM_ROWS):
                chunk = slice(c0, c0 + NORM_ROWS)
                h_scr[chunk, :] = (_rms(xin_ref[chunk, :]) * gain + shift).astype(BF16)

    def swiglu_step(first):
        zab = jnp.dot(h_scr[...], wab_ref[...], preferred_element_type=F32)
        pieces = FFN_TF // MXU_N
        a = jnp.concatenate([zab[:, (2 * i) * MXU_N:(2 * i + 1) * MXU_N] for i in range(pieces)],
                            axis=1)
        b = jnp.concatenate([zab[:, (2 * i + 1) * MXU_N:(2 * i + 2) * MXU_N]
                             for i in range(pieces)], axis=1)
        act = (a * _sigmoid(a) * b).astype(BF16)
        update = jnp.dot(act, wo_ref[...], preferred_element_type=F32)
        if first:
            xo_ref[...] = update
        else:
            xo_ref[...] += update

    def finish():
        gate = MACARON_W * mod_ref[mod_base + 2:mod_base + 3, :]
        if epilogue == "prenorm":
            shift2 = mod_ref[mod_base + 3:mod_base + 4, :]
            gain2 = nw2_ref[...] * (1.0 + mod_ref[mod_base + 4:mod_base + 5, :])
        for c0 in range(0, FFN_TM, NORM_ROWS):
            chunk = slice(c0, c0 + NORM_ROWS)
            xn = xin_ref[chunk, :] + gate * xo_ref[chunk, :]
            if epilogue == "prenorm":
                xo_ref[chunk, :] = xn
                ho_ref[chunk, :] = (_rms(xn) * gain2 + shift2).astype(BF16)
            elif epilogue == "final":
                xo_ref[chunk, :] = _rms(xn) * nw2_ref[...]
            else:
                xo_ref[chunk, :] = xn
            done = c0 + NORM_ROWS
            if epilogue == "prenorm" and done % AUX_ROWS == 0:
                rows = slice(done - AUX_ROWS, done)
                aux_ref[rows, :] = jnp.dot(ho_ref[rows, :], waux_ref[...],
                                           preferred_element_type=F32).astype(BF16)

    @pl.when(f == 0)
    def _():
        prologue()
        swiglu_step(first=True)

    @pl.when(jnp.logical_and(f > 0, f < last))
    def _():
        swiglu_step(first=False)

    @pl.when(f == last)
    def _():
        swiglu_step(first=False)
        finish()


def _ffn(x2d, mod, nw, nw2, wab, wo, *, tokens_per_batch, mod_base, epilogue, mix=None,
         mix_gate_row=None, w_aux=None):
    m, d = x2d.shape
    n_f = wab.shape[0]
    assert m % FFN_TM == 0 and tokens_per_batch % FFN_TM == 0 and wo.shape[0] == n_f * FFN_TF
    assert (mix is None) == (mix_gate_row is None) and n_f >= 2
    assert (w_aux is not None) == (epilogue == "prenorm")
    tiles_per_batch = tokens_per_batch // FFN_TM
    row_spec = pl.BlockSpec((FFN_TM, d), lambda i, f: (i, 0))
    vec_spec = pl.BlockSpec((1, d), lambda i, f: (0, 0))
    in_specs = [
        row_spec,
        pl.BlockSpec((None, N_MOD, d), lambda i, f: (i // tiles_per_batch, 0, 0)),
        vec_spec,
        vec_spec,
        pl.BlockSpec((None, d, 2 * FFN_TF), lambda i, f: (f, 0, 0)),
        pl.BlockSpec((FFN_TF, d), lambda i, f: (f, 0)),
    ]
    operands = [x2d, mod, nw, nw2, wab, wo]
    scratch = [pltpu.VMEM((FFN_TM, d), BF16)]
    if mix is not None:
        oa, ob, wa, wb = mix
        in_specs += [
            pl.BlockSpec((FFN_TM, oa.shape[1]), lambda i, f: (i, 0)),
            pl.BlockSpec((FFN_TM, ob.shape[1]), lambda i, f: (i, 0)),
            pl.BlockSpec(wa.shape, lambda i, f: (0, 0), pipeline_mode=pl.Buffered(1)),
            pl.BlockSpec(wb.shape, lambda i, f: (0, 0), pipeline_mode=pl.Buffered(1)),
        ]
        operands += [oa, ob, wa, wb]
        scratch.append(pltpu.VMEM((FFN_TM, d), F32))
    out_shape = [jax.ShapeDtypeStruct((m, d), F32)]
    out_specs = [row_spec]
    if epilogue == "prenorm":
        in_specs.append(pl.BlockSpec(w_aux.shape, lambda i, f: (0, 0)))
        operands.append(w_aux)
        out_shape += [jax.ShapeDtypeStruct((m, d), BF16),
                      jax.ShapeDtypeStruct((m, w_aux.shape[1]), BF16)]
        out_specs += [row_spec, pl.BlockSpec((FFN_TM, w_aux.shape[1]), lambda i, f: (i, 0))]
    return pl.pallas_call(
        functools.partial(_ffn_kernel, mod_base=mod_base, epilogue=epilogue,
                          mix_gate_row=mix_gate_row),
        grid=(m // FFN_TM, n_f),
        in_specs=in_specs,
        out_specs=out_specs,
        out_shape=out_shape,
        scratch_shapes=scratch,
        compiler_params=pltpu.CompilerParams(
            dimension_semantics=("arbitrary", "arbitrary"),
            vmem_limit_bytes=VMEM_LIMIT_BYTES),
        name="ffn_" + epilogue,
    )(*operands)


_LEVEL_HALVES = tuple(CHUNK >> (j + 1) for j in range(CHUNK.bit_length() - 1))
_N_LEVELS = len(_LEVEL_HALVES)
_CS_BLOCKS = _N_LEVELS + 2
_GROUP = 4
_GROUP_ROWS = _GROUP * CHUNK


def _chunk_constants():
    t = np.arange(CHUNK)
    tri = (t[None, :] <= t[:, None]).astype(np.float32)
    blocks = [tri]
    masks = []
    for half in _LEVEL_HALVES:
        ref = (t // (2 * half)) * (2 * half) + half
        blocks.append(tri - tri[ref])
        same_block = (t[:, None] // (2 * half)) == (t[None, :] // (2 * half))
        is_query = (t % (2 * half)) >= half
        masks.append((same_block & is_query[:, None] & ~is_query[None, :]).astype(np.float32))
    blocks.append(1.0 - tri)
    masks.append(np.eye(CHUNK, dtype=np.float32))
    cm = np.concatenate(blocks, axis=0)
    cm2 = np.concatenate([cm, cm], axis=1)
    group_masks = np.stack([np.kron(np.eye(_GROUP, dtype=np.float32), m) for m in masks], axis=0)
    return jnp.asarray(cm2, dtype=BF16), jnp.asarray(group_masks, dtype=F32)


def _mixer_kernel(*refs, tiles_per_head, **static):
    *io_refs, z_a, z_b, q_scr, k_scr, cs_scr, oi_scr, u_scr, s_scr = refs
    scratch = (q_scr, k_scr, cs_scr, oi_scr, u_scr, s_scr)
    s = pl.program_id(0)

    @pl.when(s == 0)
    def _():
        z_b[...] = jnp.zeros_like(z_b)

    @pl.when(lax.rem(jnp.maximum(s - 1, 0), tiles_per_head) == 0)
    def _():
        s_scr[...] = jnp.zeros_like(s_scr)

    @pl.when(lax.rem(s, 2) == 0)
    def _():
        _mixer_tile(io_refs, scratch, z_b, z_a, **static)

    @pl.when(lax.rem(s, 2) == 1)
    def _():
        _mixer_tile(io_refs, scratch, z_a, z_b, **static)


def _mixer_tile(io_refs, scratch, z_scr, z_next, *, kind, layer, dk, dv, tc):
    if kind == "gla":
        h_ref, w_ref, gr_ref, w2_ref, b2_ref, nw_ref, cm_ref, mk_ref, o_ref = io_refs
    else:
        h_ref, w_ref, lb_ref, nw_ref, cm_ref, mk_ref, o_ref = io_refs
    q_scr, k_scr, cs_scr, oi_scr, u_scr, s_scr = scratch
    n_chunks = tc // CHUNK
    v_cols = slice(2 * dk, 2 * dk + dv)
    gate_cols = slice(2 * dk + dv, 2 * dk + 2 * dv)

    ncols = w_ref.shape[1]
    pieces = [slice(c0, min(c0 + MXU_N, ncols)) for c0 in range(0, ncols, MXU_N)]
    n_slots = n_chunks // _GROUP + 2

    def project_pieces(slot):
        lo = slot * len(pieces) // n_slots
        hi = (slot + 1) * len(pieces) // n_slots
        for cols in pieces[lo:hi]:
            z_next[:, cols] = jnp.dot(h_ref[...], w_ref[:, cols], preferred_element_type=F32)

    project_pieces(0)

    if kind == "gla":
        q_scr[...] = z_scr[:, 0:dk] * (dk ** -0.5)
        k_scr[...] = z_scr[:, dk:2 * dk]
        gp = jnp.dot(gr_ref[...], w2_ref[...], preferred_element_type=F32) + b2_ref[...]
        la = (jnp.minimum(gp, 0.0) - jnp.log(1.0 + jnp.exp(-jnp.abs(gp)))) * (
            1.0 / GLA_GATE_NORMALIZER)
    else:
        raw = lb_ref[...]
        ex = jnp.exp(raw - jnp.max(raw, axis=0, keepdims=True))
        p = ex / jnp.sum(ex, axis=0, keepdims=True)
        lb = jnp.sum(p[0:layer + 1, :], axis=0, keepdims=True)
        hq = z_scr[:, 0:dk]
        fr = z_scr[:, dk:2 * dk]
        q_scr[...] = hq * _sigmoid(hq)
        en = jnp.exp(-jnp.abs(fr))
        rr = 1.0 / (1.0 + en)
        sig_pos = jnp.where(fr >= 0.0, rr, en * rr)
        sig_neg = jnp.where(fr >= 0.0, en * rr, rr)
        la = jnp.log(lb + (1.0 - lb) * sig_pos)
        k_scr[...] = (1.0 - lb) * sig_neg

    la_wide = jnp.concatenate([la[c * CHUNK:(c + 1) * CHUNK] for c in range(n_chunks)], axis=1)
    la_hi = la_wide.astype(BF16)
    la_lo = (la_wide - la_hi.astype(F32)).astype(BF16)
    cs_scr[...] = jnp.dot(cm_ref[...], jnp.concatenate([la_hi, la_lo], axis=0),
                          preferred_element_type=F32)

    def cs_block(block, c):
        return cs_scr[block * CHUNK:(block + 1) * CHUNK, c * dk:(c + 1) * dk]

    for g in range(n_chunks // _GROUP):
        project_pieces(g + 1)
        chunks = range(g * _GROUP, (g + 1) * _GROUP)
        rows = slice(g * _GROUP_ROWS, (g + 1) * _GROUP_ROWS)
        q = q_scr[rows, :]
        k = k_scr[rows, :]
        attn = mk_ref[_N_LEVELS] * jnp.sum(q * k, axis=-1, keepdims=True)
        for j in range(_N_LEVELS):
            d = jnp.concatenate([cs_block(j + 1, c) for c in chunks], axis=0)
            e = jnp.exp(-jnp.abs(d))
            s = lax.dot_general((q * e).astype(BF16), (k * e).astype(BF16), _NT,
                                preferred_element_type=F32)
            attn = attn + mk_ref[j] * s
        vb = z_scr[rows, v_cols].astype(BF16)
        oi_scr[rows, :] = jnp.dot(attn.astype(BF16), vb, preferred_element_type=F32)
        for i, c in enumerate(chunks):
            crow = slice(c * CHUNK, (c + 1) * CHUNK)
            kd = (k_scr[crow, :] * jnp.exp(cs_block(_N_LEVELS + 1, c))).astype(BF16)
            u_scr[c] = lax.dot_general(vb[i * CHUNK:(i + 1) * CHUNK], kd, _TN,
                                       preferred_element_type=F32)

    project_pieces(n_slots - 1)
    st = s_scr[...]
    for c in range(n_chunks):
        rows = slice(c * CHUNK, (c + 1) * CHUNK)
        b = cs_block(0, c)
        qb = (q_scr[rows, :] * jnp.exp(b)).astype(BF16)
        o = lax.dot_general(qb, st.astype(BF16), _NT, preferred_element_type=F32) + oi_scr[rows, :]
        st = st * jnp.exp(b[CHUNK - 1:CHUNK, :]) + u_scr[c]
        gate = z_scr[rows, gate_cols]
        o_ref[rows, :] = (_rms(o) * nw_ref[...] * (gate * _sigmoid(gate))).astype(BF16)
    s_scr[...] = st


def _mixer(h2, w_heads, extra, nw, consts, *, kind, layer, batch, tokens_per_batch, dk, dv):
    m, d = h2.shape
    n_heads, _, ncols = w_heads.shape
    tc = MIX_TC
    assert tokens_per_batch % tc == 0 and tc % _GROUP_ROWS == 0
    nt = tokens_per_batch // tc
    n_tiles = batch * n_heads * nt
    cm, mk = consts

    def coords(tile):
        bb = tile // (n_heads * nt)
        hh = lax.rem(tile // nt, n_heads)
        return bb * nt + lax.rem(tile, nt), hh

    def projected(s):
        return coords(jnp.minimum(s, n_tiles - 1))

    def consumed(s):
        return coords(jnp.maximum(s - 1, 0))

    const2 = lambda s: (0, 0)
    in_specs = [
        pl.BlockSpec((tc, d), lambda s: (projected(s)[0], 0)),
        pl.BlockSpec((None, d, ncols), lambda s: (projected(s)[1], 0, 0)),
    ]
    if kind == "gla":
        gr, w2, b2 = extra
        in_specs += [
            pl.BlockSpec((tc, LANES), lambda s: (consumed(s)[0], 0)),
            pl.BlockSpec((None, LANES, dk), lambda s: (consumed(s)[1], 0, 0)),
            pl.BlockSpec((None, 1, dk), lambda s: (consumed(s)[1], 0, 0)),
        ]
    else:
        (lbraw,) = extra
        in_specs += [pl.BlockSpec((None, lbraw.shape[1], dk), lambda s: (consumed(s)[1], 0, 0))]
    in_specs += [
        pl.BlockSpec((1, dv), const2),
        pl.BlockSpec(cm.shape, const2),
        pl.BlockSpec(mk.shape, lambda s: (0, 0, 0)),
    ]
    return pl.pallas_call(
        functools.partial(_mixer_kernel, tiles_per_head=nt, kind=kind, layer=layer, dk=dk, dv=dv,
                          tc=tc),
        grid=(n_tiles + 1,),
        in_specs=in_specs,
        out_specs=pl.BlockSpec((tc, dv), lambda s: consumed(s)),
        out_shape=jax.ShapeDtypeStruct((m, n_heads * dv), BF16),
        scratch_shapes=[
            pltpu.VMEM((tc, ncols), F32),
            pltpu.VMEM((tc, ncols), F32),
            pltpu.VMEM((tc, dk), F32),
            pltpu.VMEM((tc, dk), F32),
            pltpu.VMEM((_CS_BLOCKS * CHUNK, (tc // CHUNK) * dk), F32),
            pltpu.VMEM((tc, dv), F32),
            pltpu.VMEM((tc // CHUNK, dv, dk), F32),
            pltpu.VMEM((dv, dk), F32),
        ],
        compiler_params=pltpu.CompilerParams(
            dimension_semantics=("arbitrary",), vmem_limit_bytes=VMEM_LIMIT_BYTES),
        name="mixer_" + kind,
    )(h2, w_heads, *extra, nw, cm, mk)


def _wi_prep_kernel(w_ref, o_ref, *, d_ff, n_f):
    rows = w_ref.shape[0]
    for j in range(n_f):
        for i in range(FFN_TF // MXU_N):
            c0 = j * FFN_TF + i * MXU_N
            valid = max(0, min(MXU_N, d_ff - c0))
            for half in range(2):
                src = half * d_ff + c0
                parts = []
                if valid:
                    parts.append(w_ref[:, src:src + valid].astype(BF16))
                if valid < MXU_N:
                    parts.append(jnp.zeros((rows, MXU_N - valid), BF16))
                piece = parts[0] if len(parts) == 1 else jnp.concatenate(parts, axis=1)
                o_ref[j, :, (2 * i + half) * MXU_N:(2 * i + half + 1) * MXU_N] = piece


def _wo_prep_kernel(w_ref, *rest, n_full):
    *tail_refs, o_ref = rest
    r = pl.program_id(0)

    @pl.when(r < n_full)
    def _():
        o_ref[...] = w_ref[...].astype(BF16)

    @pl.when(r >= n_full)
    def _():
        rows = 0
        for t_ref in tail_refs:
            o_ref[rows:rows + LANES, :] = t_ref[...].astype(BF16)
            rows += LANES
        o_ref[rows:, :] = jnp.zeros((FFN_TF - rows, o_ref.shape[1]), BF16)


def _prep_ffn_weights(wi, wo):
    d, d_ff = wi.shape[0], wo.shape[0]
    assert d_ff % LANES == 0 and d % WPREP_ROWS == 0
    n_f = -(-d_ff // FFN_TF)
    wab = pl.pallas_call(
        functools.partial(_wi_prep_kernel, d_ff=d_ff, n_f=n_f),
        grid=(d // WPREP_ROWS,),
        in_specs=[pl.BlockSpec((WPREP_ROWS, 2 * d_ff), lambda r: (r, 0))],
        out_specs=pl.BlockSpec((n_f, WPREP_ROWS, 2 * FFN_TF), lambda r: (0, r, 0)),
        out_shape=jax.ShapeDtypeStruct((n_f, d, 2 * FFN_TF), BF16),
        compiler_params=pltpu.CompilerParams(
            dimension_semantics=("arbitrary",), vmem_limit_bytes=VMEM_LIMIT_BYTES),
        name="wi_prep",
    )(wi)
    n_full = d_ff // FFN_TF
    n_tail = (d_ff - n_full * FFN_TF) // LANES
    tail0 = n_full * FFN_TF // LANES
    tail_specs = [pl.BlockSpec((LANES, d), functools.partial(lambda r, k: (tail0 + k, 0), k=k))
                  for k in range(n_tail)]
    wob = pl.pallas_call(
        functools.partial(_wo_prep_kernel, n_full=n_full),
        grid=(n_f,),
        in_specs=[pl.BlockSpec((FFN_TF, d), lambda r: (jnp.minimum(r, n_full - 1), 0))]
        + tail_specs,
        out_specs=pl.BlockSpec((FFN_TF, d), lambda r: (r, 0)),
        out_shape=jax.ShapeDtypeStruct((n_f * FFN_TF, d), BF16),
        compiler_params=pltpu.CompilerParams(
            dimension_semantics=("arbitrary",), vmem_limit_bytes=VMEM_LIMIT_BYTES),
        name="wo_prep",
    )(wo, *([wo] * n_tail))
    return wab, wob


_GLA_QK = GLA_HEADS * GLA_DK
_GLA_V = GLA_HEADS * GLA_DV
_HGRN_K = HGRN_HEADS * HGRN_DK
_HGRN_V = HGRN_HEADS * HGRN_DV
_IN_OFFS = tuple(int(o) for o in np.cumsum(
    [0, _GLA_QK, _GLA_QK, _GLA_V, _GLA_V, GLA_GATE_RANK, _HGRN_K, _HGRN_K, _HGRN_V, _HGRN_V]))


def _win_prep_kernel(wt_ref, og_ref, oh_ref, ogr_ref):
    def cols(part, h, width):
        f0 = _IN_OFFS[part] + h * width
        return wt_ref[f0:f0 + width, :].T.astype(BF16)

    gr = wt_ref[_IN_OFFS[4]:_IN_OFFS[5], :]
    gr = jnp.concatenate([gr, jnp.zeros((LANES - GLA_GATE_RANK, gr.shape[1]), F32)], axis=0)
    ogr_ref[...] = gr.T.astype(BF16)
    for h in range(GLA_HEADS):
        og_ref[h] = jnp.concatenate(
            [cols(0, h, GLA_DK), cols(1, h, GLA_DK), cols(2, h, GLA_DV), cols(3, h, GLA_DV)],
            axis=1)
    for h in range(HGRN_HEADS):
        oh_ref[h] = jnp.concatenate(
            [cols(5, h, HGRN_DK), cols(6, h, HGRN_DK), cols(7, h, HGRN_DV), cols(8, h, HGRN_DV)],
            axis=1)


def _prep_mixer_weights(w_in, layer, w2, b2):
    _, d, in_width = w_in.shape
    assert in_width == _IN_OFFS[-1] and d % WPREP_ROWS == 0
    gla_cols = 2 * GLA_DK + 2 * GLA_DV
    hgrn_cols = 2 * HGRN_DK + 2 * HGRN_DV
    wt = jnp.swapaxes(w_in, 1, 2)
    w_gla, w_hg, w_gr = pl.pallas_call(
        _win_prep_kernel,
        grid=(d // WPREP_ROWS,),
        in_specs=[pl.BlockSpec((None, in_width, WPREP_ROWS), lambda r: (layer, 0, r))],
        out_specs=[pl.BlockSpec((GLA_HEADS, WPREP_ROWS, gla_cols), lambda r: (0, r, 0)),
                   pl.BlockSpec((HGRN_HEADS, WPREP_ROWS, hgrn_cols), lambda r: (0, r, 0)),
                   pl.BlockSpec((WPREP_ROWS, LANES), lambda r: (r, 0))],
        out_shape=[jax.ShapeDtypeStruct((GLA_HEADS, d, gla_cols), BF16),
                   jax.ShapeDtypeStruct((HGRN_HEADS, d, hgrn_cols), BF16),
                   jax.ShapeDtypeStruct((d, LANES), BF16)],
        compiler_params=pltpu.CompilerParams(
            dimension_semantics=("arbitrary",), vmem_limit_bytes=VMEM_LIMIT_BYTES),
        name="win_prep",
    )(wt)
    w2h = jnp.pad(w2.astype(BF16), ((0, LANES - GLA_GATE_RANK), (0, 0)))
    w2h = w2h.reshape(LANES, GLA_HEADS, GLA_DK).transpose(1, 0, 2)
    b2h = b2.reshape(GLA_HEADS, 1, GLA_DK)
    return w_gla, w_hg, w_gr, w2h, b2h


def kernel(x, c, ada_w, ada_b, norm_ffn1_w, ffn1_wi, ffn1_wo, norm_mix_w, w_in, gla_gate_w2,
           gla_gate_b2, gla_norm_w, hgrn_norm_w, hgrn_lower_bounds, w_out, norm_ffn2_w, ffn2_wi,
           ffn2_wo, final_norm_w):
    batch, seq, d = x.shape
    depth = ada_w.shape[0]
    m = batch * seq
    consts = _chunk_constants()
    xc = x.reshape(m, d)
    c_pad = jnp.pad(c, ((0, SUBLANES - batch % SUBLANES), (0, 0))) if batch % SUBLANES else c
    gla_v = GLA_HEADS * GLA_DV
    lb_heads = hgrn_lower_bounds.astype(F32).reshape(depth + 1, HGRN_HEADS, HGRN_DK).transpose(1, 0, 2)

    for l in range(depth):
        mod = _adaln(c_pad, ada_w[l], ada_b[l][None, :])[:batch].reshape(batch, N_MOD, d)
        wab1, wo1 = _prep_ffn_weights(ffn1_wi[l], ffn1_wo[l])
        wab2, wo2 = _prep_ffn_weights(ffn2_wi[l], ffn2_wo[l])
        w_gla, w_hg, w_gr, w2h, b2h = _prep_mixer_weights(w_in, l, gla_gate_w2[l], gla_gate_b2[l])
        wout = w_out[l].astype(BF16)

        x1, h2, gr = _ffn(xc, mod, norm_ffn1_w[l][None, :], norm_mix_w[l][None, :], wab1, wo1,
                          tokens_per_batch=seq, mod_base=0, epilogue="prenorm", w_aux=w_gr)
        o_gla = _mixer(h2, w_gla, (gr, w2h, b2h), gla_norm_w[l][None, :], consts, kind="gla",
                       layer=l, batch=batch, tokens_per_batch=seq, dk=GLA_DK, dv=GLA_DV)
        o_hg = _mixer(h2, w_hg, (lb_heads,), hgrn_norm_w[l][None, :], consts, kind="hgrn",
                      layer=l, batch=batch, tokens_per_batch=seq, dk=HGRN_DK, dv=HGRN_DV)
        last = l == depth - 1
        nw2 = final_norm_w[None, :] if last else norm_ffn2_w[l][None, :]
        (xc,) = _ffn(x1, mod, norm_ffn2_w[l][None, :], nw2, wab2, wo2,
                     tokens_per_batch=seq, mod_base=6, epilogue="final" if last else "none",
                     mix=(o_gla, o_hg, wout[:gla_v], wout[gla_v:]), mix_gate_row=5)
    return xc.reshape(batch, seq, d)
```

```python
import functools

import jax
import jax.numpy as jnp
import numpy as np
from jax import lax
from jax.experimental import pallas as pl
from jax.experimental.pallas import tpu as pltpu

F32 = jnp.float32
BF16 = jnp.bfloat16

GLA_HEADS = 4
GLA_DK = 128
GLA_DV = 256
GLA_GATE_RANK = 16
GLA_GATE_NORMALIZER = 16.0
HGRN_HEADS = 8
HGRN_DK = 128
HGRN_DV = 128
CHUNK = 64
MACARON_W = 0.5
N_MOD = 9
EPS = 1e-6

LANES = 128
SUBLANES = 8
MXU_N = 256
VMEM_LIMIT_BYTES = 56 * 1024 * 1024

FFN_TM = 512
FFN_TF = 512
PROLOGUE_ROWS = 256
NORM_ROWS = 16
AUX_ROWS = 128
MIX_TC = 1024
ADALN_TN = 1024
WPREP_ROWS = 256

_NT = (((1,), (1,)), ((), ()))
_TN = (((0,), (0,)), ((), ()))


def _sigmoid(x):
    return jax.nn.sigmoid(x)


def _rms(x):
    return x * lax.rsqrt(jnp.mean(x * x, axis=-1, keepdims=True) + EPS)


def _adaln_kernel(c_ref, w_ref, b_ref, o_ref):
    c = c_ref[...]
    ca = (c * _sigmoid(c)).astype(BF16)
    o_ref[...] = jnp.dot(ca, w_ref[...].astype(BF16), preferred_element_type=F32) + b_ref[...]


def _adaln(c_pad, w, b):
    rows, d = c_pad.shape
    n = w.shape[1]
    assert n % ADALN_TN == 0
    return pl.pallas_call(
        _adaln_kernel,
        grid=(n // ADALN_TN,),
        in_specs=[
            pl.BlockSpec((rows, d), lambda j: (0, 0)),
            pl.BlockSpec((d, ADALN_TN), lambda j: (0, j)),
            pl.BlockSpec((1, ADALN_TN), lambda j: (0, j)),
        ],
        out_specs=pl.BlockSpec((rows, ADALN_TN), lambda j: (0, j)),
        out_shape=jax.ShapeDtypeStruct((rows, n), F32),
        compiler_params=pltpu.CompilerParams(
            dimension_semantics=("arbitrary",), vmem_limit_bytes=VMEM_LIMIT_BYTES),
        name="adaln",
    )(c_pad, w, b)


def _ffn_kernel(*refs, mod_base, epilogue, mix_gate_row):
    x_ref, mod_ref, nw_ref, nw2_ref, wab_ref, wo_ref, *rest = refs
    if mix_gate_row is not None:
        oa_ref, ob_ref, wa_ref, wb_ref, *rest = rest
    if epilogue == "prenorm":
        waux_ref, *rest = rest
        xo_ref, ho_ref, aux_ref, h_scr, *rest = rest
    else:
        xo_ref, h_scr, *rest = rest
    xin_ref = rest[0] if mix_gate_row is not None else x_ref
    f = pl.program_id(1)
    last = pl.num_programs(1) - 1
    tm = x_ref.shape[0]

    def prologue():
        shift = mod_ref[mod_base:mod_base + 1, :]
        gain = nw_ref[...] * (1.0 + mod_ref[mod_base + 1:mod_base + 2, :])
        for r0 in range(0, tm, PROLOGUE_ROWS):
            rows = slice(r0, r0 + PROLOGUE_ROWS)
            if mix_gate_row is not None:
                y = (jnp.dot(oa_ref[rows, :], wa_ref[...], preferred_element_type=F32)
                     + jnp.dot(ob_ref[rows, :], wb_ref[...], preferred_element_type=F32))
                xin_ref[rows, :] = x_ref[rows, :] + mod_ref[mix_gate_row:mix_gate_row + 1, :] * y
            for c0 in range(r0, r0 + PROLOGUE_ROWS, NORM_ROWS):
                chunk = slice(c0, c0 + NORM_ROWS)
                h_scr[chunk, :] = (_rms(xin_ref[chunk, :]) * gain + shift).astype(BF16)

    def swiglu_step(first):
        zab = jnp.dot(h_scr[...], wab_ref[...], preferred_element_type=F32)
        pieces = FFN_TF // MXU_N
        a = jnp.concatenate([zab[:, (2 * i) * MXU_N:(2 * i + 1) * MXU_N] for i in range(pieces)],
                            axis=1)
        b = jnp.concatenate([zab[:, (2 * i + 1) * MXU_N:(2 * i + 2) * MXU_N]
                             for i in range(pieces)], axis=1)
        act = (a * _sigmoid(a) * b).astype(BF16)
        update = jnp.dot(act, wo_ref[...], preferred_element_type=F32)
        if first:
            xo_ref[...] = update
        else:
            xo_ref[...] += update

    def finish():
        gate = MACARON_W * mod_ref[mod_base + 2:mod_base + 3, :]
        if epilogue == "prenorm":
            shift2 = mod_ref[mod_base + 3:mod_base + 4, :]
            gain2 = nw2_ref[...] * (1.0 + mod_ref[mod_base + 4:mod_base + 5, :])
        for c0 in range(0, tm, NORM_ROWS):
            chunk = slice(c0, c0 + NORM_ROWS)
            xn = xin_ref[chunk, :] + gate * xo_ref[chunk, :]
            if epilogue == "prenorm":
                xo_ref[chunk, :] = xn
                ho_ref[chunk, :] = (_rms(xn) * gain2 + shift2).astype(BF16)
            elif epilogue == "final":
                xo_ref[chunk, :] = _rms(xn) * nw2_ref[...]
            else:
                xo_ref[chunk, :] = xn
            done = c0 + NORM_ROWS
            if epilogue == "prenorm" and done % AUX_ROWS == 0:
                rows = slice(done - AUX_ROWS, done)
                aux_ref[rows, :] = jnp.dot(ho_ref[rows, :], waux_ref[...],
                                           preferred_element_type=F32).astype(BF16)

    @pl.when(f == 0)
    def _():
        prologue()
        swiglu_step(first=True)

    @pl.when(jnp.logical_and(f > 0, f < last))
    def _():
        swiglu_step(first=False)

    @pl.when(f == last)
    def _():
        swiglu_step(first=False)
        finish()


def _ffn(x2d, mod, nw, nw2, wab, wo, *, tokens_per_batch, mod_base, epilogue, mix=None,
         mix_gate_row=None, w_aux=None):
    m, d = x2d.shape
    n_f = wab.shape[0]
    tm = FFN_TM
    assert m % tm == 0 and tokens_per_batch % tm == 0 and wo.shape[0] == n_f * FFN_TF
    assert (mix is None) == (mix_gate_row is None) and n_f >= 2
    assert (w_aux is not None) == (epilogue == "prenorm")
    tiles_per_batch = tokens_per_batch // tm
    row_spec = pl.BlockSpec((tm, d), lambda i, f: (i, 0))
    vec_spec = pl.BlockSpec((1, d), lambda i, f: (0, 0))
    in_specs = [
        row_spec,
        pl.BlockSpec((None, N_MOD, d), lambda i, f: (i // tiles_per_batch, 0, 0)),
        vec_spec,
        vec_spec,
        pl.BlockSpec((None, d, 2 * FFN_TF), lambda i, f: (f, 0, 0)),
        pl.BlockSpec((FFN_TF, d), lambda i, f: (f, 0)),
    ]
    operands = [x2d, mod, nw, nw2, wab, wo]
    scratch = [pltpu.VMEM((tm, d), BF16)]
    if mix is not None:
        oa, ob, wa, wb = mix
        in_specs += [
            pl.BlockSpec((tm, oa.shape[1]), lambda i, f: (i, 0)),
            pl.BlockSpec((tm, ob.shape[1]), lambda i, f: (i, 0)),
            pl.BlockSpec(wa.shape, lambda i, f: (0, 0), pipeline_mode=pl.Buffered(1)),
            pl.BlockSpec(wb.shape, lambda i, f: (0, 0), pipeline_mode=pl.Buffered(1)),
        ]
        operands += [oa, ob, wa, wb]
        scratch.append(pltpu.VMEM((tm, d), F32))
    out_shape = [jax.ShapeDtypeStruct((m, d), F32)]
    out_specs = [row_spec]
    if epilogue == "prenorm":
        in_specs.append(pl.BlockSpec(w_aux.shape, lambda i, f: (0, 0)))
        operands.append(w_aux)
        out_shape += [jax.ShapeDtypeStruct((m, d), BF16),
                      jax.ShapeDtypeStruct((m, w_aux.shape[1]), BF16)]
        out_specs += [row_spec, pl.BlockSpec((tm, w_aux.shape[1]), lambda i, f: (i, 0))]
    return pl.pallas_call(
        functools.partial(_ffn_kernel, mod_base=mod_base, epilogue=epilogue,
                          mix_gate_row=mix_gate_row),
        grid=(m // tm, n_f),
        in_specs=in_specs,
        out_specs=out_specs,
        out_shape=out_shape,
        scratch_shapes=scratch,
        compiler_params=pltpu.CompilerParams(
            dimension_semantics=("arbitrary", "arbitrary"),
            vmem_limit_bytes=VMEM_LIMIT_BYTES),
        name="ffn_" + epilogue,
    )(*operands)


_LEVEL_HALVES = tuple(CHUNK >> (j + 1) for j in range(CHUNK.bit_length() - 1))
_N_LEVELS = len(_LEVEL_HALVES)
_CS_BLOCKS = _N_LEVELS + 2
_GROUP = 4
_GROUP_ROWS = _GROUP * CHUNK


def _chunk_constants():
    t = np.arange(CHUNK)
    tri = (t[None, :] <= t[:, None]).astype(np.float32)
    blocks = [tri]
    masks = []
    for half in _LEVEL_HALVES:
        ref = (t // (2 * half)) * (2 * half) + half
        blocks.append(tri - tri[ref])
        same_block = (t[:, None] // (2 * half)) == (t[None, :] // (2 * half))
        is_query = (t % (2 * half)) >= half
        masks.append((same_block & is_query[:, None] & ~is_query[None, :]).astype(np.float32))
    blocks.append(1.0 - tri)
    masks.append(np.eye(CHUNK, dtype=np.float32))
    cm = np.concatenate(blocks, axis=0)
    cm2 = np.concatenate([cm, cm], axis=1)
    group_masks = np.stack([np.kron(np.eye(_GROUP, dtype=np.float32), m) for m in masks], axis=0)
    return jnp.asarray(cm2, dtype=BF16), jnp.asarray(group_masks, dtype=F32)


def _mixer_kernel(*refs, tiles_per_head, **static):
    *io_refs, z_a, z_b, q_scr, k_scr, cs_scr, oi_scr, u_scr, s_scr = refs
    scratch = (q_scr, k_scr, cs_scr, oi_scr, u_scr, s_scr)
    s = pl.program_id(0)

    @pl.when(s == 0)
    def _():
        z_b[...] = jnp.zeros_like(z_b)

    @pl.when(lax.rem(jnp.maximum(s - 1, 0), tiles_per_head) == 0)
    def _():
        s_scr[...] = jnp.zeros_like(s_scr)

    @pl.when(lax.rem(s, 2) == 0)
    def _():
        _mixer_tile(io_refs, scratch, z_b, z_a, **static)

    @pl.when(lax.rem(s, 2) == 1)
    def _():
        _mixer_tile(io_refs, scratch, z_a, z_b, **static)


def _mixer_tile(io_refs, scratch, z_scr, z_next, *, kind, layer, dk, dv, tc):
    if kind == "gla":
        h_ref, w_ref, gr_ref, w2_ref, b2_ref, nw_ref, cm_ref, mk_ref, o_ref = io_refs
    else:
        h_ref, w_ref, lb_ref, nw_ref, cm_ref, mk_ref, o_ref = io_refs
    q_scr, k_scr, cs_scr, oi_scr, u_scr, s_scr = scratch
    n_chunks = tc // CHUNK
    v_cols = slice(2 * dk, 2 * dk + dv)
    gate_cols = slice(2 * dk + dv, 2 * dk + 2 * dv)

    ncols = w_ref.shape[1]
    pieces = [slice(c0, min(c0 + MXU_N, ncols)) for c0 in range(0, ncols, MXU_N)]
    n_slots = n_chunks // _GROUP + 2

    def project_pieces(slot):
        lo = slot * len(pieces) // n_slots
        hi = (slot + 1) * len(pieces) // n_slots
        for cols in pieces[lo:hi]:
            z_next[:, cols] = jnp.dot(h_ref[...], w_ref[:, cols], preferred_element_type=F32)

    project_pieces(0)

    if kind == "gla":
        q_scr[...] = z_scr[:, 0:dk] * (dk ** -0.5)
        k_scr[...] = z_scr[:, dk:2 * dk]
        gp = jnp.dot(gr_ref[...], w2_ref[...], preferred_element_type=F32) + b2_ref[...]
        la = (jnp.minimum(gp, 0.0) - jnp.log(1.0 + jnp.exp(-jnp.abs(gp)))) * (
            1.0 / GLA_GATE_NORMALIZER)
    else:
        raw = lb_ref[...]
        ex = jnp.exp(raw - jnp.max(raw, axis=0, keepdims=True))
        p = ex / jnp.sum(ex, axis=0, keepdims=True)
        lb = jnp.sum(p[0:layer + 1, :], axis=0, keepdims=True)
        hq = z_scr[:, 0:dk]
        fr = z_scr[:, dk:2 * dk]
        q_scr[...] = hq * _sigmoid(hq)
        en = jnp.exp(-jnp.abs(fr))
        one_en = 1.0 + en
        log_sig = jnp.minimum(fr, 0.0) - jnp.log(one_en)
        sig_neg = jnp.where(fr >= 0.0, en, 1.0) / one_en
        la_a = jnp.log(lb)
        la_b = jnp.log(1.0 - lb) + log_sig
        la = jnp.maximum(la_a, la_b) + jnp.log(1.0 + jnp.exp(-jnp.abs(la_a - la_b)))
        k_scr[...] = (1.0 - lb) * sig_neg

    la_wide = jnp.concatenate([la[c * CHUNK:(c + 1) * CHUNK] for c in range(n_chunks)], axis=1)
    la_hi = la_wide.astype(BF16)
    la_lo = (la_wide - la_hi.astype(F32)).astype(BF16)
    cs_scr[...] = jnp.dot(cm_ref[...], jnp.concatenate([la_hi, la_lo], axis=0),
                          preferred_element_type=F32)

    def cs_block(block, c):
        return cs_scr[block * CHUNK:(block + 1) * CHUNK, c * dk:(c + 1) * dk]

    for g in range(n_chunks // _GROUP):
        project_pieces(g + 1)
        chunks = range(g * _GROUP, (g + 1) * _GROUP)
        rows = slice(g * _GROUP_ROWS, (g + 1) * _GROUP_ROWS)
        q = q_scr[rows, :]
        k = k_scr[rows, :]
        attn = mk_ref[_N_LEVELS] * jnp.sum(q * k, axis=-1, keepdims=True)
        for j in range(_N_LEVELS):
            d = jnp.concatenate([cs_block(j + 1, c) for c in chunks], axis=0)
            e = jnp.exp(-jnp.abs(d))
            s = lax.dot_general((q * e).astype(BF16), (k * e).astype(BF16), _NT,
                                preferred_element_type=F32)
            attn = attn + mk_ref[j] * s
        vb = z_scr[rows, v_cols].astype(BF16)
        oi_scr[rows, :] = jnp.dot(attn.astype(BF16), vb, preferred_element_type=F32)
        for i, c in enumerate(chunks):
            crow = slice(c * CHUNK, (c + 1) * CHUNK)
            kd = (k_scr[crow, :] * jnp.exp(cs_block(_N_LEVELS + 1, c))).astype(BF16)
            u_scr[c] = lax.dot_general(vb[i * CHUNK:(i + 1) * CHUNK], kd, _TN,
                                       preferred_element_type=F32)

    project_pieces(n_slots - 1)
    st = s_scr[...]
    for c in range(n_chunks):
        rows = slice(c * CHUNK, (c + 1) * CHUNK)
        b = cs_block(0, c)
        qb = (q_scr[rows, :] * jnp.exp(b)).astype(BF16)
        o = lax.dot_general(qb, st.astype(BF16), _NT, preferred_element_type=F32) + oi_scr[rows, :]
        st = st * jnp.exp(b[CHUNK - 1:CHUNK, :]) + u_scr[c]
        gate = z_scr[rows, gate_cols]
        o_ref[rows, :] = (_rms(o) * nw_ref[...] * (gate * _sigmoid(gate))).astype(BF16)
    s_scr[...] = st


def _mixer(h2, w_heads, extra, nw, consts, *, kind, layer, batch, tokens_per_batch, dk, dv):
    m, d = h2.shape
    n_heads, _, ncols = w_heads.shape
    tc = MIX_TC
    assert tokens_per_batch % tc == 0 and tc % _GROUP_ROWS == 0
    nt = tokens_per_batch // tc
    n_tiles = batch * n_heads * nt
    cm, mk = consts

    def coords(tile):
        bb = tile // (n_heads * nt)
        hh = lax.rem(tile // nt, n_heads)
        return bb * nt + lax.rem(tile, nt), hh

    def projected(s):
        return coords(jnp.minimum(s, n_tiles - 1))

    def consumed(s):
        return coords(jnp.maximum(s - 1, 0))

    const2 = lambda s: (0, 0)
    in_specs = [
        pl.BlockSpec((tc, d), lambda s: (projected(s)[0], 0)),
        pl.BlockSpec((None, d, ncols), lambda s: (projected(s)[1], 0, 0)),
    ]
    if kind == "gla":
        gr, w2, b2 = extra
        in_specs += [
            pl.BlockSpec((tc, LANES), lambda s: (consumed(s)[0], 0)),
            pl.BlockSpec((None, LANES, dk), lambda s: (consumed(s)[1], 0, 0)),
            pl.BlockSpec((None, 1, dk), lambda s: (consumed(s)[1], 0, 0)),
        ]
    else:
        (lbraw,) = extra
        in_specs += [pl.BlockSpec((None, lbraw.shape[1], dk), lambda s: (consumed(s)[1], 0, 0))]
    in_specs += [
        pl.BlockSpec((1, dv), const2),
        pl.BlockSpec(cm.shape, const2),
        pl.BlockSpec(mk.shape, lambda s: (0, 0, 0)),
    ]
    return pl.pallas_call(
        functools.partial(_mixer_kernel, tiles_per_head=nt, kind=kind, layer=layer, dk=dk, dv=dv,
                          tc=tc),
        grid=(n_tiles + 1,),
        in_specs=in_specs,
        out_specs=pl.BlockSpec((tc, dv), lambda s: consumed(s)),
        out_shape=jax.ShapeDtypeStruct((m, n_heads * dv), BF16),
        scratch_shapes=[
            pltpu.VMEM((tc, ncols), F32),
            pltpu.VMEM((tc, ncols), F32),
            pltpu.VMEM((tc, dk), F32),
            pltpu.VMEM((tc, dk), F32),
            pltpu.VMEM((_CS_BLOCKS * CHUNK, (tc // CHUNK) * dk), F32),
            pltpu.VMEM((tc, dv), F32),
            pltpu.VMEM((tc // CHUNK, dv, dk), F32),
            pltpu.VMEM((dv, dk), F32),
        ],
        compiler_params=pltpu.CompilerParams(
            dimension_semantics=("arbitrary",), vmem_limit_bytes=VMEM_LIMIT_BYTES),
        name="mixer_" + kind,
    )(h2, w_heads, *extra, nw, cm, mk)


def _wi_prep_kernel(w_ref, o_ref, *, d_ff, n_f):
    rows = w_ref.shape[0]
    for j in range(n_f):
        for i in range(FFN_TF // MXU_N):
            c0 = j * FFN_TF + i * MXU_N
            valid = max(0, min(MXU_N, d_ff - c0))
            for half in range(2):
                src = half * d_ff + c0
                parts = []
                if valid:
                    parts.append(w_ref[:, src:src + valid].astype(BF16))
                if valid < MXU_N:
                    parts.append(jnp.zeros((rows, MXU_N - valid), BF16))
                piece = parts[0] if len(parts) == 1 else jnp.concatenate(parts, axis=1)
                o_ref[j, :, (2 * i + half) * MXU_N:(2 * i + half + 1) * MXU_N] = piece


def _wo_prep_kernel(w_ref, *rest, n_full):
    *tail_refs, o_ref = rest
    r = pl.program_id(0)

    @pl.when(r < n_full)
    def _():
        o_ref[...] = w_ref[...].astype(BF16)

    @pl.when(r >= n_full)
    def _():
        rows = 0
        for t_ref in tail_refs:
            o_ref[rows:rows + LANES, :] = t_ref[...].astype(BF16)
            rows += LANES
        o_ref[rows:, :] = jnp.zeros((FFN_TF - rows, o_ref.shape[1]), BF16)


def _prep_ffn_weights(wi, wo):
    d, d_ff = wi.shape[0], wo.shape[0]
    assert d_ff % LANES == 0 and d % WPREP_ROWS == 0
    n_f = -(-d_ff // FFN_TF)
    wab = pl.pallas_call(
        functools.partial(_wi_prep_kernel, d_ff=d_ff, n_f=n_f),
        grid=(d // WPREP_ROWS,),
        in_specs=[pl.BlockSpec((WPREP_ROWS, 2 * d_ff), lambda r: (r, 0))],
        out_specs=pl.BlockSpec((n_f, WPREP_ROWS, 2 * FFN_TF), lambda r: (0, r, 0)),
        out_shape=jax.ShapeDtypeStruct((n_f, d, 2 * FFN_TF), BF16),
        compiler_params=pltpu.CompilerParams(
            dimension_semantics=("arbitrary",), vmem_limit_bytes=VMEM_LIMIT_BYTES),
        name="wi_prep",
    )(wi)
    n_full = d_ff // FFN_TF
    n_tail = (d_ff - n_full * FFN_TF) // LANES
    tail0 = n_full * FFN_TF // LANES
    tail_specs = [pl.BlockSpec((LANES, d), functools.partial(lambda r, k: (tail0 + k, 0), k=k))
                  for k in range(n_tail)]
    wob = pl.pallas_call(
        functools.partial(_wo_prep_kernel, n_full=n_full),
        grid=(n_f,),
        in_specs=[pl.BlockSpec((FFN_TF, d), lambda r: (jnp.minimum(r, n_full - 1), 0))]
        + tail_specs,
        out_specs=pl.BlockSpec((FFN_TF, d), lambda r: (r, 0)),
        out_shape=jax.ShapeDtypeStruct((n_f * FFN_TF, d), BF16),
        compiler_params=pltpu.CompilerParams(
            dimension_semantics=("arbitrary",), vmem_limit_bytes=VMEM_LIMIT_BYTES),
        name="wo_prep",
    )(wo, *([wo] * n_tail))
    return wab, wob


_GLA_QK = GLA_HEADS * GLA_DK
_GLA_V = GLA_HEADS * GLA_DV
_HGRN_K = HGRN_HEADS * HGRN_DK
_HGRN_V = HGRN_HEADS * HGRN_DV
_IN_OFFS = tuple(int(o) for o in np.cumsum(
    [0, _GLA_QK, _GLA_QK, _GLA_V, _GLA_V, GLA_GATE_RANK, _HGRN_K, _HGRN_K, _HGRN_V, _HGRN_V]))


def _win_prep_kernel(wt_ref, og_ref, oh_ref, ogr_ref):
    def cols(part, h, width):
        f0 = _IN_OFFS[part] + h * width
        return wt_ref[f0:f0 + width, :].T.astype(BF16)

    gr = wt_ref[_IN_OFFS[4]:_IN_OFFS[5], :]
    gr = jnp.concatenate([gr, jnp.zeros((LANES - GLA_GATE_RANK, gr.shape[1]), F32)], axis=0)
    ogr_ref[...] = gr.T.astype(BF16)
    for h in range(GLA_HEADS):
        og_ref[h] = jnp.concatenate(
            [cols(0, h, GLA_DK), cols(1, h, GLA_DK), cols(2, h, GLA_DV), cols(3, h, GLA_DV)],
            axis=1)
    for h in range(HGRN_HEADS):
        oh_ref[h] = jnp.concatenate(
            [cols(5, h, HGRN_DK), cols(6, h, HGRN_DK), cols(7, h, HGRN_DV), cols(8, h, HGRN_DV)],
            axis=1)


def _prep_mixer_weights(w_in, layer, w2, b2):
    _, d, in_width = w_in.shape
    assert in_width == _IN_OFFS[-1] and d % WPREP_ROWS == 0
    gla_cols = 2 * GLA_DK + 2 * GLA_DV
    hgrn_cols = 2 * HGRN_DK + 2 * HGRN_DV
    wt = jnp.swapaxes(w_in, 1, 2)
    w_gla, w_hg, w_gr = pl.pallas_call(
        _win_prep_kernel,
        grid=(d // WPREP_ROWS,),
        in_specs=[pl.BlockSpec((None, in_width, WPREP_ROWS), lambda r: (layer, 0, r))],
        out_specs=[pl.BlockSpec((GLA_HEADS, WPREP_ROWS, gla_cols), lambda r: (0, r, 0)),
                   pl.BlockSpec((HGRN_HEADS, WPREP_ROWS, hgrn_cols), lambda r: (0, r, 0)),
                   pl.BlockSpec((WPREP_ROWS, LANES), lambda r: (r, 0))],
        out_shape=[jax.ShapeDtypeStruct((GLA_HEADS, d, gla_cols), BF16),
                   jax.ShapeDtypeStruct((HGRN_HEADS, d, hgrn_cols), BF16),
                   jax.ShapeDtypeStruct((d, LANES), BF16)],
        compiler_params=pltpu.CompilerParams(
            dimension_semantics=("arbitrary",), vmem_limit_bytes=VMEM_LIMIT_BYTES),
        name="win_prep",
    )(wt)
    w2h = jnp.pad(w2.astype(BF16), ((0, LANES - GLA_GATE_RANK), (0, 0)))
    w2h = w2h.reshape(LANES, GLA_HEADS, GLA_DK).transpose(1, 0, 2)
    b2h = b2.reshape(GLA_HEADS, 1, GLA_DK)
    return w_gla, w_hg, w_gr, w2h, b2h


def kernel(x, c, ada_w, ada_b, norm_ffn1_w, ffn1_wi, ffn1_wo, norm_mix_w, w_in, gla_gate_w2,
           gla_gate_b2, gla_norm_w, hgrn_norm_w, hgrn_lower_bounds, w_out, norm_ffn2_w, ffn2_wi,
           ffn2_wo, final_norm_w):
    batch, seq, d = x.shape
    depth = ada_w.shape[0]
    m = batch * seq
    consts = _chunk_constants()
    xc = x.reshape(m, d)
    c_pad = jnp.pad(c, ((0, SUBLANES - batch % SUBLANES), (0, 0))) if batch % SUBLANES else c
    gla_v = GLA_HEADS * GLA_DV
    lb_heads = hgrn_lower_bounds.astype(F32).reshape(depth + 1, HGRN_HEADS, HGRN_DK).transpose(1, 0, 2)

    for l in range(depth):
        mod = _adaln(c_pad, ada_w[l], ada_b[l][None, :])[:batch].reshape(batch, N_MOD, d)
        wab1, wo1 = _prep_ffn_weights(ffn1_wi[l], ffn1_wo[l])
        wab2, wo2 = _prep_ffn_weights(ffn2_wi[l], ffn2_wo[l])
        w_gla, w_hg, w_gr, w2h, b2h = _prep_mixer_weights(w_in, l, gla_gate_w2[l], gla_gate_b2[l])
        wout = w_out[l].astype(BF16)

        x1, h2, gr = _ffn(xc, mod, norm_ffn1_w[l][None, :], norm_mix_w[l][None, :], wab1, wo1,
                          tokens_per_batch=seq, mod_base=0, epilogue="prenorm", w_aux=w_gr)
        o_gla = _mixer(h2, w_gla, (gr, w2h, b2h), gla_norm_w[l][None, :], consts, kind="gla",
                       layer=l, batch=batch, tokens_per_batch=seq, dk=GLA_DK, dv=GLA_DV)
        o_hg = _mixer(h2, w_hg, (lb_heads,), hgrn_norm_w[l][None, :], consts, kind="hgrn",
                      layer=l, batch=batch, tokens_per_batch=seq, dk=HGRN_DK, dv=HGRN_DV)
        last = l == depth - 1
        nw2 = final_norm_w[None, :] if last else norm_ffn2_w[l][None, :]
        (xc,) = _ffn(x1, mod, norm_ffn2_w[l][None, :], nw2, wab2, wo2,
                     tokens_per_batch=seq, mod_base=6, epilogue="final" if last else "none",
                     mix=(o_gla, o_hg, wout[:gla_v], wout[gla_v:]), mix_gate_row=5)
    return xc.reshape(batch, seq, d)
```

```python
import functools

import jax
import jax.numpy as jnp
import numpy as np
from jax import lax
from jax.experimental import pallas as pl
from jax.experimental.pallas import tpu as pltpu

F32 = jnp.float32
BF16 = jnp.bfloat16

GLA_HEADS = 4
GLA_DK = 128
GLA_DV = 256
GLA_GATE_RANK = 16
GLA_GATE_NORMALIZER = 16.0
HGRN_HEADS = 8
HGRN_DK = 128
HGRN_DV = 128
CHUNK = 64
MACARON_W = 0.5
N_MOD = 9
EPS = 1e-6

LANES = 128
SUBLANES = 8
MXU_N = 256
VMEM_LIMIT_BYTES = 56 * 1024 * 1024

FFN_TM = 512
FFN_TF = 512
PROLOGUE_ROWS = 256
NORM_ROWS = 16
AUX_ROWS = 128
MIX_TC = 1024
ADALN_TN = 1024
WPREP_ROWS = 256
SIDE_ROWS = 16

_NT = (((1,), (1,)), ((), ()))
_TN = (((0,), (0,)), ((), ()))


def _sigmoid(x):
    return jax.nn.sigmoid(x)


def _rms(x):
    return x * lax.rsqrt(jnp.mean(x * x, axis=-1, keepdims=True) + EPS)


def _adaln_kernel(c_ref, w_ref, b_ref, o_ref):
    c = c_ref[...]
    ca = (c * _sigmoid(c)).astype(BF16)
    o_ref[...] = jnp.dot(ca, w_ref[...].astype(BF16), preferred_element_type=F32) + b_ref[...]


def _adaln(c_pad, w, b):
    rows, d = c_pad.shape
    n = w.shape[1]
    assert n % ADALN_TN == 0
    return pl.pallas_call(
        _adaln_kernel,
        grid=(n // ADALN_TN,),
        in_specs=[
            pl.BlockSpec((rows, d), lambda j: (0, 0)),
            pl.BlockSpec((d, ADALN_TN), lambda j: (0, j)),
            pl.BlockSpec((1, ADALN_TN), lambda j: (0, j)),
        ],
        out_specs=pl.BlockSpec((rows, ADALN_TN), lambda j: (0, j)),
        out_shape=jax.ShapeDtypeStruct((rows, n), F32),
        compiler_params=pltpu.CompilerParams(
            dimension_semantics=("arbitrary",), vmem_limit_bytes=VMEM_LIMIT_BYTES),
        name="adaln",
    )(c_pad, w, b)


def _ffn_kernel(*refs, mod_base, epilogue, mix_gate_row):
    x_ref, mod_ref, nw_ref, nw2_ref, wab_ref, wo_ref, *rest = refs
    if mix_gate_row is not None:
        oa_ref, ob_ref, wa_ref, wb_ref, *rest = rest
    if epilogue == "prenorm":
        waux_ref, *rest = rest
        xo_ref, ho_ref, aux_ref, h_scr, *rest = rest
    else:
        xo_ref, h_scr, *rest = rest
    xin_ref = rest[0] if mix_gate_row is not None else x_ref
    f = pl.program_id(1)
    last = pl.num_programs(1) - 1
    tm = x_ref.shape[0]

    def prologue():
        shift = mod_ref[mod_base:mod_base + 1, :]
        gain = nw_ref[...] * (1.0 + mod_ref[mod_base + 1:mod_base + 2, :])
        for r0 in range(0, tm, PROLOGUE_ROWS):
            rows = slice(r0, r0 + PROLOGUE_ROWS)
            if mix_gate_row is not None:
                y = (jnp.dot(oa_ref[rows, :], wa_ref[...], preferred_element_type=F32)
                     + jnp.dot(ob_ref[rows, :], wb_ref[...], preferred_element_type=F32))
                xin_ref[rows, :] = x_ref[rows, :] + mod_ref[mix_gate_row:mix_gate_row + 1, :] * y
            for c0 in range(r0, r0 + PROLOGUE_ROWS, NORM_ROWS):
                chunk = slice(c0, c0 + NORM_ROWS)
                h_scr[chunk, :] = (_rms(xin_ref[chunk, :]) * gain + shift).astype(BF16)

    def swiglu_step(first):
        zab = jnp.dot(h_scr[...], wab_ref[...], preferred_element_type=F32)
        pieces = FFN_TF // MXU_N
        a = jnp.concatenate([zab[:, (2 * i) * MXU_N:(2 * i + 1) * MXU_N] for i in range(pieces)],
                            axis=1)
        b = jnp.concatenate([zab[:, (2 * i + 1) * MXU_N:(2 * i + 2) * MXU_N]
                             for i in range(pieces)], axis=1)
        act = (a * _sigmoid(a) * b).astype(BF16)
        update = jnp.dot(act, wo_ref[...], preferred_element_type=F32)
        if first:
            xo_ref[...] = update
        else:
            xo_ref[...] += update

    def finish():
        gate = MACARON_W * mod_ref[mod_base + 2:mod_base + 3, :]
        if epilogue == "prenorm":
            shift2 = mod_ref[mod_base + 3:mod_base + 4, :]
            gain2 = nw2_ref[...] * (1.0 + mod_ref[mod_base + 4:mod_base + 5, :])
        for c0 in range(0, tm, NORM_ROWS):
            chunk = slice(c0, c0 + NORM_ROWS)
            xn = xin_ref[chunk, :] + gate * xo_ref[chunk, :]
            if epilogue == "prenorm":
                xo_ref[chunk, :] = xn
                ho_ref[chunk, :] = (_rms(xn) * gain2 + shift2).astype(BF16)
            elif epilogue == "final":
                xo_ref[chunk, :] = _rms(xn) * nw2_ref[...]
            else:
                xo_ref[chunk, :] = xn
            done = c0 + NORM_ROWS
            if epilogue == "prenorm" and done % AUX_ROWS == 0:
                rows = slice(done - AUX_ROWS, done)
                aux_ref[rows, :] = jnp.dot(ho_ref[rows, :], waux_ref[...],
                                           preferred_element_type=F32).astype(BF16)

    @pl.when(f == 0)
    def _():
        prologue()
        swiglu_step(first=True)

    @pl.when(jnp.logical_and(f > 0, f < last))
    def _():
        swiglu_step(first=False)

    @pl.when(f == last)
    def _():
        swiglu_step(first=False)
        finish()


def _ffn(x2d, mod, nw, nw2, wab, wo, *, tokens_per_batch, mod_base, epilogue, mix=None,
         mix_gate_row=None, w_aux=None):
    m, d = x2d.shape
    n_f = wab.shape[0]
    tm = FFN_TM
    assert m % tm == 0 and tokens_per_batch % tm == 0 and wo.shape[0] == n_f * FFN_TF
    assert (mix is None) == (mix_gate_row is None) and n_f >= 2
    assert (w_aux is not None) == (epilogue == "prenorm")
    tiles_per_batch = tokens_per_batch // tm
    row_spec = pl.BlockSpec((tm, d), lambda i, f: (i, 0))
    vec_spec = pl.BlockSpec((1, d), lambda i, f: (0, 0))
    in_specs = [
        row_spec,
        pl.BlockSpec((None, N_MOD, d), lambda i, f: (i // tiles_per_batch, 0, 0)),
        vec_spec,
        vec_spec,
        pl.BlockSpec((None, d, 2 * FFN_TF), lambda i, f: (f, 0, 0)),
        pl.BlockSpec((FFN_TF, d), lambda i, f: (f, 0)),
    ]
    operands = [x2d, mod, nw, nw2, wab, wo]
    scratch = [pltpu.VMEM((tm, d), BF16)]
    if mix is not None:
        oa, ob, wa, wb = mix
        in_specs += [
            pl.BlockSpec((tm, oa.shape[1]), lambda i, f: (i, 0)),
            pl.BlockSpec((tm, ob.shape[1]), lambda i, f: (i, 0)),
            pl.BlockSpec(wa.shape, lambda i, f: (0, 0), pipeline_mode=pl.Buffered(1)),
            pl.BlockSpec(wb.shape, lambda i, f: (0, 0), pipeline_mode=pl.Buffered(1)),
        ]
        operands += [oa, ob, wa, wb]
        scratch.append(pltpu.VMEM((tm, d), F32))
    out_shape = [jax.ShapeDtypeStruct((m, d), F32)]
    out_specs = [row_spec]
    if epilogue == "prenorm":
        in_specs.append(pl.BlockSpec(w_aux.shape, lambda i, f: (0, 0)))
        operands.append(w_aux)
        out_shape += [jax.ShapeDtypeStruct((m, d), BF16),
                      jax.ShapeDtypeStruct((m, w_aux.shape[1]), BF16)]
        out_specs += [row_spec, pl.BlockSpec((tm, w_aux.shape[1]), lambda i, f: (i, 0))]
    return pl.pallas_call(
        functools.partial(_ffn_kernel, mod_base=mod_base, epilogue=epilogue,
                          mix_gate_row=mix_gate_row),
        grid=(m // tm, n_f),
        in_specs=in_specs,
        out_specs=out_specs,
        out_shape=out_shape,
        scratch_shapes=scratch,
        compiler_params=pltpu.CompilerParams(
            dimension_semantics=("arbitrary", "arbitrary"),
            vmem_limit_bytes=VMEM_LIMIT_BYTES),
        name="ffn_" + epilogue,
    )(*operands)


_LEVEL_HALVES = tuple(CHUNK >> (j + 1) for j in range(CHUNK.bit_length() - 1))
_N_LEVELS = len(_LEVEL_HALVES)
_CS_BLOCKS = _N_LEVELS + 2
_GROUP = 4
_GROUP_ROWS = _GROUP * CHUNK


def _chunk_constants():
    t = np.arange(CHUNK)
    tri = (t[None, :] <= t[:, None]).astype(np.float32)
    blocks = [tri]
    masks = []
    for half in _LEVEL_HALVES:
        ref = (t // (2 * half)) * (2 * half) + half
        blocks.append(tri - tri[ref])
        same_block = (t[:, None] // (2 * half)) == (t[None, :] // (2 * half))
        is_query = (t % (2 * half)) >= half
        masks.append((same_block & is_query[:, None] & ~is_query[None, :]).astype(np.float32))
    blocks.append(1.0 - tri)
    masks.append(np.eye(CHUNK, dtype=np.float32))
    cm = np.concatenate(blocks, axis=0)
    cm2 = np.concatenate([cm, cm], axis=1)
    group_masks = np.stack([np.kron(np.eye(_GROUP, dtype=np.float32), m) for m in masks], axis=0)
    return jnp.asarray(cm2, dtype=BF16), jnp.asarray(group_masks, dtype=F32)


def _mixer_kernel(*refs, tiles_per_head, side_job, **static):
    *io_refs, z_a, z_b, q_scr, k_scr, cs_scr, oi_scr, u_scr, s_scr = refs
    scratch = (q_scr, k_scr, cs_scr, oi_scr, u_scr, s_scr)
    s = pl.program_id(0)
    if side_job is not None:
        *io_refs, side_in, o_ref, side_out = io_refs
        io_refs.append(o_ref)
        side_job(s, side_in, side_out)

    @pl.when(s == 0)
    def _():
        z_b[...] = jnp.zeros_like(z_b)

    @pl.when(lax.rem(jnp.maximum(s - 1, 0), tiles_per_head) == 0)
    def _():
        s_scr[...] = jnp.zeros_like(s_scr)

    @pl.when(lax.rem(s, 2) == 0)
    def _():
        _mixer_tile(io_refs, scratch, z_b, z_a, **static)

    @pl.when(lax.rem(s, 2) == 1)
    def _():
        _mixer_tile(io_refs, scratch, z_a, z_b, **static)


def _mixer_tile(io_refs, scratch, z_scr, z_next, *, kind, layer, dk, dv, tc):
    if kind == "gla":
        h_ref, w_ref, gr_ref, w2_ref, b2_ref, nw_ref, cm_ref, mk_ref, o_ref = io_refs
    else:
        h_ref, w_ref, lb_ref, nw_ref, cm_ref, mk_ref, o_ref = io_refs
    q_scr, k_scr, cs_scr, oi_scr, u_scr, s_scr = scratch
    n_chunks = tc // CHUNK
    v_cols = slice(2 * dk, 2 * dk + dv)
    gate_cols = slice(2 * dk + dv, 2 * dk + 2 * dv)

    ncols = w_ref.shape[1]
    pieces = [slice(c0, min(c0 + MXU_N, ncols)) for c0 in range(0, ncols, MXU_N)]
    n_slots = n_chunks // _GROUP + 2

    def project_pieces(slot):
        lo = slot * len(pieces) // n_slots
        hi = (slot + 1) * len(pieces) // n_slots
        for cols in pieces[lo:hi]:
            z_next[:, cols] = jnp.dot(h_ref[...], w_ref[:, cols], preferred_element_type=F32)

    project_pieces(0)

    if kind == "gla":
        q_scr[...] = z_scr[:, 0:dk] * (dk ** -0.5)
        k_scr[...] = z_scr[:, dk:2 * dk]
        gp = jnp.dot(gr_ref[...], w2_ref[...], preferred_element_type=F32) + b2_ref[...]
        la = (jnp.minimum(gp, 0.0) - jnp.log(1.0 + jnp.exp(-jnp.abs(gp)))) * (
            1.0 / GLA_GATE_NORMALIZER)
    else:
        raw = lb_ref[...]
        ex = jnp.exp(raw - jnp.max(raw, axis=0, keepdims=True))
        p = ex / jnp.sum(ex, axis=0, keepdims=True)
        lb = jnp.sum(p[0:layer + 1, :], axis=0, keepdims=True)
        hq = z_scr[:, 0:dk]
        fr = z_scr[:, dk:2 * dk]
        q_scr[...] = hq * _sigmoid(hq)
        en = jnp.exp(-jnp.abs(fr))
        one_en = 1.0 + en
        log_sig = jnp.minimum(fr, 0.0) - jnp.log(one_en)
        sig_neg = jnp.where(fr >= 0.0, en, 1.0) / one_en
        la_a = jnp.log(lb)
        la_b = jnp.log(1.0 - lb) + log_sig
        la = jnp.maximum(la_a, la_b) + jnp.log(1.0 + jnp.exp(-jnp.abs(la_a - la_b)))
        k_scr[...] = (1.0 - lb) * sig_neg

    la_wide = jnp.concatenate([la[c * CHUNK:(c + 1) * CHUNK] for c in range(n_chunks)], axis=1)
    la_hi = la_wide.astype(BF16)
    la_lo = (la_wide - la_hi.astype(F32)).astype(BF16)
    cs_scr[...] = jnp.dot(cm_ref[...], jnp.concatenate([la_hi, la_lo], axis=0),
                          preferred_element_type=F32)

    def cs_block(block, c):
        return cs_scr[block * CHUNK:(block + 1) * CHUNK, c * dk:(c + 1) * dk]

    for g in range(n_chunks // _GROUP):
        project_pieces(g + 1)
        chunks = range(g * _GROUP, (g + 1) * _GROUP)
        rows = slice(g * _GROUP_ROWS, (g + 1) * _GROUP_ROWS)
        q = q_scr[rows, :]
        k = k_scr[rows, :]
        attn = mk_ref[_N_LEVELS] * jnp.sum(q * k, axis=-1, keepdims=True)
        for j in range(_N_LEVELS):
            d = jnp.concatenate([cs_block(j + 1, c) for c in chunks], axis=0)
            e = jnp.exp(-jnp.abs(d))
            s = lax.dot_general((q * e).astype(BF16), (k * e).astype(BF16), _NT,
                                preferred_element_type=F32)
            attn = attn + mk_ref[j] * s
        vb = z_scr[rows, v_cols].astype(BF16)
        oi_scr[rows, :] = jnp.dot(attn.astype(BF16), vb, preferred_element_type=F32)
        for i, c in enumerate(chunks):
            crow = slice(c * CHUNK, (c + 1) * CHUNK)
            kd = (k_scr[crow, :] * jnp.exp(cs_block(_N_LEVELS + 1, c))).astype(BF16)
            u_scr[c] = lax.dot_general(vb[i * CHUNK:(i + 1) * CHUNK], kd, _TN,
                                       preferred_element_type=F32)

    project_pieces(n_slots - 1)
    st = s_scr[...]
    for c in range(n_chunks):
        rows = slice(c * CHUNK, (c + 1) * CHUNK)
        b = cs_block(0, c)
        qb = (q_scr[rows, :] * jnp.exp(b)).astype(BF16)
        o = lax.dot_general(qb, st.astype(BF16), _NT, preferred_element_type=F32) + oi_scr[rows, :]
        st = st * jnp.exp(b[CHUNK - 1:CHUNK, :]) + u_scr[c]
        gate = z_scr[rows, gate_cols]
        o_ref[rows, :] = (_rms(o) * nw_ref[...] * (gate * _sigmoid(gate))).astype(BF16)
    s_scr[...] = st


def _side_wi(s, w_ref, o_ref, *, n_slabs, d_ff, n_f):
    @pl.when(s < n_slabs)
    def _():
        _wi_prep_kernel(w_ref, o_ref, d_ff=d_ff, n_f=n_f)


def _side_wo(s, w_ref, o_ref, *, n_src, n_out):
    @pl.when(s < n_src)
    def _():
        o_ref[...] = w_ref[...].astype(BF16)

    @pl.when(jnp.logical_and(s >= n_src, s < n_out))
    def _():
        o_ref[...] = jnp.zeros_like(o_ref)


def _side_job_specs(side, n_steps):
    what, w = side
    if what == "wi":
        d, d_ff = w.shape[0], w.shape[1] // 2
        n_f = -(-d_ff // FFN_TF)
        n_slabs = d // SIDE_ROWS
        assert d % SIDE_ROWS == 0 and n_slabs <= n_steps and d_ff % LANES == 0
        slab = lambda s: jnp.minimum(s, n_slabs - 1)
        return (functools.partial(_side_wi, n_slabs=n_slabs, d_ff=d_ff, n_f=n_f),
                pl.BlockSpec((SIDE_ROWS, 2 * d_ff), lambda s: (slab(s), 0)),
                pl.BlockSpec((n_f, SIDE_ROWS, 2 * FFN_TF), lambda s: (0, slab(s), 0)),
                jax.ShapeDtypeStruct((n_f, d, 2 * FFN_TF), BF16), w)
    assert what == "wo"
    d_ff, d = w.shape
    n_src = d_ff // LANES
    n_out = -(-d_ff // FFN_TF) * FFN_TF // LANES
    assert d_ff % LANES == 0 and n_out <= n_steps
    return (functools.partial(_side_wo, n_src=n_src, n_out=n_out),
            pl.BlockSpec((LANES, d), lambda s: (jnp.minimum(s, n_src - 1), 0)),
            pl.BlockSpec((LANES, d), lambda s: (jnp.minimum(s, n_out - 1), 0)),
            jax.ShapeDtypeStruct((n_out * LANES, d), BF16), w)


def _mixer(h2, w_heads, extra, nw, consts, *, kind, layer, batch, tokens_per_batch, dk, dv,
           side=None):
    m, d = h2.shape
    n_heads, _, ncols = w_heads.shape
    tc = MIX_TC
    assert tokens_per_batch % tc == 0 and tc % _GROUP_ROWS == 0
    nt = tokens_per_batch // tc
    n_tiles = batch * n_heads * nt
    cm, mk = consts

    def coords(tile):
        bb = tile // (n_heads * nt)
        hh = lax.rem(tile // nt, n_heads)
        return bb * nt + lax.rem(tile, nt), hh

    def projected(s):
        return coords(jnp.minimum(s, n_tiles - 1))

    def consumed(s):
        return coords(jnp.maximum(s - 1, 0))

    const2 = lambda s: (0, 0)
    in_specs = [
        pl.BlockSpec((tc, d), lambda s: (projected(s)[0], 0)),
        pl.BlockSpec((None, d, ncols), lambda s: (projected(s)[1], 0, 0)),
    ]
    if kind == "gla":
        gr, w2, b2 = extra
        in_specs += [
            pl.BlockSpec((tc, LANES), lambda s: (consumed(s)[0], 0)),
            pl.BlockSpec((None, LANES, dk), lambda s: (consumed(s)[1], 0, 0)),
            pl.BlockSpec((None, 1, dk), lambda s: (consumed(s)[1], 0, 0)),
        ]
    else:
        (lbraw,) = extra
        in_specs += [pl.BlockSpec((None, lbraw.shape[1], dk), lambda s: (consumed(s)[1], 0, 0))]
    in_specs += [
        pl.BlockSpec((1, dv), const2),
        pl.BlockSpec(cm.shape, const2),
        pl.BlockSpec(mk.shape, lambda s: (0, 0, 0)),
    ]
    operands = [h2, w_heads, *extra, nw, cm, mk]
    out_specs = [pl.BlockSpec((tc, dv), lambda s: consumed(s))]
    out_shape = [jax.ShapeDtypeStruct((m, n_heads * dv), BF16)]
    side_job = None
    if side is not None:
        side_job, side_in_spec, side_out_spec, side_shape, side_operand = _side_job_specs(
            side, n_tiles + 1)
        in_specs.append(side_in_spec)
        operands.append(side_operand)
        out_specs.append(side_out_spec)
        out_shape.append(side_shape)
    return pl.pallas_call(
        functools.partial(_mixer_kernel, tiles_per_head=nt, side_job=side_job, kind=kind,
                          layer=layer, dk=dk, dv=dv, tc=tc),
        grid=(n_tiles + 1,),
        in_specs=in_specs,
        out_specs=out_specs,
        out_shape=out_shape,
        scratch_shapes=[
            pltpu.VMEM((tc, ncols), F32),
            pltpu.VMEM((tc, ncols), F32),
            pltpu.VMEM((tc, dk), F32),
            pltpu.VMEM((tc, dk), F32),
            pltpu.VMEM((_CS_BLOCKS * CHUNK, (tc // CHUNK) * dk), F32),
            pltpu.VMEM((tc, dv), F32),
            pltpu.VMEM((tc // CHUNK, dv, dk), F32),
            pltpu.VMEM((dv, dk), F32),
        ],
        compiler_params=pltpu.CompilerParams(
            dimension_semantics=("arbitrary",), vmem_limit_bytes=VMEM_LIMIT_BYTES),
        name="mixer_" + kind,
    )(*operands)


def _wi_prep_kernel(w_ref, o_ref, *, d_ff, n_f):
    rows = w_ref.shape[0]
    for j in range(n_f):
        for i in range(FFN_TF // MXU_N):
            c0 = j * FFN_TF + i * MXU_N
            valid = max(0, min(MXU_N, d_ff - c0))
            for half in range(2):
                src = half * d_ff + c0
                parts = []
                if valid:
                    parts.append(w_ref[:, src:src + valid].astype(BF16))
                if valid < MXU_N:
                    parts.append(jnp.zeros((rows, MXU_N - valid), BF16))
                piece = parts[0] if len(parts) == 1 else jnp.concatenate(parts, axis=1)
                o_ref[j, :, (2 * i + half) * MXU_N:(2 * i + half + 1) * MXU_N] = piece


def _wo_prep_kernel(w_ref, *rest, n_full):
    *tail_refs, o_ref = rest
    r = pl.program_id(0)

    @pl.when(r < n_full)
    def _():
        o_ref[...] = w_ref[...].astype(BF16)

    @pl.when(r >= n_full)
    def _():
        rows = 0
        for t_ref in tail_refs:
            o_ref[rows:rows + LANES, :] = t_ref[...].astype(BF16)
            rows += LANES
        o_ref[rows:, :] = jnp.zeros((FFN_TF - rows, o_ref.shape[1]), BF16)


def _prep_ffn_weights(wi, wo):
    d, d_ff = wi.shape[0], wo.shape[0]
    assert d_ff % LANES == 0 and d % WPREP_ROWS == 0
    n_f = -(-d_ff // FFN_TF)
    wab = pl.pallas_call(
        functools.partial(_wi_prep_kernel, d_ff=d_ff, n_f=n_f),
        grid=(d // WPREP_ROWS,),
        in_specs=[pl.BlockSpec((WPREP_ROWS, 2 * d_ff), lambda r: (r, 0))],
        out_specs=pl.BlockSpec((n_f, WPREP_ROWS, 2 * FFN_TF), lambda r: (0, r, 0)),
        out_shape=jax.ShapeDtypeStruct((n_f, d, 2 * FFN_TF), BF16),
        compiler_params=pltpu.CompilerParams(
            dimension_semantics=("arbitrary",), vmem_limit_bytes=VMEM_LIMIT_BYTES),
        name="wi_prep",
    )(wi)
    n_full = d_ff // FFN_TF
    n_tail = (d_ff - n_full * FFN_TF) // LANES
    tail0 = n_full * FFN_TF // LANES
    tail_specs = [pl.BlockSpec((LANES, d), functools.partial(lambda r, k: (tail0 + k, 0), k=k))
                  for k in range(n_tail)]
    wob = pl.pallas_call(
        functools.partial(_wo_prep_kernel, n_full=n_full),
        grid=(n_f,),
        in_specs=[pl.BlockSpec((FFN_TF, d), lambda r: (jnp.minimum(r, n_full - 1), 0))]
        + tail_specs,
        out_specs=pl.BlockSpec((FFN_TF, d), lambda r: (r, 0)),
        out_shape=jax.ShapeDtypeStruct((n_f * FFN_TF, d), BF16),
        compiler_params=pltpu.CompilerParams(
            dimension_semantics=("arbitrary",), vmem_limit_bytes=VMEM_LIMIT_BYTES),
        name="wo_prep",
    )(wo, *([wo] * n_tail))
    return wab, wob


_GLA_QK = GLA_HEADS * GLA_DK
_GLA_V = GLA_HEADS * GLA_DV
_HGRN_K = HGRN_HEADS * HGRN_DK
_HGRN_V = HGRN_HEADS * HGRN_DV
_IN_OFFS = tuple(int(o) for o in np.cumsum(
    [0, _GLA_QK, _GLA_QK, _GLA_V, _GLA_V, GLA_GATE_RANK, _HGRN_K, _HGRN_K, _HGRN_V, _HGRN_V]))


def _win_prep_kernel(wt_ref, og_ref, oh_ref, ogr_ref):
    def cols(part, h, width):
        f0 = _IN_OFFS[part] + h * width
        return wt_ref[f0:f0 + width, :].T.astype(BF16)

    gr = wt_ref[_IN_OFFS[4]:_IN_OFFS[5], :]
    gr = jnp.concatenate([gr, jnp.zeros((LANES - GLA_GATE_RANK, gr.shape[1]), F32)], axis=0)
    ogr_ref[...] = gr.T.astype(BF16)
    for h in range(GLA_HEADS):
        og_ref[h] = jnp.concatenate(
            [cols(0, h, GLA_DK), cols(1, h, GLA_DK), cols(2, h, GLA_DV), cols(3, h, GLA_DV)],
            axis=1)
    for h in range(HGRN_HEADS):
        oh_ref[h] = jnp.concatenate(
            [cols(5, h, HGRN_DK), cols(6, h, HGRN_DK), cols(7, h, HGRN_DV), cols(8, h, HGRN_DV)],
            axis=1)


def _prep_mixer_weights(w_in, layer, w2, b2):
    _, d, in_width = w_in.shape
    assert in_width == _IN_OFFS[-1] and d % WPREP_ROWS == 0
    gla_cols = 2 * GLA_DK + 2 * GLA_DV
    hgrn_cols = 2 * HGRN_DK + 2 * HGRN_DV
    wt = jnp.swapaxes(w_in, 1, 2)
    w_gla, w_hg, w_gr = pl.pallas_call(
        _win_prep_kernel,
        grid=(d // WPREP_ROWS,),
        in_specs=[pl.BlockSpec((None, in_width, WPREP_ROWS), lambda r: (layer, 0, r))],
        out_specs=[pl.BlockSpec((GLA_HEADS, WPREP_ROWS, gla_cols), lambda r: (0, r, 0)),
                   pl.BlockSpec((HGRN_HEADS, WPREP_ROWS, hgrn_cols), lambda r: (0, r, 0)),
                   pl.BlockSpec((WPREP_ROWS, LANES), lambda r: (r, 0))],
        out_shape=[jax.ShapeDtypeStruct((GLA_HEADS, d, gla_cols), BF16),
                   jax.ShapeDtypeStruct((HGRN_HEADS, d, hgrn_cols), BF16),
                   jax.ShapeDtypeStruct((d, LANES), BF16)],
        compiler_params=pltpu.CompilerParams(
            dimension_semantics=("arbitrary",), vmem_limit_bytes=VMEM_LIMIT_BYTES),
        name="win_prep",
    )(wt)
    w2h = jnp.pad(w2.astype(BF16), ((0, LANES - GLA_GATE_RANK), (0, 0)))
    w2h = w2h.reshape(LANES, GLA_HEADS, GLA_DK).transpose(1, 0, 2)
    b2h = b2.reshape(GLA_HEADS, 1, GLA_DK)
    return w_gla, w_hg, w_gr, w2h, b2h


def kernel(x, c, ada_w, ada_b, norm_ffn1_w, ffn1_wi, ffn1_wo, norm_mix_w, w_in, gla_gate_w2,
           gla_gate_b2, gla_norm_w, hgrn_norm_w, hgrn_lower_bounds, w_out, norm_ffn2_w, ffn2_wi,
           ffn2_wo, final_norm_w):
    batch, seq, d = x.shape
    depth = ada_w.shape[0]
    m = batch * seq
    consts = _chunk_constants()
    xc = x.reshape(m, d)
    c_pad = jnp.pad(c, ((0, SUBLANES - batch % SUBLANES), (0, 0))) if batch % SUBLANES else c
    gla_v = GLA_HEADS * GLA_DV
    lb_heads = hgrn_lower_bounds.astype(F32).reshape(depth + 1, HGRN_HEADS, HGRN_DK).transpose(1, 0, 2)

    for l in range(depth):
        mod = _adaln(c_pad, ada_w[l], ada_b[l][None, :])[:batch].reshape(batch, N_MOD, d)
        wab1, wo1 = _prep_ffn_weights(ffn1_wi[l], ffn1_wo[l])
        w_gla, w_hg, w_gr, w2h, b2h = _prep_mixer_weights(w_in, l, gla_gate_w2[l], gla_gate_b2[l])
        wout = w_out[l].astype(BF16)

        x1, h2, gr = _ffn(xc, mod, norm_ffn1_w[l][None, :], norm_mix_w[l][None, :], wab1, wo1,
                          tokens_per_batch=seq, mod_base=0, epilogue="prenorm", w_aux=w_gr)
        o_gla, wo2 = _mixer(h2, w_gla, (gr, w2h, b2h), gla_norm_w[l][None, :], consts,
                            kind="gla", layer=l, batch=batch, tokens_per_batch=seq, dk=GLA_DK,
                            dv=GLA_DV, side=("wo", ffn2_wo[l]))
        o_hg, wab2 = _mixer(h2, w_hg, (lb_heads,), hgrn_norm_w[l][None, :], consts,
                            kind="hgrn", layer=l, batch=batch, tokens_per_batch=seq, dk=HGRN_DK,
                            dv=HGRN_DV, side=("wi", ffn2_wi[l]))
        last = l == depth - 1
        nw2 = final_norm_w[None, :] if last else norm_ffn2_w[l][None, :]
        (xc,) = _ffn(x1, mod, norm_ffn2_w[l][None, :], nw2, wab2, wo2,
                     tokens_per_batch=seq, mod_base=6, epilogue="final" if last else "none",
                     mix=(o_gla, o_hg, wout[:gla_v], wout[gla_v:]), mix_gate_row=5)
    return xc.reshape(batch, seq, d)
```

```python
import functools

import jax
import jax.numpy as jnp
import numpy as np
from jax import lax
from jax.experimental import pallas as pl
from jax.experimental.pallas import tpu as pltpu

F32 = jnp.float32
BF16 = jnp.bfloat16

GLA_HEADS = 4
GLA_DK = 128
GLA_DV = 256
GLA_GATE_RANK = 16
GLA_GATE_NORMALIZER = 16.0
HGRN_HEADS = 8
HGRN_DK = 128
HGRN_DV = 128
CHUNK = 64
MACARON_W = 0.5
N_MOD = 9
EPS = 1e-6

LANES = 128
SUBLANES = 8
MXU_N = 256
VMEM_LIMIT_BYTES = 56 * 1024 * 1024

FFN_TM = 512
FFN_TF = 512
FFN_TF_WIDE = 1024
FFN_PAD = 512
PROLOGUE_ROWS = 256
NORM_ROWS = 16
AUX_ROWS = 128
MIX_TC = 1024
ADALN_TN = 1024
WPREP_ROWS = 256
SIDE_ROWS = 16

_NT = (((1,), (1,)), ((), ()))
_TN = (((0,), (0,)), ((), ()))


def _sigmoid(x):
    return jax.nn.sigmoid(x)


def _rms(x):
    return x * lax.rsqrt(jnp.mean(x * x, axis=-1, keepdims=True) + EPS)


def _adaln_kernel(c_ref, w_ref, b_ref, o_ref):
    c = c_ref[...]
    ca = (c * _sigmoid(c)).astype(BF16)
    o_ref[...] = jnp.dot(ca, w_ref[...].astype(BF16), preferred_element_type=F32) + b_ref[...]


def _adaln(c_pad, w, b):
    rows, d = c_pad.shape
    n = w.shape[1]
    assert n % ADALN_TN == 0
    return pl.pallas_call(
        _adaln_kernel,
        grid=(n // ADALN_TN,),
        in_specs=[
            pl.BlockSpec((rows, d), lambda j: (0, 0)),
            pl.BlockSpec((d, ADALN_TN), lambda j: (0, j)),
            pl.BlockSpec((1, ADALN_TN), lambda j: (0, j)),
        ],
        out_specs=pl.BlockSpec((rows, ADALN_TN), lambda j: (0, j)),
        out_shape=jax.ShapeDtypeStruct((rows, n), F32),
        compiler_params=pltpu.CompilerParams(
            dimension_semantics=("arbitrary",), vmem_limit_bytes=VMEM_LIMIT_BYTES),
        name="adaln",
    )(c_pad, w, b)


def _ffn_kernel(*refs, mod_base, epilogue, mix_gate_row, last_cols):
    x_ref, mod_ref, nw_ref, nw2_ref, wab_ref, wo_ref, *rest = refs
    if mix_gate_row is not None:
        oa_ref, ob_ref, wa_ref, wb_ref, *rest = rest
    if epilogue == "prenorm":
        waux_ref, *rest = rest
        xo_ref, ho_ref, aux_ref, h_scr, *rest = rest
    else:
        xo_ref, h_scr, *rest = rest
    xin_ref = rest[0] if mix_gate_row is not None else x_ref
    f = pl.program_id(1)
    last = pl.num_programs(1) - 1
    tm = x_ref.shape[0]

    def prologue():
        shift = mod_ref[mod_base:mod_base + 1, :]
        gain = nw_ref[...] * (1.0 + mod_ref[mod_base + 1:mod_base + 2, :])
        for r0 in range(0, tm, PROLOGUE_ROWS):
            rows = slice(r0, r0 + PROLOGUE_ROWS)
            if mix_gate_row is not None:
                y = (jnp.dot(oa_ref[rows, :], wa_ref[...], preferred_element_type=F32)
                     + jnp.dot(ob_ref[rows, :], wb_ref[...], preferred_element_type=F32))
                xin_ref[rows, :] = x_ref[rows, :] + mod_ref[mix_gate_row:mix_gate_row + 1, :] * y
            for c0 in range(r0, r0 + PROLOGUE_ROWS, NORM_ROWS):
                chunk = slice(c0, c0 + NORM_ROWS)
                h_scr[chunk, :] = (_rms(xin_ref[chunk, :]) * gain + shift).astype(BF16)

    def swiglu_step(first, cols=None):
        cols = wo_ref.shape[0] if cols is None else cols
        zab = jnp.dot(h_scr[...], wab_ref[:, :2 * cols], preferred_element_type=F32)
        pieces = cols // MXU_N
        a = jnp.concatenate([zab[:, (2 * i) * MXU_N:(2 * i + 1) * MXU_N] for i in range(pieces)],
                            axis=1)
        b = jnp.concatenate([zab[:, (2 * i + 1) * MXU_N:(2 * i + 2) * MXU_N]
                             for i in range(pieces)], axis=1)
        act = (a * _sigmoid(a) * b).astype(BF16)
        update = jnp.dot(act, wo_ref[:cols, :], preferred_element_type=F32)
        if first:
            xo_ref[...] = update
        else:
            xo_ref[...] += update

    def finish():
        gate = MACARON_W * mod_ref[mod_base + 2:mod_base + 3, :]
        if epilogue == "prenorm":
            shift2 = mod_ref[mod_base + 3:mod_base + 4, :]
            gain2 = nw2_ref[...] * (1.0 + mod_ref[mod_base + 4:mod_base + 5, :])
        for c0 in range(0, tm, NORM_ROWS):
            chunk = slice(c0, c0 + NORM_ROWS)
            xn = xin_ref[chunk, :] + gate * xo_ref[chunk, :]
            if epilogue == "prenorm":
                xo_ref[chunk, :] = xn
                ho_ref[chunk, :] = (_rms(xn) * gain2 + shift2).astype(BF16)
            elif epilogue == "final":
                xo_ref[chunk, :] = _rms(xn) * nw2_ref[...]
            else:
                xo_ref[chunk, :] = xn
            done = c0 + NORM_ROWS
            if epilogue == "prenorm" and done % AUX_ROWS == 0:
                rows = slice(done - AUX_ROWS, done)
                aux_ref[rows, :] = jnp.dot(ho_ref[rows, :], waux_ref[...],
                                           preferred_element_type=F32).astype(BF16)

    @pl.when(f == 0)
    def _():
        prologue()
        swiglu_step(first=True)

    @pl.when(jnp.logical_and(f > 0, f < last))
    def _():
        swiglu_step(first=False)

    @pl.when(f == last)
    def _():
        swiglu_step(first=False, cols=last_cols)
        finish()


def _ffn(x2d, mod, nw, nw2, wab, wo, *, d_ff_pad, tokens_per_batch, mod_base, epilogue, mix=None,
         mix_gate_row=None, w_aux=None):
    m, d = x2d.shape
    n_f, _, tf2 = wab.shape
    tf = tf2 // 2
    tm = FFN_TM
    assert m % tm == 0 and tokens_per_batch % tm == 0 and wo.shape[0] == n_f * tf
    assert d_ff_pad % MXU_N == 0 and (n_f - 1) * tf < d_ff_pad <= n_f * tf
    last_cols = d_ff_pad - (n_f - 1) * tf
    assert (mix is None) == (mix_gate_row is None) and n_f >= 2
    assert (w_aux is not None) == (epilogue == "prenorm")
    tiles_per_batch = tokens_per_batch // tm
    row_spec = pl.BlockSpec((tm, d), lambda i, f: (i, 0))
    vec_spec = pl.BlockSpec((1, d), lambda i, f: (0, 0))
    in_specs = [
        row_spec,
        pl.BlockSpec((None, N_MOD, d), lambda i, f: (i // tiles_per_batch, 0, 0)),
        vec_spec,
        vec_spec,
        pl.BlockSpec((None, d, 2 * tf), lambda i, f: (f, 0, 0)),
        pl.BlockSpec((tf, d), lambda i, f: (f, 0)),
    ]
    operands = [x2d, mod, nw, nw2, wab, wo]
    scratch = [pltpu.VMEM((tm, d), BF16)]
    if mix is not None:
        oa, ob, wa, wb = mix
        in_specs += [
            pl.BlockSpec((tm, oa.shape[1]), lambda i, f: (i, 0)),
            pl.BlockSpec((tm, ob.shape[1]), lambda i, f: (i, 0)),
            pl.BlockSpec(wa.shape, lambda i, f: (0, 0), pipeline_mode=pl.Buffered(1)),
            pl.BlockSpec(wb.shape, lambda i, f: (0, 0), pipeline_mode=pl.Buffered(1)),
        ]
        operands += [oa, ob, wa, wb]
        scratch.append(pltpu.VMEM((tm, d), F32))
    out_shape = [jax.ShapeDtypeStruct((m, d), F32)]
    out_specs = [row_spec]
    if epilogue == "prenorm":
        in_specs.append(pl.BlockSpec(w_aux.shape, lambda i, f: (0, 0)))
        operands.append(w_aux)
        out_shape += [jax.ShapeDtypeStruct((m, d), BF16),
                      jax.ShapeDtypeStruct((m, w_aux.shape[1]), BF16)]
        out_specs += [row_spec, pl.BlockSpec((tm, w_aux.shape[1]), lambda i, f: (i, 0))]
    return pl.pallas_call(
        functools.partial(_ffn_kernel, mod_base=mod_base, epilogue=epilogue,
                          mix_gate_row=mix_gate_row, last_cols=last_cols),
        grid=(m // tm, n_f),
        in_specs=in_specs,
        out_specs=out_specs,
        out_shape=out_shape,
        scratch_shapes=scratch,
        compiler_params=pltpu.CompilerParams(
            dimension_semantics=("arbitrary", "arbitrary"),
            vmem_limit_bytes=VMEM_LIMIT_BYTES),
        name="ffn_" + epilogue,
    )(*operands)


_LEVEL_HALVES = tuple(CHUNK >> (j + 1) for j in range(CHUNK.bit_length() - 1))
_N_LEVELS = len(_LEVEL_HALVES)
_CS_BLOCKS = _N_LEVELS + 2
_GROUP = 4
_GROUP_ROWS = _GROUP * CHUNK


def _chunk_constants():
    t = np.arange(CHUNK)
    tri = (t[None, :] <= t[:, None]).astype(np.float32)
    blocks = [tri]
    masks = []
    for half in _LEVEL_HALVES:
        ref = (t // (2 * half)) * (2 * half) + half
        blocks.append(tri - tri[ref])
        same_block = (t[:, None] // (2 * half)) == (t[None, :] // (2 * half))
        is_query = (t % (2 * half)) >= half
        masks.append((same_block & is_query[:, None] & ~is_query[None, :]).astype(np.float32))
    blocks.append(1.0 - tri)
    masks.append(np.eye(CHUNK, dtype=np.float32))
    cm = np.concatenate(blocks, axis=0)
    cm2 = np.concatenate([cm, cm], axis=1)
    group_masks = np.stack([np.kron(np.eye(_GROUP, dtype=np.float32), m) for m in masks], axis=0)
    return jnp.asarray(cm2, dtype=BF16), jnp.asarray(group_masks, dtype=F32)


def _mixer_kernel(*refs, tiles_per_head, side_job, **static):
    *io_refs, z_a, z_b, q_scr, k_scr, cs_scr, oi_scr, u_scr, s_scr = refs
    scratch = (q_scr, k_scr, cs_scr, oi_scr, u_scr, s_scr)
    s = pl.program_id(0)
    if side_job is not None:
        *io_refs, side_in, o_ref, side_out = io_refs
        io_refs.append(o_ref)
        side_job(s, side_in, side_out)

    @pl.when(s == 0)
    def _():
        z_b[...] = jnp.zeros_like(z_b)

    @pl.when(lax.rem(jnp.maximum(s - 1, 0), tiles_per_head) == 0)
    def _():
        s_scr[...] = jnp.zeros_like(s_scr)

    @pl.when(lax.rem(s, 2) == 0)
    def _():
        _mixer_tile(io_refs, scratch, z_b, z_a, **static)

    @pl.when(lax.rem(s, 2) == 1)
    def _():
        _mixer_tile(io_refs, scratch, z_a, z_b, **static)


def _mixer_tile(io_refs, scratch, z_scr, z_next, *, kind, layer, dk, dv, tc):
    if kind == "gla":
        h_ref, w_ref, gr_ref, w2_ref, b2_ref, nw_ref, cm_ref, mk_ref, o_ref = io_refs
    else:
        h_ref, w_ref, lb_ref, nw_ref, cm_ref, mk_ref, o_ref = io_refs
    q_scr, k_scr, cs_scr, oi_scr, u_scr, s_scr = scratch
    n_chunks = tc // CHUNK
    v_cols = slice(2 * dk, 2 * dk + dv)
    gate_cols = slice(2 * dk + dv, 2 * dk + 2 * dv)

    ncols = w_ref.shape[1]
    pieces = [slice(c0, min(c0 + MXU_N, ncols)) for c0 in range(0, ncols, MXU_N)]
    n_slots = n_chunks // _GROUP + 2

    def project_pieces(slot):
        lo = slot * len(pieces) // n_slots
        hi = (slot + 1) * len(pieces) // n_slots
        for cols in pieces[lo:hi]:
            z_next[:, cols] = jnp.dot(h_ref[...], w_ref[:, cols], preferred_element_type=F32)

    project_pieces(0)

    if kind == "gla":
        q_scr[...] = z_scr[:, 0:dk] * (dk ** -0.5)
        k_scr[...] = z_scr[:, dk:2 * dk]
        gp = jnp.dot(gr_ref[...], w2_ref[...], preferred_element_type=F32) + b2_ref[...]
        la = (jnp.minimum(gp, 0.0) - jnp.log(1.0 + jnp.exp(-jnp.abs(gp)))) * (
            1.0 / GLA_GATE_NORMALIZER)
    else:
        raw = lb_ref[...]
        ex = jnp.exp(raw - jnp.max(raw, axis=0, keepdims=True))
        p = ex / jnp.sum(ex, axis=0, keepdims=True)
        lb = jnp.sum(p[0:layer + 1, :], axis=0, keepdims=True)
        hq = z_scr[:, 0:dk]
        fr = z_scr[:, dk:2 * dk]
        q_scr[...] = hq * _sigmoid(hq)
        en = jnp.exp(-jnp.abs(fr))
        one_en = 1.0 + en
        log_sig = jnp.minimum(fr, 0.0) - jnp.log(one_en)
        sig_neg = jnp.where(fr >= 0.0, en, 1.0) / one_en
        la_a = jnp.log(lb)
        la_b = jnp.log(1.0 - lb) + log_sig
        la = jnp.maximum(la_a, la_b) + jnp.log(1.0 + jnp.exp(-jnp.abs(la_a - la_b)))
        k_scr[...] = (1.0 - lb) * sig_neg

    la_wide = jnp.concatenate([la[c * CHUNK:(c + 1) * CHUNK] for c in range(n_chunks)], axis=1)
    la_hi = la_wide.astype(BF16)
    la_lo = (la_wide - la_hi.astype(F32)).astype(BF16)
    cs_scr[...] = jnp.dot(cm_ref[...], jnp.concatenate([la_hi, la_lo], axis=0),
                          preferred_element_type=F32)

    def cs_block(block, c):
        return cs_scr[block * CHUNK:(block + 1) * CHUNK, c * dk:(c + 1) * dk]

    for g in range(n_chunks // _GROUP):
        project_pieces(g + 1)
        chunks = range(g * _GROUP, (g + 1) * _GROUP)
        rows = slice(g * _GROUP_ROWS, (g + 1) * _GROUP_ROWS)
        q = q_scr[rows, :]
        k = k_scr[rows, :]
        attn = mk_ref[_N_LEVELS] * jnp.sum(q * k, axis=-1, keepdims=True)
        for j in range(_N_LEVELS):
            d = jnp.concatenate([cs_block(j + 1, c) for c in chunks], axis=0)
            e = jnp.exp(-jnp.abs(d))
            s = lax.dot_general((q * e).astype(BF16), (k * e).astype(BF16), _NT,
                                preferred_element_type=F32)
            attn = attn + mk_ref[j] * s
        vb = z_scr[rows, v_cols].astype(BF16)
        oi_scr[rows, :] = jnp.dot(attn.astype(BF16), vb, preferred_element_type=F32)
        for i, c in enumerate(chunks):
            crow = slice(c * CHUNK, (c + 1) * CHUNK)
            kd = (k_scr[crow, :] * jnp.exp(cs_block(_N_LEVELS + 1, c))).astype(BF16)
            u_scr[c] = lax.dot_general(vb[i * CHUNK:(i + 1) * CHUNK], kd, _TN,
                                       preferred_element_type=F32)

    project_pieces(n_slots - 1)
    st = s_scr[...]
    for c in range(n_chunks):
        rows = slice(c * CHUNK, (c + 1) * CHUNK)
        b = cs_block(0, c)
        qb = (q_scr[rows, :] * jnp.exp(b)).astype(BF16)
        o = lax.dot_general(qb, st.astype(BF16), _NT, preferred_element_type=F32) + oi_scr[rows, :]
        st = st * jnp.exp(b[CHUNK - 1:CHUNK, :]) + u_scr[c]
        gate = z_scr[rows, gate_cols]
        o_ref[rows, :] = (_rms(o) * nw_ref[...] * (gate * _sigmoid(gate))).astype(BF16)
    s_scr[...] = st


def _side_wi(s, w_ref, o_ref, *, n_slabs, d_ff, n_f, tf):
    @pl.when(s < n_slabs)
    def _():
        _wi_prep_kernel(w_ref, o_ref, d_ff=d_ff, n_f=n_f, tf=tf)


def _side_wo(s, w_ref, o_ref, *, n_src, n_out):
    @pl.when(s < n_src)
    def _():
        o_ref[...] = w_ref[...].astype(BF16)

    @pl.when(jnp.logical_and(s >= n_src, s < n_out))
    def _():
        o_ref[...] = jnp.zeros_like(o_ref)


def _side_job_specs(side, n_steps):
    what, w = side
    if what == "wi":
        d, d_ff = w.shape[0], w.shape[1] // 2
        n_f = -(-d_ff // FFN_TF)
        tf = FFN_TF
        n_slabs = d // SIDE_ROWS
        assert d % SIDE_ROWS == 0 and n_slabs <= n_steps and d_ff % LANES == 0
        slab = lambda s: jnp.minimum(s, n_slabs - 1)
        return (functools.partial(_side_wi, n_slabs=n_slabs, d_ff=d_ff, n_f=n_f, tf=tf),
                pl.BlockSpec((SIDE_ROWS, 2 * d_ff), lambda s: (slab(s), 0)),
                pl.BlockSpec((n_f, SIDE_ROWS, 2 * tf), lambda s: (0, slab(s), 0)),
                jax.ShapeDtypeStruct((n_f, d, 2 * tf), BF16), w)
    assert what == "wo"
    d_ff, d = w.shape
    n_src = d_ff // LANES
    n_out = -(-d_ff // FFN_TF) * FFN_TF // LANES
    assert d_ff % LANES == 0 and n_out <= n_steps
    return (functools.partial(_side_wo, n_src=n_src, n_out=n_out),
            pl.BlockSpec((LANES, d), lambda s: (jnp.minimum(s, n_src - 1), 0)),
            pl.BlockSpec((LANES, d), lambda s: (jnp.minimum(s, n_out - 1), 0)),
            jax.ShapeDtypeStruct((n_out * LANES, d), BF16), w)


def _mixer(h2, w_heads, extra, nw, consts, *, kind, layer, batch, tokens_per_batch, dk, dv,
           side=None):
    m, d = h2.shape
    n_heads, _, ncols = w_heads.shape
    tc = MIX_TC
    assert tokens_per_batch % tc == 0 and tc % _GROUP_ROWS == 0
    nt = tokens_per_batch // tc
    n_tiles = batch * n_heads * nt
    cm, mk = consts

    def coords(tile):
        bb = tile // (n_heads * nt)
        hh = lax.rem(tile // nt, n_heads)
        return bb * nt + lax.rem(tile, nt), hh

    def projected(s):
        return coords(jnp.minimum(s, n_tiles - 1))

    def consumed(s):
        return coords(jnp.maximum(s - 1, 0))

    const2 = lambda s: (0, 0)
    in_specs = [
        pl.BlockSpec((tc, d), lambda s: (projected(s)[0], 0)),
        pl.BlockSpec((None, d, ncols), lambda s: (projected(s)[1], 0, 0)),
    ]
    if kind == "gla":
        gr, w2, b2 = extra
        in_specs += [
            pl.BlockSpec((tc, LANES), lambda s: (consumed(s)[0], 0)),
            pl.BlockSpec((None, LANES, dk), lambda s: (consumed(s)[1], 0, 0)),
            pl.BlockSpec((None, 1, dk), lambda s: (consumed(s)[1], 0, 0)),
        ]
    else:
        (lbraw,) = extra
        in_specs += [pl.BlockSpec((None, lbraw.shape[1], dk), lambda s: (consumed(s)[1], 0, 0))]
    in_specs += [
        pl.BlockSpec((1, dv), const2),
        pl.BlockSpec(cm.shape, const2),
        pl.BlockSpec(mk.shape, lambda s: (0, 0, 0)),
    ]
    operands = [h2, w_heads, *extra, nw, cm, mk]
    out_specs = [pl.BlockSpec((tc, dv), lambda s: consumed(s))]
    out_shape = [jax.ShapeDtypeStruct((m, n_heads * dv), BF16)]
    side_job = None
    if side is not None:
        side_job, side_in_spec, side_out_spec, side_shape, side_operand = _side_job_specs(
            side, n_tiles + 1)
        in_specs.append(side_in_spec)
        operands.append(side_operand)
        out_specs.append(side_out_spec)
        out_shape.append(side_shape)
    return pl.pallas_call(
        functools.partial(_mixer_kernel, tiles_per_head=nt, side_job=side_job, kind=kind,
                          layer=layer, dk=dk, dv=dv, tc=tc),
        grid=(n_tiles + 1,),
        in_specs=in_specs,
        out_specs=out_specs,
        out_shape=out_shape,
        scratch_shapes=[
            pltpu.VMEM((tc, ncols), F32),
            pltpu.VMEM((tc, ncols), F32),
            pltpu.VMEM((tc, dk), F32),
            pltpu.VMEM((tc, dk), F32),
            pltpu.VMEM((_CS_BLOCKS * CHUNK, (tc // CHUNK) * dk), F32),
            pltpu.VMEM((tc, dv), F32),
            pltpu.VMEM((tc // CHUNK, dv, dk), F32),
            pltpu.VMEM((dv, dk), F32),
        ],
        compiler_params=pltpu.CompilerParams(
            dimension_semantics=("arbitrary",), vmem_limit_bytes=VMEM_LIMIT_BYTES),
        name="mixer_" + kind,
    )(*operands)


def _wi_prep_kernel(w_ref, o_ref, *, d_ff, n_f, tf):
    rows = w_ref.shape[0]
    for j in range(n_f):
        for i in range(tf // MXU_N):
            c0 = j * tf + i * MXU_N
            valid = max(0, min(MXU_N, d_ff - c0))
            for half in range(2):
                src = half * d_ff + c0
                parts = []
                if valid:
                    parts.append(w_ref[:, src:src + valid].astype(BF16))
                if valid < MXU_N:
                    parts.append(jnp.zeros((rows, MXU_N - valid), BF16))
                piece = parts[0] if len(parts) == 1 else jnp.concatenate(parts, axis=1)
                o_ref[j, :, (2 * i + half) * MXU_N:(2 * i + half + 1) * MXU_N] = piece


def _wo_prep_kernel(w_ref, *rest, n_full):
    *tail_refs, o_ref = rest
    r = pl.program_id(0)

    @pl.when(r < n_full)
    def _():
        o_ref[...] = w_ref[...].astype(BF16)

    @pl.when(r >= n_full)
    def _():
        rows = 0
        for t_ref in tail_refs:
            o_ref[rows:rows + LANES, :] = t_ref[...].astype(BF16)
            rows += LANES
        o_ref[rows:, :] = jnp.zeros((o_ref.shape[0] - rows, o_ref.shape[1]), BF16)


def _prep_ffn_weights(wi, wo, tf):
    d, d_ff = wi.shape[0], wo.shape[0]
    assert d_ff % LANES == 0 and d % WPREP_ROWS == 0
    n_f = -(-d_ff // tf)
    wab = pl.pallas_call(
        functools.partial(_wi_prep_kernel, d_ff=d_ff, n_f=n_f, tf=tf),
        grid=(d // WPREP_ROWS,),
        in_specs=[pl.BlockSpec((WPREP_ROWS, 2 * d_ff), lambda r: (r, 0))],
        out_specs=pl.BlockSpec((n_f, WPREP_ROWS, 2 * tf), lambda r: (0, r, 0)),
        out_shape=jax.ShapeDtypeStruct((n_f, d, 2 * tf), BF16),
        compiler_params=pltpu.CompilerParams(
            dimension_semantics=("arbitrary",), vmem_limit_bytes=VMEM_LIMIT_BYTES),
        name="wi_prep",
    )(wi)
    n_full = d_ff // tf
    n_tail = (d_ff - n_full * tf) // LANES
    tail0 = n_full * tf // LANES
    tail_specs = [pl.BlockSpec((LANES, d), functools.partial(lambda r, k: (tail0 + k, 0), k=k))
                  for k in range(n_tail)]
    wob = pl.pallas_call(
        functools.partial(_wo_prep_kernel, n_full=n_full),
        grid=(n_f,),
        in_specs=[pl.BlockSpec((tf, d), lambda r: (jnp.minimum(r, n_full - 1), 0))]
        + tail_specs,
        out_specs=pl.BlockSpec((tf, d), lambda r: (r, 0)),
        out_shape=jax.ShapeDtypeStruct((n_f * tf, d), BF16),
        compiler_params=pltpu.CompilerParams(
            dimension_semantics=("arbitrary",), vmem_limit_bytes=VMEM_LIMIT_BYTES),
        name="wo_prep",
    )(wo, *([wo] * n_tail))
    return wab, wob


_GLA_QK = GLA_HEADS * GLA_DK
_GLA_V = GLA_HEADS * GLA_DV
_HGRN_K = HGRN_HEADS * HGRN_DK
_HGRN_V = HGRN_HEADS * HGRN_DV
_IN_OFFS = tuple(int(o) for o in np.cumsum(
    [0, _GLA_QK, _GLA_QK, _GLA_V, _GLA_V, GLA_GATE_RANK, _HGRN_K, _HGRN_K, _HGRN_V, _HGRN_V]))


def _win_prep_kernel(wt_ref, og_ref, oh_ref, ogr_ref):
    def cols(part, h, width):
        f0 = _IN_OFFS[part] + h * width
        return wt_ref[f0:f0 + width, :].T.astype(BF16)

    gr = wt_ref[_IN_OFFS[4]:_IN_OFFS[5], :]
    gr = jnp.concatenate([gr, jnp.zeros((LANES - GLA_GATE_RANK, gr.shape[1]), F32)], axis=0)
    ogr_ref[...] = gr.T.astype(BF16)
    for h in range(GLA_HEADS):
        og_ref[h] = jnp.concatenate(
            [cols(0, h, GLA_DK), cols(1, h, GLA_DK), cols(2, h, GLA_DV), cols(3, h, GLA_DV)],
            axis=1)
    for h in range(HGRN_HEADS):
        oh_ref[h] = jnp.concatenate(
            [cols(5, h, HGRN_DK), cols(6, h, HGRN_DK), cols(7, h, HGRN_DV), cols(8, h, HGRN_DV)],
            axis=1)


def _prep_mixer_weights(w_in, layer, w2, b2):
    _, d, in_width = w_in.shape
    assert in_width == _IN_OFFS[-1] and d % WPREP_ROWS == 0
    gla_cols = 2 * GLA_DK + 2 * GLA_DV
    hgrn_cols = 2 * HGRN_DK + 2 * HGRN_DV
    wt = jnp.swapaxes(w_in, 1, 2)
    w_gla, w_hg, w_gr = pl.pallas_call(
        _win_prep_kernel,
        grid=(d // WPREP_ROWS,),
        in_specs=[pl.BlockSpec((None, in_width, WPREP_ROWS), lambda r: (layer, 0, r))],
        out_specs=[pl.BlockSpec((GLA_HEADS, WPREP_ROWS, gla_cols), lambda r: (0, r, 0)),
                   pl.BlockSpec((HGRN_HEADS, WPREP_ROWS, hgrn_cols), lambda r: (0, r, 0)),
                   pl.BlockSpec((WPREP_ROWS, LANES), lambda r: (r, 0))],
        out_shape=[jax.ShapeDtypeStruct((GLA_HEADS, d, gla_cols), BF16),
                   jax.ShapeDtypeStruct((HGRN_HEADS, d, hgrn_cols), BF16),
                   jax.ShapeDtypeStruct((d, LANES), BF16)],
        compiler_params=pltpu.CompilerParams(
            dimension_semantics=("arbitrary",), vmem_limit_bytes=VMEM_LIMIT_BYTES),
        name="win_prep",
    )(wt)
    w2h = jnp.pad(w2.astype(BF16), ((0, LANES - GLA_GATE_RANK), (0, 0)))
    w2h = w2h.reshape(LANES, GLA_HEADS, GLA_DK).transpose(1, 0, 2)
    b2h = b2.reshape(GLA_HEADS, 1, GLA_DK)
    return w_gla, w_hg, w_gr, w2h, b2h


def kernel(x, c, ada_w, ada_b, norm_ffn1_w, ffn1_wi, ffn1_wo, norm_mix_w, w_in, gla_gate_w2,
           gla_gate_b2, gla_norm_w, hgrn_norm_w, hgrn_lower_bounds, w_out, norm_ffn2_w, ffn2_wi,
           ffn2_wo, final_norm_w):
    batch, seq, d = x.shape
    depth = ada_w.shape[0]
    m = batch * seq
    consts = _chunk_constants()
    xc = x.reshape(m, d)
    c_pad = jnp.pad(c, ((0, SUBLANES - batch % SUBLANES), (0, 0))) if batch % SUBLANES else c
    gla_v = GLA_HEADS * GLA_DV
    lb_heads = hgrn_lower_bounds.astype(F32).reshape(depth + 1, HGRN_HEADS, HGRN_DK).transpose(1, 0, 2)

    for l in range(depth):
        mod = _adaln(c_pad, ada_w[l], ada_b[l][None, :])[:batch].reshape(batch, N_MOD, d)
        d_ff_pad = -(-ffn1_wo.shape[1] // FFN_PAD) * FFN_PAD
        wab1, wo1 = _prep_ffn_weights(ffn1_wi[l], ffn1_wo[l], FFN_TF_WIDE)
        w_gla, w_hg, w_gr, w2h, b2h = _prep_mixer_weights(w_in, l, gla_gate_w2[l], gla_gate_b2[l])
        wout = w_out[l].astype(BF16)

        x1, h2, gr = _ffn(xc, mod, norm_ffn1_w[l][None, :], norm_mix_w[l][None, :], wab1, wo1,
                          d_ff_pad=d_ff_pad, tokens_per_batch=seq, mod_base=0, epilogue="prenorm",
                          w_aux=w_gr)
        o_gla, wo2 = _mixer(h2, w_gla, (gr, w2h, b2h), gla_norm_w[l][None, :], consts,
                            kind="gla", layer=l, batch=batch, tokens_per_batch=seq, dk=GLA_DK,
                            dv=GLA_DV, side=("wo", ffn2_wo[l]))
        o_hg, wab2 = _mixer(h2, w_hg, (lb_heads,), hgrn_norm_w[l][None, :], consts,
                            kind="hgrn", layer=l, batch=batch, tokens_per_batch=seq, dk=HGRN_DK,
                            dv=HGRN_DV, side=("wi", ffn2_wi[l]))
        last = l == depth - 1
        nw2 = final_norm_w[None, :] if last else norm_ffn2_w[l][None, :]
        (xc,) = _ffn(x1, mod, norm_ffn2_w[l][None, :], nw2, wab2, wo2, d_ff_pad=d_ff_pad,
                     tokens_per_batch=seq, mod_base=6, epilogue="final" if last else "none",
                     mix=(o_gla, o_hg, wout[:gla_v], wout[gla_v:]), mix_gate_row=5)
    return xc.reshape(batch, seq, d)
```

```python
import functools

import jax
import jax.numpy as jnp
import numpy as np
from jax import lax
from jax.experimental import pallas as pl
from jax.experimental.pallas import tpu as pltpu

F32 = jnp.float32
BF16 = jnp.bfloat16

GLA_HEADS = 4
GLA_DK = 128
GLA_DV = 256
GLA_GATE_RANK = 16
GLA_GATE_NORMALIZER = 16.0
HGRN_HEADS = 8
HGRN_DK = 128
HGRN_DV = 128
CHUNK = 64
MACARON_W = 0.5
N_MOD = 9
EPS = 1e-6

LANES = 128
SUBLANES = 8
MXU_N = 256
VMEM_LIMIT_BYTES = 56 * 1024 * 1024

FFN_TM = 512
FFN_TF = 512
FFN_TF_WIDE = 1024
FFN_PAD = 512
PROLOGUE_ROWS = 256
NORM_ROWS = 16
AUX_ROWS = 128
MIX_TC = 1024
ADALN_TN = 1024
WPREP_ROWS = 256
SIDE_ROWS = 16

_NT = (((1,), (1,)), ((), ()))
_TN = (((0,), (0,)), ((), ()))


def _sigmoid(x):
    return jax.nn.sigmoid(x)


def _rms(x):
    return x * lax.rsqrt(jnp.mean(x * x, axis=-1, keepdims=True) + EPS)


def _adaln_kernel(c_ref, w_ref, b_ref, o_ref):
    c = c_ref[...]
    ca = (c * _sigmoid(c)).astype(BF16)
    o_ref[...] = jnp.dot(ca, w_ref[...].astype(BF16), preferred_element_type=F32) + b_ref[...]


def _adaln(c_pad, w, b):
    rows, d = c_pad.shape
    n = w.shape[1]
    assert n % ADALN_TN == 0
    return pl.pallas_call(
        _adaln_kernel,
        grid=(n // ADALN_TN,),
        in_specs=[
            pl.BlockSpec((rows, d), lambda j: (0, 0)),
            pl.BlockSpec((d, ADALN_TN), lambda j: (0, j)),
            pl.BlockSpec((1, ADALN_TN), lambda j: (0, j)),
        ],
        out_specs=pl.BlockSpec((rows, ADALN_TN), lambda j: (0, j)),
        out_shape=jax.ShapeDtypeStruct((rows, n), F32),
        compiler_params=pltpu.CompilerParams(
            dimension_semantics=("arbitrary",), vmem_limit_bytes=VMEM_LIMIT_BYTES),
        name="adaln",
    )(c_pad, w, b)


def _ffn_kernel(*refs, mod_base, epilogue, mix_gate_row, last_cols):
    x_ref, mod_ref, nw_ref, nw2_ref, wab_ref, wo_ref, *rest = refs
    if mix_gate_row is not None:
        oa_ref, ob_ref, wa_ref, wb_ref, *rest = rest
    if epilogue == "prenorm":
        waux_ref, *rest = rest
        xo_ref, ho_ref, aux_ref, h_scr, *rest = rest
    else:
        xo_ref, h_scr, *rest = rest
    xin_ref = rest[0] if mix_gate_row is not None else x_ref
    f = pl.program_id(1)
    last = pl.num_programs(1) - 1
    tm = x_ref.shape[0]

    def prologue():
        shift = mod_ref[mod_base:mod_base + 1, :]
        gain = nw_ref[...] * (1.0 + mod_ref[mod_base + 1:mod_base + 2, :])
        for r0 in range(0, tm, PROLOGUE_ROWS):
            rows = slice(r0, r0 + PROLOGUE_ROWS)
            if mix_gate_row is not None:
                y = (jnp.dot(oa_ref[rows, :], wa_ref[...], preferred_element_type=F32)
                     + jnp.dot(ob_ref[rows, :], wb_ref[...], preferred_element_type=F32))
                xin_ref[rows, :] = x_ref[rows, :] + mod_ref[mix_gate_row:mix_gate_row + 1, :] * y
            for c0 in range(r0, r0 + PROLOGUE_ROWS, NORM_ROWS):
                chunk = slice(c0, c0 + NORM_ROWS)
                h_scr[chunk, :] = (_rms(xin_ref[chunk, :]) * gain + shift).astype(BF16)

    def swiglu_step(first, cols=None):
        cols = wo_ref.shape[0] if cols is None else cols
        zab = jnp.dot(h_scr[...], wab_ref[:, :2 * cols], preferred_element_type=F32)
        pieces = cols // MXU_N
        a = jnp.concatenate([zab[:, (2 * i) * MXU_N:(2 * i + 1) * MXU_N] for i in range(pieces)],
                            axis=1)
        b = jnp.concatenate([zab[:, (2 * i + 1) * MXU_N:(2 * i + 2) * MXU_N]
                             for i in range(pieces)], axis=1)
        act = (a * _sigmoid(a) * b).astype(BF16)
        update = jnp.dot(act, wo_ref[:cols, :], preferred_element_type=F32)
        if first:
            xo_ref[...] = update
        else:
            xo_ref[...] += update

    def finish():
        gate = MACARON_W * mod_ref[mod_base + 2:mod_base + 3, :]
        if epilogue == "prenorm":
            shift2 = mod_ref[mod_base + 3:mod_base + 4, :]
            gain2 = nw2_ref[...] * (1.0 + mod_ref[mod_base + 4:mod_base + 5, :])
        for c0 in range(0, tm, NORM_ROWS):
            chunk = slice(c0, c0 + NORM_ROWS)
            xn = xin_ref[chunk, :] + gate * xo_ref[chunk, :]
            if epilogue == "prenorm":
                xo_ref[chunk, :] = xn
                ho_ref[chunk, :] = (_rms(xn) * gain2 + shift2).astype(BF16)
            elif epilogue == "final":
                xo_ref[chunk, :] = _rms(xn) * nw2_ref[...]
            else:
                xo_ref[chunk, :] = xn
            done = c0 + NORM_ROWS
            if epilogue == "prenorm" and done % AUX_ROWS == 0:
                rows = slice(done - AUX_ROWS, done)
                aux_ref[rows, :] = jnp.dot(ho_ref[rows, :], waux_ref[...],
                                           preferred_element_type=F32).astype(BF16)

    @pl.when(f == 0)
    def _():
        prologue()
        swiglu_step(first=True)

    @pl.when(jnp.logical_and(f > 0, f < last))
    def _():
        swiglu_step(first=False)

    @pl.when(f == last)
    def _():
        swiglu_step(first=False, cols=last_cols)
        finish()


def _ffn(x2d, mod, nw, nw2, wab, wo, *, d_ff_pad, tokens_per_batch, mod_base, epilogue, mix=None,
         mix_gate_row=None, w_aux=None):
    m, d = x2d.shape
    n_f, _, tf2 = wab.shape
    tf = tf2 // 2
    tm = FFN_TM
    assert m % tm == 0 and tokens_per_batch % tm == 0 and wo.shape[0] == n_f * tf
    assert d_ff_pad % MXU_N == 0 and (n_f - 1) * tf < d_ff_pad <= n_f * tf
    last_cols = d_ff_pad - (n_f - 1) * tf
    assert (mix is None) == (mix_gate_row is None) and n_f >= 2
    assert (w_aux is not None) == (epilogue == "prenorm")
    tiles_per_batch = tokens_per_batch // tm
    row_spec = pl.BlockSpec((tm, d), lambda i, f: (i, 0))
    vec_spec = pl.BlockSpec((1, d), lambda i, f: (0, 0))
    in_specs = [
        row_spec,
        pl.BlockSpec((None, N_MOD, d), lambda i, f: (i // tiles_per_batch, 0, 0)),
        vec_spec,
        vec_spec,
        pl.BlockSpec((None, d, 2 * tf), lambda i, f: (f, 0, 0)),
        pl.BlockSpec((tf, d), lambda i, f: (f, 0)),
    ]
    operands = [x2d, mod, nw, nw2, wab, wo]
    scratch = [pltpu.VMEM((tm, d), BF16)]
    if mix is not None:
        oa, ob, wa, wb = mix
        in_specs += [
            pl.BlockSpec((tm, oa.shape[1]), lambda i, f: (i, 0)),
            pl.BlockSpec((tm, ob.shape[1]), lambda i, f: (i, 0)),
            pl.BlockSpec(wa.shape, lambda i, f: (0, 0), pipeline_mode=pl.Buffered(1)),
            pl.BlockSpec(wb.shape, lambda i, f: (0, 0), pipeline_mode=pl.Buffered(1)),
        ]
        operands += [oa, ob, wa, wb]
        scratch.append(pltpu.VMEM((tm, d), F32))
    out_shape = [jax.ShapeDtypeStruct((m, d), F32)]
    out_specs = [row_spec]
    if epilogue == "prenorm":
        in_specs.append(pl.BlockSpec(w_aux.shape, lambda i, f: (0, 0)))
        operands.append(w_aux)
        out_shape += [jax.ShapeDtypeStruct((m, d), BF16),
                      jax.ShapeDtypeStruct((m, w_aux.shape[1]), BF16)]
        out_specs += [row_spec, pl.BlockSpec((tm, w_aux.shape[1]), lambda i, f: (i, 0))]
    return pl.pallas_call(
        functools.partial(_ffn_kernel, mod_base=mod_base, epilogue=epilogue,
                          mix_gate_row=mix_gate_row, last_cols=last_cols),
        grid=(m // tm, n_f),
        in_specs=in_specs,
        out_specs=out_specs,
        out_shape=out_shape,
        scratch_shapes=scratch,
        compiler_params=pltpu.CompilerParams(
            dimension_semantics=("arbitrary", "arbitrary"),
            vmem_limit_bytes=VMEM_LIMIT_BYTES),
        name="ffn_" + epilogue,
    )(*operands)


_LEVEL_HALVES = tuple(CHUNK >> (j + 1) for j in range(CHUNK.bit_length() - 1))
_N_LEVELS = len(_LEVEL_HALVES)
_CS_BLOCKS = _N_LEVELS + 2
_GROUP = 4
_GROUP_ROWS = _GROUP * CHUNK


def _chunk_constants():
    t = np.arange(CHUNK)
    tri = (t[None, :] <= t[:, None]).astype(np.float32)
    blocks = [tri]
    masks = []
    for half in _LEVEL_HALVES:
        ref = (t // (2 * half)) * (2 * half) + half
        if half < SUBLANES // 2:
            blocks.append(tri - tri[ref])
        same_block = (t[:, None] // (2 * half)) == (t[None, :] // (2 * half))
        is_query = (t % (2 * half)) >= half
        masks.append((same_block & is_query[:, None] & ~is_query[None, :]).astype(np.float32))
    masks.append(np.eye(CHUNK, dtype=np.float32))
    cm = np.concatenate(blocks, axis=0)
    cm2 = np.concatenate([cm, cm], axis=1)
    group_masks = np.stack([np.kron(np.eye(_GROUP, dtype=np.float32), m) for m in masks], axis=0)
    return jnp.asarray(cm2, dtype=BF16), jnp.asarray(group_masks, dtype=F32)


def _mixer_kernel(*refs, tiles_per_head, side_job, **static):
    *io_refs, z_a, z_b, q_scr, k_scr, cs_scr, oi_scr, u_scr, s_scr = refs
    scratch = (q_scr, k_scr, cs_scr, oi_scr, u_scr, s_scr)
    s = pl.program_id(0)
    if side_job is not None:
        *io_refs, side_in, o_ref, side_out = io_refs
        io_refs.append(o_ref)
        side_job(s, side_in, side_out)

    @pl.when(s == 0)
    def _():
        z_b[...] = jnp.zeros_like(z_b)

    @pl.when(lax.rem(jnp.maximum(s - 1, 0), tiles_per_head) == 0)
    def _():
        s_scr[...] = jnp.zeros_like(s_scr)

    @pl.when(lax.rem(s, 2) == 0)
    def _():
        _mixer_tile(io_refs, scratch, z_b, z_a, **static)

    @pl.when(lax.rem(s, 2) == 1)
    def _():
        _mixer_tile(io_refs, scratch, z_a, z_b, **static)


def _mixer_tile(io_refs, scratch, z_scr, z_next, *, kind, layer, dk, dv, tc):
    if kind == "gla":
        h_ref, w_ref, gr_ref, w2_ref, b2_ref, nw_ref, cm_ref, mk_ref, o_ref = io_refs
    else:
        h_ref, w_ref, lb_ref, nw_ref, cm_ref, mk_ref, o_ref = io_refs
    q_scr, k_scr, cs_scr, oi_scr, u_scr, s_scr = scratch
    n_chunks = tc // CHUNK
    v_cols = slice(2 * dk, 2 * dk + dv)
    gate_cols = slice(2 * dk + dv, 2 * dk + 2 * dv)

    ncols = w_ref.shape[1]
    pieces = [slice(c0, min(c0 + MXU_N, ncols)) for c0 in range(0, ncols, MXU_N)]
    n_slots = n_chunks // _GROUP + 2

    def project_pieces(slot):
        for p, cols in enumerate(pieces):
            if p * n_slots // len(pieces) == slot:
                z_next[:, cols] = jnp.dot(h_ref[...], w_ref[:, cols], preferred_element_type=F32)

    project_pieces(0)

    if kind == "gla":
        q_scr[...] = z_scr[:, 0:dk] * (dk ** -0.5)
        k_scr[...] = z_scr[:, dk:2 * dk]
        gp = jnp.dot(gr_ref[...], w2_ref[...], preferred_element_type=F32) + b2_ref[...]
        la = (jnp.minimum(gp, 0.0) - jnp.log(1.0 + jnp.exp(-jnp.abs(gp)))) * (
            1.0 / GLA_GATE_NORMALIZER)
    else:
        raw = lb_ref[...]
        ex = jnp.exp(raw - jnp.max(raw, axis=0, keepdims=True))
        p = ex / jnp.sum(ex, axis=0, keepdims=True)
        lb = jnp.sum(p[0:layer + 1, :], axis=0, keepdims=True)
        hq = z_scr[:, 0:dk]
        fr = z_scr[:, dk:2 * dk]
        q_scr[...] = hq * _sigmoid(hq)
        en = jnp.exp(-jnp.abs(fr))
        one_en = 1.0 + en
        log_sig = jnp.minimum(fr, 0.0) - jnp.log(one_en)
        sig_neg = jnp.where(fr >= 0.0, en, 1.0) / one_en
        la_a = jnp.log(lb)
        la_b = jnp.log(1.0 - lb) + log_sig
        la = jnp.maximum(la_a, la_b) + jnp.log(1.0 + jnp.exp(-jnp.abs(la_a - la_b)))
        k_scr[...] = (1.0 - lb) * sig_neg

    la_wide = jnp.concatenate([la[c * CHUNK:(c + 1) * CHUNK] for c in range(n_chunks)], axis=1)
    la_hi = la_wide.astype(BF16)
    la_lo = (la_wide - la_hi.astype(F32)).astype(BF16)
    cs_mxu = jnp.dot(cm_ref[...], jnp.concatenate([la_hi, la_lo], axis=0),
                     preferred_element_type=F32)
    cs_scr[0:CHUNK, :] = cs_mxu[0:CHUNK]
    mxu_block = 1
    for j, half in enumerate(_LEVEL_HALVES):
        lo = (j + 1) * CHUNK
        if half < SUBLANES // 2:
            cs_scr[lo:lo + CHUNK, :] = cs_mxu[mxu_block * CHUNK:(mxu_block + 1) * CHUNK]
            mxu_block += 1
            continue
        for t0 in range(0, CHUNK, 2 * half):
            cs_scr[lo + t0:lo + t0 + 2 * half, :] = (
                cs_scr[t0:t0 + 2 * half, :] - cs_scr[t0 + half:t0 + half + 1, :])
    lo = (_N_LEVELS + 1) * CHUNK
    cs_scr[lo:lo + CHUNK, :] = cs_scr[CHUNK - 1:CHUNK, :] - cs_scr[0:CHUNK, :]

    def cs_block(block, c):
        return cs_scr[block * CHUNK:(block + 1) * CHUNK, c * dk:(c + 1) * dk]

    for g in range(n_chunks // _GROUP):
        project_pieces(g + 1)
        chunks = range(g * _GROUP, (g + 1) * _GROUP)
        rows = slice(g * _GROUP_ROWS, (g + 1) * _GROUP_ROWS)
        q = q_scr[rows, :]
        k = k_scr[rows, :]
        attn = mk_ref[_N_LEVELS] * jnp.sum(q * k, axis=-1, keepdims=True)
        for j in range(_N_LEVELS):
            d = jnp.concatenate([cs_block(j + 1, c) for c in chunks], axis=0)
            e = jnp.exp(-jnp.abs(d))
            s = lax.dot_general((q * e).astype(BF16), (k * e).astype(BF16), _NT,
                                preferred_element_type=F32)
            attn = attn + mk_ref[j] * s
        vb = z_scr[rows, v_cols].astype(BF16)
        oi_scr[rows, :] = jnp.dot(attn.astype(BF16), vb, preferred_element_type=F32)
        for i, c in enumerate(chunks):
            crow = slice(c * CHUNK, (c + 1) * CHUNK)
            kd = (k_scr[crow, :] * jnp.exp(cs_block(_N_LEVELS + 1, c))).astype(BF16)
            u_scr[c] = lax.dot_general(vb[i * CHUNK:(i + 1) * CHUNK], kd, _TN,
                                       preferred_element_type=F32)

    project_pieces(n_slots - 1)
    st = s_scr[...]
    for c in range(n_chunks):
        rows = slice(c * CHUNK, (c + 1) * CHUNK)
        b = cs_block(0, c)
        qb = (q_scr[rows, :] * jnp.exp(b)).astype(BF16)
        o = lax.dot_general(qb, st.astype(BF16), _NT, preferred_element_type=F32) + oi_scr[rows, :]
        st = st * jnp.exp(b[CHUNK - 1:CHUNK, :]) + u_scr[c]
        gate = z_scr[rows, gate_cols]
        o_ref[rows, :] = (_rms(o) * nw_ref[...] * (gate * _sigmoid(gate))).astype(BF16)
    s_scr[...] = st


def _side_wi(s, w_ref, o_ref, *, n_slabs, d_ff, n_f, tf):
    @pl.when(s < n_slabs)
    def _():
        _wi_prep_kernel(w_ref, o_ref, d_ff=d_ff, n_f=n_f, tf=tf)


def _side_wo(s, w_ref, o_ref, *, n_src, n_out):
    @pl.when(s < n_src)
    def _():
        o_ref[...] = w_ref[...].astype(BF16)

    @pl.when(jnp.logical_and(s >= n_src, s < n_out))
    def _():
        o_ref[...] = jnp.zeros_like(o_ref)


def _side_job_specs(side, n_steps):
    what, w = side
    if what == "wi":
        d, d_ff = w.shape[0], w.shape[1] // 2
        n_f = -(-d_ff // FFN_TF)
        tf = FFN_TF
        n_slabs = d // SIDE_ROWS
        assert d % SIDE_ROWS == 0 and n_slabs <= n_steps and d_ff % LANES == 0
        slab = lambda s: jnp.minimum(s, n_slabs - 1)
        return (functools.partial(_side_wi, n_slabs=n_slabs, d_ff=d_ff, n_f=n_f, tf=tf),
                pl.BlockSpec((SIDE_ROWS, 2 * d_ff), lambda s: (slab(s), 0)),
                pl.BlockSpec((n_f, SIDE_ROWS, 2 * tf), lambda s: (0, slab(s), 0)),
                jax.ShapeDtypeStruct((n_f, d, 2 * tf), BF16), w)
    assert what == "wo"
    d_ff, d = w.shape
    n_src = d_ff // LANES
    n_out = -(-d_ff // FFN_TF) * FFN_TF // LANES
    assert d_ff % LANES == 0 and n_out <= n_steps
    return (functools.partial(_side_wo, n_src=n_src, n_out=n_out),
            pl.BlockSpec((LANES, d), lambda s: (jnp.minimum(s, n_src - 1), 0)),
            pl.BlockSpec((LANES, d), lambda s: (jnp.minimum(s, n_out - 1), 0)),
            jax.ShapeDtypeStruct((n_out * LANES, d), BF16), w)


def _mixer(h2, w_heads, extra, nw, consts, *, kind, layer, batch, tokens_per_batch, dk, dv,
           side=None):
    m, d = h2.shape
    n_heads, _, ncols = w_heads.shape
    tc = MIX_TC
    assert tokens_per_batch % tc == 0 and tc % _GROUP_ROWS == 0
    nt = tokens_per_batch // tc
    n_tiles = batch * n_heads * nt
    cm, mk = consts

    def coords(tile):
        bb = tile // (n_heads * nt)
        hh = lax.rem(tile // nt, n_heads)
        return bb * nt + lax.rem(tile, nt), hh

    def projected(s):
        return coords(jnp.minimum(s, n_tiles - 1))

    def consumed(s):
        return coords(jnp.maximum(s - 1, 0))

    const2 = lambda s: (0, 0)
    in_specs = [
        pl.BlockSpec((tc, d), lambda s: (projected(s)[0], 0)),
        pl.BlockSpec((None, d, ncols), lambda s: (projected(s)[1], 0, 0)),
    ]
    if kind == "gla":
        gr, w2, b2 = extra
        in_specs += [
            pl.BlockSpec((tc, LANES), lambda s: (consumed(s)[0], 0)),
            pl.BlockSpec((None, LANES, dk), lambda s: (consumed(s)[1], 0, 0)),
            pl.BlockSpec((None, 1, dk), lambda s: (consumed(s)[1], 0, 0)),
        ]
    else:
        (lbraw,) = extra
        in_specs += [pl.BlockSpec((None, lbraw.shape[1], dk), lambda s: (consumed(s)[1], 0, 0))]
    in_specs += [
        pl.BlockSpec((1, dv), const2),
        pl.BlockSpec(cm.shape, const2),
        pl.BlockSpec(mk.shape, lambda s: (0, 0, 0)),
    ]
    operands = [h2, w_heads, *extra, nw, cm, mk]
    out_specs = [pl.BlockSpec((tc, dv), lambda s: consumed(s))]
    out_shape = [jax.ShapeDtypeStruct((m, n_heads * dv), BF16)]
    side_job = None
    if side is not None:
        side_job, side_in_spec, side_out_spec, side_shape, side_operand = _side_job_specs(
            side, n_tiles + 1)
        in_specs.append(side_in_spec)
        operands.append(side_operand)
        out_specs.append(side_out_spec)
        out_shape.append(side_shape)
    return pl.pallas_call(
        functools.partial(_mixer_kernel, tiles_per_head=nt, side_job=side_job, kind=kind,
                          layer=layer, dk=dk, dv=dv, tc=tc),
        grid=(n_tiles + 1,),
        in_specs=in_specs,
        out_specs=out_specs,
        out_shape=out_shape,
        scratch_shapes=[
            pltpu.VMEM((tc, ncols), F32),
            pltpu.VMEM((tc, ncols), F32),
            pltpu.VMEM((tc, dk), F32),
            pltpu.VMEM((tc, dk), F32),
            pltpu.VMEM((_CS_BLOCKS * CHUNK, (tc // CHUNK) * dk), F32),
            pltpu.VMEM((tc, dv), F32),
            pltpu.VMEM((tc // CHUNK, dv, dk), F32),
            pltpu.VMEM((dv, dk), F32),
        ],
        compiler_params=pltpu.CompilerParams(
            dimension_semantics=("arbitrary",), vmem_limit_bytes=VMEM_LIMIT_BYTES),
        name="mixer_" + kind,
    )(*operands)


def _wi_prep_kernel(w_ref, o_ref, *, d_ff, n_f, tf):
    rows = w_ref.shape[0]
    for j in range(n_f):
        for i in range(tf // MXU_N):
            c0 = j * tf + i * MXU_N
            valid = max(0, min(MXU_N, d_ff - c0))
            for half in range(2):
                src = half * d_ff + c0
                parts = []
                if valid:
                    parts.append(w_ref[:, src:src + valid].astype(BF16))
                if valid < MXU_N:
                    parts.append(jnp.zeros((rows, MXU_N - valid), BF16))
                piece = parts[0] if len(parts) == 1 else jnp.concatenate(parts, axis=1)
                o_ref[j, :, (2 * i + half) * MXU_N:(2 * i + half + 1) * MXU_N] = piece


def _wo_prep_kernel(w_ref, *rest, n_full):
    *tail_refs, o_ref = rest
    r = pl.program_id(0)

    @pl.when(r < n_full)
    def _():
        o_ref[...] = w_ref[...].astype(BF16)

    @pl.when(r >= n_full)
    def _():
        rows = 0
        for t_ref in tail_refs:
            o_ref[rows:rows + LANES, :] = t_ref[...].astype(BF16)
            rows += LANES
        o_ref[rows:, :] = jnp.zeros((o_ref.shape[0] - rows, o_ref.shape[1]), BF16)


def _prep_ffn_weights(wi, wo, tf):
    d, d_ff = wi.shape[0], wo.shape[0]
    assert d_ff % LANES == 0 and d % WPREP_ROWS == 0
    n_f = -(-d_ff // tf)
    wab = pl.pallas_call(
        functools.partial(_wi_prep_kernel, d_ff=d_ff, n_f=n_f, tf=tf),
        grid=(d // WPREP_ROWS,),
        in_specs=[pl.BlockSpec((WPREP_ROWS, 2 * d_ff), lambda r: (r, 0))],
        out_specs=pl.BlockSpec((n_f, WPREP_ROWS, 2 * tf), lambda r: (0, r, 0)),
        out_shape=jax.ShapeDtypeStruct((n_f, d, 2 * tf), BF16),
        compiler_params=pltpu.CompilerParams(
            dimension_semantics=("arbitrary",), vmem_limit_bytes=VMEM_LIMIT_BYTES),
        name="wi_prep",
    )(wi)
    n_full = d_ff // tf
    n_tail = (d_ff - n_full * tf) // LANES
    tail0 = n_full * tf // LANES
    tail_specs = [pl.BlockSpec((LANES, d), functools.partial(lambda r, k: (tail0 + k, 0), k=k))
                  for k in range(n_tail)]
    wob = pl.pallas_call(
        functools.partial(_wo_prep_kernel, n_full=n_full),
        grid=(n_f,),
        in_specs=[pl.BlockSpec((tf, d), lambda r: (jnp.minimum(r, n_full - 1), 0))]
        + tail_specs,
        out_specs=pl.BlockSpec((tf, d), lambda r: (r, 0)),
        out_shape=jax.ShapeDtypeStruct((n_f * tf, d), BF16),
        compiler_params=pltpu.CompilerParams(
            dimension_semantics=("arbitrary",), vmem_limit_bytes=VMEM_LIMIT_BYTES),
        name="wo_prep",
    )(wo, *([wo] * n_tail))
    return wab, wob


_GLA_QK = GLA_HEADS * GLA_DK
_GLA_V = GLA_HEADS * GLA_DV
_HGRN_K = HGRN_HEADS * HGRN_DK
_HGRN_V = HGRN_HEADS * HGRN_DV
_IN_OFFS = tuple(int(o) for o in np.cumsum(
    [0, _GLA_QK, _GLA_QK, _GLA_V, _GLA_V, GLA_GATE_RANK, _HGRN_K, _HGRN_K, _HGRN_V, _HGRN_V]))


def _win_prep_kernel(wt_ref, og_ref, oh_ref, ogr_ref):
    def cols(part, h, width):
        f0 = _IN_OFFS[part] + h * width
        return wt_ref[f0:f0 + width, :].T.astype(BF16)

    gr = wt_ref[_IN_OFFS[4]:_IN_OFFS[5], :]
    gr = jnp.concatenate([gr, jnp.zeros((LANES - GLA_GATE_RANK, gr.shape[1]), F32)], axis=0)
    ogr_ref[...] = gr.T.astype(BF16)
    for h in range(GLA_HEADS):
        og_ref[h] = jnp.concatenate(
            [cols(0, h, GLA_DK), cols(1, h, GLA_DK), cols(2, h, GLA_DV), cols(3, h, GLA_DV)],
            axis=1)
    for h in range(HGRN_HEADS):
        oh_ref[h] = jnp.concatenate(
            [cols(5, h, HGRN_DK), cols(6, h, HGRN_DK), cols(7, h, HGRN_DV), cols(8, h, HGRN_DV)],
            axis=1)


def _prep_mixer_weights(w_in, layer, w2, b2):
    _, d, in_width = w_in.shape
    assert in_width == _IN_OFFS[-1] and d % WPREP_ROWS == 0
    gla_cols = 2 * GLA_DK + 2 * GLA_DV
    hgrn_cols = 2 * HGRN_DK + 2 * HGRN_DV
    wt = jnp.swapaxes(w_in, 1, 2)
    w_gla, w_hg, w_gr = pl.pallas_call(
        _win_prep_kernel,
        grid=(d // WPREP_ROWS,),
        in_specs=[pl.BlockSpec((None, in_width, WPREP_ROWS), lambda r: (layer, 0, r))],
        out_specs=[pl.BlockSpec((GLA_HEADS, WPREP_ROWS, gla_cols), lambda r: (0, r, 0)),
                   pl.BlockSpec((HGRN_HEADS, WPREP_ROWS, hgrn_cols), lambda r: (0, r, 0)),
                   pl.BlockSpec((WPREP_ROWS, LANES), lambda r: (r, 0))],
        out_shape=[jax.ShapeDtypeStruct((GLA_HEADS, d, gla_cols), BF16),
                   jax.ShapeDtypeStruct((HGRN_HEADS, d, hgrn_cols), BF16),
                   jax.ShapeDtypeStruct((d, LANES), BF16)],
        compiler_params=pltpu.CompilerParams(
            dimension_semantics=("arbitrary",), vmem_limit_bytes=VMEM_LIMIT_BYTES),
        name="win_prep",
    )(wt)
    w2h = jnp.pad(w2.astype(BF16), ((0, LANES - GLA_GATE_RANK), (0, 0)))
    w2h = w2h.reshape(LANES, GLA_HEADS, GLA_DK).transpose(1, 0, 2)
    b2h = b2.reshape(GLA_HEADS, 1, GLA_DK)
    return w_gla, w_hg, w_gr, w2h, b2h


def kernel(x, c, ada_w, ada_b, norm_ffn1_w, ffn1_wi, ffn1_wo, norm_mix_w, w_in, gla_gate_w2,
           gla_gate_b2, gla_norm_w, hgrn_norm_w, hgrn_lower_bounds, w_out, norm_ffn2_w, ffn2_wi,
           ffn2_wo, final_norm_w):
    batch, seq, d = x.shape
    depth = ada_w.shape[0]
    m = batch * seq
    consts = _chunk_constants()
    xc = x.reshape(m, d)
    c_pad = jnp.pad(c, ((0, SUBLANES - batch % SUBLANES), (0, 0))) if batch % SUBLANES else c
    gla_v = GLA_HEADS * GLA_DV
    lb_heads = hgrn_lower_bounds.astype(F32).reshape(depth + 1, HGRN_HEADS, HGRN_DK).transpose(1, 0, 2)

    for l in range(depth):
        mod = _adaln(c_pad, ada_w[l], ada_b[l][None, :])[:batch].reshape(batch, N_MOD, d)
        d_ff_pad = -(-ffn1_wo.shape[1] // FFN_PAD) * FFN_PAD
        wab1, wo1 = _prep_ffn_weights(ffn1_wi[l], ffn1_wo[l], FFN_TF_WIDE)
        w_gla, w_hg, w_gr, w2h, b2h = _prep_mixer_weights(w_in, l, gla_gate_w2[l], gla_gate_b2[l])
        wout = w_out[l].astype(BF16)

        x1, h2, gr = _ffn(xc, mod, norm_ffn1_w[l][None, :], norm_mix_w[l][None, :], wab1, wo1,
                          d_ff_pad=d_ff_pad, tokens_per_batch=seq, mod_base=0, epilogue="prenorm",
                          w_aux=w_gr)
        o_gla, wo2 = _mixer(h2, w_gla, (gr, w2h, b2h), gla_norm_w[l][None, :], consts,
                            kind="gla", layer=l, batch=batch, tokens_per_batch=seq, dk=GLA_DK,
                            dv=GLA_DV, side=("wo", ffn2_wo[l]))
        o_hg, wab2 = _mixer(h2, w_hg, (lb_heads,), hgrn_norm_w[l][None, :], consts,
                            kind="hgrn", layer=l, batch=batch, tokens_per_batch=seq, dk=HGRN_DK,
                            dv=HGRN_DV, side=("wi", ffn2_wi[l]))
        last = l == depth - 1
        nw2 = final_norm_w[None, :] if last else norm_ffn2_w[l][None, :]
        (xc,) = _ffn(x1, mod, norm_ffn2_w[l][None, :], nw2, wab2, wo2, d_ff_pad=d_ff_pad,
                     tokens_per_batch=seq, mod_base=6, epilogue="final" if last else "none",
                     mix=(o_gla, o_hg, wout[:gla_v], wout[gla_v:]), mix_gate_row=5)
    return xc.reshape(batch, seq, d)
```

```python
import functools

import jax
import jax.numpy as jnp
import numpy as np
from jax import lax
from jax.experimental import pallas as pl
from jax.experimental.pallas import tpu as pltpu

F32 = jnp.float32
BF16 = jnp.bfloat16

GLA_HEADS = 4
GLA_DK = 128
GLA_DV = 256
GLA_GATE_RANK = 16
GLA_GATE_NORMALIZER = 16.0
HGRN_HEADS = 8
HGRN_DK = 128
HGRN_DV = 128
CHUNK = 64
MACARON_W = 0.5
N_MOD = 9
EPS = 1e-6

LANES = 128
SUBLANES = 8
MXU_N = 256
VMEM_LIMIT_BYTES = 56 * 1024 * 1024

FFN_TM = 512
FFN_TF = 512
FFN_TF_WIDE = 1024
FFN_PAD = 512
PROLOGUE_ROWS = 256
NORM_ROWS = 16
AUX_ROWS = 128
MIX_TC = 1024
ADALN_TN = 1024
WPREP_ROWS = 256
SIDE_ROWS = 16

_NT = (((1,), (1,)), ((), ()))
_TN = (((0,), (0,)), ((), ()))


def _sigmoid(x):
    return jax.nn.sigmoid(x)


def _rms(x):
    return x * lax.rsqrt(jnp.mean(x * x, axis=-1, keepdims=True) + EPS)


def _adaln_kernel(c_ref, w_ref, b_ref, o_ref):
    c = c_ref[...]
    ca = (c * _sigmoid(c)).astype(BF16)
    o_ref[...] = jnp.dot(ca, w_ref[...].astype(BF16), preferred_element_type=F32) + b_ref[...]


def _adaln(c_pad, w, b):
    rows, d = c_pad.shape
    n = w.shape[1]
    assert n % ADALN_TN == 0
    return pl.pallas_call(
        _adaln_kernel,
        grid=(n // ADALN_TN,),
        in_specs=[
            pl.BlockSpec((rows, d), lambda j: (0, 0)),
            pl.BlockSpec((d, ADALN_TN), lambda j: (0, j)),
            pl.BlockSpec((1, ADALN_TN), lambda j: (0, j)),
        ],
        out_specs=pl.BlockSpec((rows, ADALN_TN), lambda j: (0, j)),
        out_shape=jax.ShapeDtypeStruct((rows, n), F32),
        compiler_params=pltpu.CompilerParams(
            dimension_semantics=("arbitrary",), vmem_limit_bytes=VMEM_LIMIT_BYTES),
        name="adaln",
    )(c_pad, w, b)


def _ffn_kernel(*refs, mod_base, epilogue, mix_gate_row, last_cols):
    x_ref, mod_ref, nw_ref, nw2_ref, wab_ref, wo_ref, *rest = refs
    if mix_gate_row is not None:
        oa_ref, ob_ref, wa_ref, wb_ref, *rest = rest
    if epilogue == "prenorm":
        waux_ref, *rest = rest
        xo_ref, ho_ref, aux_ref, h_scr, *rest = rest
    else:
        xo_ref, h_scr, *rest = rest
    xin_ref = rest[0] if mix_gate_row is not None else x_ref
    f = pl.program_id(1)
    last = pl.num_programs(1) - 1
    tm = x_ref.shape[0]

    def prologue():
        shift = mod_ref[mod_base:mod_base + 1, :]
        gain = nw_ref[...] * (1.0 + mod_ref[mod_base + 1:mod_base + 2, :])
        for r0 in range(0, tm, PROLOGUE_ROWS):
            rows = slice(r0, r0 + PROLOGUE_ROWS)
            if mix_gate_row is not None:
                y = (jnp.dot(oa_ref[rows, :], wa_ref[...], preferred_element_type=F32)
                     + jnp.dot(ob_ref[rows, :], wb_ref[...], preferred_element_type=F32))
                xin_ref[rows, :] = x_ref[rows, :] + mod_ref[mix_gate_row:mix_gate_row + 1, :] * y
            for c0 in range(r0, r0 + PROLOGUE_ROWS, NORM_ROWS):
                chunk = slice(c0, c0 + NORM_ROWS)
                h_scr[chunk, :] = (_rms(xin_ref[chunk, :]) * gain + shift).astype(BF16)

    def swiglu_step(first, cols=None):
        cols = wo_ref.shape[0] if cols is None else cols
        zab = jnp.dot(h_scr[...], wab_ref[:, :2 * cols], preferred_element_type=F32)
        pieces = cols // MXU_N
        a = jnp.concatenate([zab[:, (2 * i) * MXU_N:(2 * i + 1) * MXU_N] for i in range(pieces)],
                            axis=1)
        b = jnp.concatenate([zab[:, (2 * i + 1) * MXU_N:(2 * i + 2) * MXU_N]
                             for i in range(pieces)], axis=1)
        act = (a * _sigmoid(a) * b).astype(BF16)
        update = jnp.dot(act, wo_ref[:cols, :], preferred_element_type=F32)
        if first:
            xo_ref[...] = update
        else:
            xo_ref[...] += update

    def finish():
        gate = MACARON_W * mod_ref[mod_base + 2:mod_base + 3, :]
        if epilogue == "prenorm":
            shift2 = mod_ref[mod_base + 3:mod_base + 4, :]
            gain2 = nw2_ref[...] * (1.0 + mod_ref[mod_base + 4:mod_base + 5, :])
        for c0 in range(0, tm, NORM_ROWS):
            chunk = slice(c0, c0 + NORM_ROWS)
            xn = xin_ref[chunk, :] + gate * xo_ref[chunk, :]
            if epilogue == "prenorm":
                xo_ref[chunk, :] = xn
                ho_ref[chunk, :] = (_rms(xn) * gain2 + shift2).astype(BF16)
            elif epilogue == "final":
                xo_ref[chunk, :] = _rms(xn) * nw2_ref[...]
            else:
                xo_ref[chunk, :] = xn
            done = c0 + NORM_ROWS
            if epilogue == "prenorm" and done % AUX_ROWS == 0:
                rows = slice(done - AUX_ROWS, done)
                aux_ref[rows, :] = jnp.dot(ho_ref[rows, :], waux_ref[...],
                                           preferred_element_type=F32).astype(BF16)

    @pl.when(f == 0)
    def _():
        prologue()
        swiglu_step(first=True)

    @pl.when(jnp.logical_and(f > 0, f < last))
    def _():
        swiglu_step(first=False)

    @pl.when(f == last)
    def _():
        swiglu_step(first=False, cols=last_cols)
        finish()


def _ffn(x2d, mod, nw, nw2, wab, wo, *, d_ff_pad, tokens_per_batch, mod_base, epilogue, mix=None,
         mix_gate_row=None, w_aux=None):
    m, d = x2d.shape
    n_f, _, tf2 = wab.shape
    tf = tf2 // 2
    tm = FFN_TM
    assert m % tm == 0 and tokens_per_batch % tm == 0 and wo.shape[0] == n_f * tf
    assert d_ff_pad % MXU_N == 0 and (n_f - 1) * tf < d_ff_pad <= n_f * tf
    last_cols = d_ff_pad - (n_f - 1) * tf
    assert (mix is None) == (mix_gate_row is None) and n_f >= 2
    assert (w_aux is not None) == (epilogue == "prenorm")
    tiles_per_batch = tokens_per_batch // tm
    row_spec = pl.BlockSpec((tm, d), lambda i, f: (i, 0))
    vec_spec = pl.BlockSpec((1, d), lambda i, f: (0, 0))
    in_specs = [
        row_spec,
        pl.BlockSpec((None, N_MOD, d), lambda i, f: (i // tiles_per_batch, 0, 0)),
        vec_spec,
        vec_spec,
        pl.BlockSpec((None, d, 2 * tf), lambda i, f: (f, 0, 0)),
        pl.BlockSpec((tf, d), lambda i, f: (f, 0)),
    ]
    operands = [x2d, mod, nw, nw2, wab, wo]
    scratch = [pltpu.VMEM((tm, d), BF16)]
    if mix is not None:
        oa, ob, wa, wb = mix
        in_specs += [
            pl.BlockSpec((tm, oa.shape[1]), lambda i, f: (i, 0)),
            pl.BlockSpec((tm, ob.shape[1]), lambda i, f: (i, 0)),
            pl.BlockSpec(wa.shape, lambda i, f: (0, 0), pipeline_mode=pl.Buffered(1)),
            pl.BlockSpec(wb.shape, lambda i, f: (0, 0), pipeline_mode=pl.Buffered(1)),
        ]
        operands += [oa, ob, wa, wb]
        scratch.append(pltpu.VMEM((tm, d), F32))
    out_shape = [jax.ShapeDtypeStruct((m, d), F32)]
    out_specs = [row_spec]
    if epilogue == "prenorm":
        in_specs.append(pl.BlockSpec(w_aux.shape, lambda i, f: (0, 0)))
        operands.append(w_aux)
        out_shape += [jax.ShapeDtypeStruct((m, d), BF16),
                      jax.ShapeDtypeStruct((m, w_aux.shape[1]), BF16)]
        out_specs += [row_spec, pl.BlockSpec((tm, w_aux.shape[1]), lambda i, f: (i, 0))]
    return pl.pallas_call(
        functools.partial(_ffn_kernel, mod_base=mod_base, epilogue=epilogue,
                          mix_gate_row=mix_gate_row, last_cols=last_cols),
        grid=(m // tm, n_f),
        in_specs=in_specs,
        out_specs=out_specs,
        out_shape=out_shape,
        scratch_shapes=scratch,
        compiler_params=pltpu.CompilerParams(
            dimension_semantics=("arbitrary", "arbitrary"),
            vmem_limit_bytes=VMEM_LIMIT_BYTES),
        name="ffn_" + epilogue,
    )(*operands)


_LEVEL_HALVES = tuple(CHUNK >> (j + 1) for j in range(CHUNK.bit_length() - 1))
_N_LEVELS = len(_LEVEL_HALVES)
_CS_BLOCKS = _N_LEVELS + 2
_GROUP = 4
_GROUP_ROWS = _GROUP * CHUNK


def _chunk_constants():
    t = np.arange(CHUNK)
    tri = (t[None, :] <= t[:, None]).astype(np.float32)
    blocks = [tri]
    masks = []
    for half in _LEVEL_HALVES:
        ref = (t // (2 * half)) * (2 * half) + half
        if half < SUBLANES // 2:
            blocks.append(tri - tri[ref])
        same_block = (t[:, None] // (2 * half)) == (t[None, :] // (2 * half))
        is_query = (t % (2 * half)) >= half
        masks.append((same_block & is_query[:, None] & ~is_query[None, :]).astype(np.float32))
    masks.append(np.eye(CHUNK, dtype=np.float32))
    cm = np.concatenate(blocks, axis=0)
    cm2 = np.concatenate([cm, cm], axis=1)
    per_half = LANES // CHUNK
    group_masks = np.stack([np.kron(np.eye(per_half, dtype=np.float32), m) for m in masks], axis=0)
    return jnp.asarray(cm2, dtype=BF16), jnp.asarray(group_masks, dtype=F32)


def _mixer_kernel(*refs, tiles_per_head, side_job, **static):
    *io_refs, z_a, z_b, q_scr, k_scr, cs_scr, oi_scr, u_scr, s_scr = refs
    scratch = (q_scr, k_scr, cs_scr, oi_scr, u_scr, s_scr)
    s = pl.program_id(0)
    if side_job is not None:
        *io_refs, side_in, o_ref, side_out = io_refs
        io_refs.append(o_ref)
        side_job(s, side_in, side_out)

    @pl.when(s == 0)
    def _():
        z_b[...] = jnp.zeros_like(z_b)

    @pl.when(lax.rem(jnp.maximum(s - 1, 0), tiles_per_head) == 0)
    def _():
        s_scr[...] = jnp.zeros_like(s_scr)

    @pl.when(lax.rem(s, 2) == 0)
    def _():
        _mixer_tile(io_refs, scratch, z_b, z_a, **static)

    @pl.when(lax.rem(s, 2) == 1)
    def _():
        _mixer_tile(io_refs, scratch, z_a, z_b, **static)


def _mixer_tile(io_refs, scratch, z_scr, z_next, *, kind, layer, dk, dv, tc):
    if kind == "gla":
        h_ref, w_ref, gr_ref, w2_ref, b2_ref, nw_ref, cm_ref, mk_ref, o_ref = io_refs
    else:
        h_ref, w_ref, lb_ref, nw_ref, cm_ref, mk_ref, o_ref = io_refs
    q_scr, k_scr, cs_scr, oi_scr, u_scr, s_scr = scratch
    n_chunks = tc // CHUNK
    v_cols = slice(2 * dk, 2 * dk + dv)
    gate_cols = slice(2 * dk + dv, 2 * dk + 2 * dv)

    ncols = w_ref.shape[1]
    pieces = [slice(c0, min(c0 + MXU_N, ncols)) for c0 in range(0, ncols, MXU_N)]
    n_slots = n_chunks // _GROUP + 2

    def project_pieces(slot):
        for p, cols in enumerate(pieces):
            if p * n_slots // len(pieces) == slot:
                z_next[:, cols] = jnp.dot(h_ref[...], w_ref[:, cols], preferred_element_type=F32)

    project_pieces(0)

    if kind == "gla":
        q_scr[...] = z_scr[:, 0:dk] * (dk ** -0.5)
        k_scr[...] = z_scr[:, dk:2 * dk]
        gp = jnp.dot(gr_ref[...], w2_ref[...], preferred_element_type=F32) + b2_ref[...]
        la = (jnp.minimum(gp, 0.0) - jnp.log(1.0 + jnp.exp(-jnp.abs(gp)))) * (
            1.0 / GLA_GATE_NORMALIZER)
    else:
        raw = lb_ref[...]
        ex = jnp.exp(raw - jnp.max(raw, axis=0, keepdims=True))
        p = ex / jnp.sum(ex, axis=0, keepdims=True)
        lb = jnp.sum(p[0:layer + 1, :], axis=0, keepdims=True)
        hq = z_scr[:, 0:dk]
        fr = z_scr[:, dk:2 * dk]
        q_scr[...] = hq * _sigmoid(hq)
        en = jnp.exp(-jnp.abs(fr))
        one_en = 1.0 + en
        log_sig = jnp.minimum(fr, 0.0) - jnp.log(one_en)
        sig_neg = jnp.where(fr >= 0.0, en, 1.0) / one_en
        la_a = jnp.log(lb)
        la_b = jnp.log(1.0 - lb) + log_sig
        la = jnp.maximum(la_a, la_b) + jnp.log(1.0 + jnp.exp(-jnp.abs(la_a - la_b)))
        k_scr[...] = (1.0 - lb) * sig_neg

    la_wide = jnp.concatenate([la[c * CHUNK:(c + 1) * CHUNK] for c in range(n_chunks)], axis=1)
    la_hi = la_wide.astype(BF16)
    la_lo = (la_wide - la_hi.astype(F32)).astype(BF16)
    cs_mxu = jnp.dot(cm_ref[...], jnp.concatenate([la_hi, la_lo], axis=0),
                     preferred_element_type=F32)
    cs_scr[0:CHUNK, :] = cs_mxu[0:CHUNK]
    mxu_block = 1
    for j, half in enumerate(_LEVEL_HALVES):
        lo = (j + 1) * CHUNK
        if half < SUBLANES // 2:
            cs_scr[lo:lo + CHUNK, :] = cs_mxu[mxu_block * CHUNK:(mxu_block + 1) * CHUNK]
            mxu_block += 1
            continue
        for t0 in range(0, CHUNK, 2 * half):
            cs_scr[lo + t0:lo + t0 + 2 * half, :] = (
                cs_scr[t0:t0 + 2 * half, :] - cs_scr[t0 + half:t0 + half + 1, :])
    lo = (_N_LEVELS + 1) * CHUNK
    cs_scr[lo:lo + CHUNK, :] = cs_scr[CHUNK - 1:CHUNK, :] - cs_scr[0:CHUNK, :]

    def cs_block(block, c):
        return cs_scr[block * CHUNK:(block + 1) * CHUNK, c * dk:(c + 1) * dk]

    for g in range(n_chunks // _GROUP):
        project_pieces(g + 1)
        chunks = range(g * _GROUP, (g + 1) * _GROUP)
        rows = slice(g * _GROUP_ROWS, (g + 1) * _GROUP_ROWS)
        q = q_scr[rows, :]
        k = k_scr[rows, :]
        halves = [slice(i * LANES, (i + 1) * LANES) for i in range(_GROUP_ROWS // LANES)]
        qk = jnp.sum(q * k, axis=-1, keepdims=True)
        attn = [mk_ref[_N_LEVELS] * qk[hs] for hs in halves]
        for j in range(_N_LEVELS):
            d = jnp.concatenate([cs_block(j + 1, c) for c in chunks], axis=0)
            e = jnp.exp(-jnp.abs(d))
            s = lax.dot_general((q * e).astype(BF16), (k * e).astype(BF16), _NT,
                                preferred_element_type=F32)
            attn = [a + mk_ref[j] * s[hs, hs] for a, hs in zip(attn, halves)]
        vb = z_scr[rows, v_cols].astype(BF16)
        for a, hs in zip(attn, halves):
            oi_scr[rows.start + hs.start:rows.start + hs.stop, :] = jnp.dot(
                a.astype(BF16), vb[hs], preferred_element_type=F32)
        for i, c in enumerate(chunks):
            crow = slice(c * CHUNK, (c + 1) * CHUNK)
            kd = (k_scr[crow, :] * jnp.exp(cs_block(_N_LEVELS + 1, c))).astype(BF16)
            u_scr[c] = lax.dot_general(vb[i * CHUNK:(i + 1) * CHUNK], kd, _TN,
                                       preferred_element_type=F32)

    project_pieces(n_slots - 1)
    st = s_scr[...]
    for c in range(n_chunks):
        rows = slice(c * CHUNK, (c + 1) * CHUNK)
        b = cs_block(0, c)
        qb = (q_scr[rows, :] * jnp.exp(b)).astype(BF16)
        o = lax.dot_general(qb, st.astype(BF16), _NT, preferred_element_type=F32) + oi_scr[rows, :]
        st = st * jnp.exp(b[CHUNK - 1:CHUNK, :]) + u_scr[c]
        gate = z_scr[rows, gate_cols]
        o_ref[rows, :] = (_rms(o) * nw_ref[...] * (gate * _sigmoid(gate))).astype(BF16)
    s_scr[...] = st


def _side_wi(s, w_ref, o_ref, *, n_slabs, d_ff, n_f, tf):
    @pl.when(s < n_slabs)
    def _():
        _wi_prep_kernel(w_ref, o_ref, d_ff=d_ff, n_f=n_f, tf=tf)


def _side_wo(s, w_ref, o_ref, *, n_src, n_out):
    @pl.when(s < n_src)
    def _():
        o_ref[...] = w_ref[...].astype(BF16)

    @pl.when(jnp.logical_and(s >= n_src, s < n_out))
    def _():
        o_ref[...] = jnp.zeros_like(o_ref)


def _side_job_specs(side, n_steps):
    what, w = side
    if what == "wi":
        d, d_ff = w.shape[0], w.shape[1] // 2
        n_f = -(-d_ff // FFN_TF)
        tf = FFN_TF
        n_slabs = d // SIDE_ROWS
        assert d % SIDE_ROWS == 0 and n_slabs <= n_steps and d_ff % LANES == 0
        slab = lambda s: jnp.minimum(s, n_slabs - 1)
        return (functools.partial(_side_wi, n_slabs=n_slabs, d_ff=d_ff, n_f=n_f, tf=tf),
                pl.BlockSpec((SIDE_ROWS, 2 * d_ff), lambda s: (slab(s), 0)),
                pl.BlockSpec((n_f, SIDE_ROWS, 2 * tf), lambda s: (0, slab(s), 0)),
                jax.ShapeDtypeStruct((n_f, d, 2 * tf), BF16), w)
    assert what == "wo"
    d_ff, d = w.shape
    n_src = d_ff // LANES
    n_out = -(-d_ff // FFN_TF) * FFN_TF // LANES
    assert d_ff % LANES == 0 and n_out <= n_steps
    return (functools.partial(_side_wo, n_src=n_src, n_out=n_out),
            pl.BlockSpec((LANES, d), lambda s: (jnp.minimum(s, n_src - 1), 0)),
            pl.BlockSpec((LANES, d), lambda s: (jnp.minimum(s, n_out - 1), 0)),
            jax.ShapeDtypeStruct((n_out * LANES, d), BF16), w)


def _mixer(h2, w_heads, extra, nw, consts, *, kind, layer, batch, tokens_per_batch, dk, dv,
           side=None):
    m, d = h2.shape
    n_heads, _, ncols = w_heads.shape
    tc = MIX_TC
    assert tokens_per_batch % tc == 0 and tc % _GROUP_ROWS == 0
    nt = tokens_per_batch // tc
    n_tiles = batch * n_heads * nt
    cm, mk = consts

    def coords(tile):
        bb = tile // (n_heads * nt)
        hh = lax.rem(tile // nt, n_heads)
        return bb * nt + lax.rem(tile, nt), hh

    def projected(s):
        return coords(jnp.minimum(s, n_tiles - 1))

    def consumed(s):
        return coords(jnp.maximum(s - 1, 0))

    const2 = lambda s: (0, 0)
    in_specs = [
        pl.BlockSpec((tc, d), lambda s: (projected(s)[0], 0)),
        pl.BlockSpec((None, d, ncols), lambda s: (projected(s)[1], 0, 0)),
    ]
    if kind == "gla":
        gr, w2, b2 = extra
        in_specs += [
            pl.BlockSpec((tc, LANES), lambda s: (consumed(s)[0], 0)),
            pl.BlockSpec((None, LANES, dk), lambda s: (consumed(s)[1], 0, 0)),
            pl.BlockSpec((None, 1, dk), lambda s: (consumed(s)[1], 0, 0)),
        ]
    else:
        (lbraw,) = extra
        in_specs += [pl.BlockSpec((None, lbraw.shape[1], dk), lambda s: (consumed(s)[1], 0, 0))]
    in_specs += [
        pl.BlockSpec((1, dv), const2),
        pl.BlockSpec(cm.shape, const2),
        pl.BlockSpec(mk.shape, lambda s: (0, 0, 0)),
    ]
    operands = [h2, w_heads, *extra, nw, cm, mk]
    out_specs = [pl.BlockSpec((tc, dv), lambda s: consumed(s))]
    out_shape = [jax.ShapeDtypeStruct((m, n_heads * dv), BF16)]
    side_job = None
    if side is not None:
        side_job, side_in_spec, side_out_spec, side_shape, side_operand = _side_job_specs(
            side, n_tiles + 1)
        in_specs.append(side_in_spec)
        operands.append(side_operand)
        out_specs.append(side_out_spec)
        out_shape.append(side_shape)
    return pl.pallas_call(
        functools.partial(_mixer_kernel, tiles_per_head=nt, side_job=side_job, kind=kind,
                          layer=layer, dk=dk, dv=dv, tc=tc),
        grid=(n_tiles + 1,),
        in_specs=in_specs,
        out_specs=out_specs,
        out_shape=out_shape,
        scratch_shapes=[
            pltpu.VMEM((tc, ncols), F32),
            pltpu.VMEM((tc, ncols), F32),
            pltpu.VMEM((tc, dk), F32),
            pltpu.VMEM((tc, dk), F32),
            pltpu.VMEM((_CS_BLOCKS * CHUNK, (tc // CHUNK) * dk), F32),
            pltpu.VMEM((tc, dv), F32),
            pltpu.VMEM((tc // CHUNK, dv, dk), F32),
            pltpu.VMEM((dv, dk), F32),
        ],
        compiler_params=pltpu.CompilerParams(
            dimension_semantics=("arbitrary",), vmem_limit_bytes=VMEM_LIMIT_BYTES),
        name="mixer_" + kind,
    )(*operands)


def _wi_prep_kernel(w_ref, o_ref, *, d_ff, n_f, tf):
    rows = w_ref.shape[0]
    for j in range(n_f):
        for i in range(tf // MXU_N):
            c0 = j * tf + i * MXU_N
            valid = max(0, min(MXU_N, d_ff - c0))
            for half in range(2):
                src = half * d_ff + c0
                parts = []
                if valid:
                    parts.append(w_ref[:, src:src + valid].astype(BF16))
                if valid < MXU_N:
                    parts.append(jnp.zeros((rows, MXU_N - valid), BF16))
                piece = parts[0] if len(parts) == 1 else jnp.concatenate(parts, axis=1)
                o_ref[j, :, (2 * i + half) * MXU_N:(2 * i + half + 1) * MXU_N] = piece


def _wo_prep_kernel(w_ref, *rest, n_full):
    *tail_refs, o_ref = rest
    r = pl.program_id(0)

    @pl.when(r < n_full)
    def _():
        o_ref[...] = w_ref[...].astype(BF16)

    @pl.when(r >= n_full)
    def _():
        rows = 0
        for t_ref in tail_refs:
            o_ref[rows:rows + LANES, :] = t_ref[...].astype(BF16)
            rows += LANES
        o_ref[rows:, :] = jnp.zeros((o_ref.shape[0] - rows, o_ref.shape[1]), BF16)


def _prep_ffn_weights(wi, wo, tf):
    d, d_ff = wi.shape[0], wo.shape[0]
    assert d_ff % LANES == 0 and d % WPREP_ROWS == 0
    n_f = -(-d_ff // tf)
    wab = pl.pallas_call(
        functools.partial(_wi_prep_kernel, d_ff=d_ff, n_f=n_f, tf=tf),
        grid=(d // WPREP_ROWS,),
        in_specs=[pl.BlockSpec((WPREP_ROWS, 2 * d_ff), lambda r: (r, 0))],
        out_specs=pl.BlockSpec((n_f, WPREP_ROWS, 2 * tf), lambda r: (0, r, 0)),
        out_shape=jax.ShapeDtypeStruct((n_f, d, 2 * tf), BF16),
        compiler_params=pltpu.CompilerParams(
            dimension_semantics=("arbitrary",), vmem_limit_bytes=VMEM_LIMIT_BYTES),
        name="wi_prep",
    )(wi)
    n_full = d_ff // tf
    n_tail = (d_ff - n_full * tf) // LANES
    tail0 = n_full * tf // LANES
    tail_specs = [pl.BlockSpec((LANES, d), functools.partial(lambda r, k: (tail0 + k, 0), k=k))
                  for k in range(n_tail)]
    wob = pl.pallas_call(
        functools.partial(_wo_prep_kernel, n_full=n_full),
        grid=(n_f,),
        in_specs=[pl.BlockSpec((tf, d), lambda r: (jnp.minimum(r, n_full - 1), 0))]
        + tail_specs,
        out_specs=pl.BlockSpec((tf, d), lambda r: (r, 0)),
        out_shape=jax.ShapeDtypeStruct((n_f * tf, d), BF16),
        compiler_params=pltpu.CompilerParams(
            dimension_semantics=("arbitrary",), vmem_limit_bytes=VMEM_LIMIT_BYTES),
        name="wo_prep",
    )(wo, *([wo] * n_tail))
    return wab, wob


_GLA_QK = GLA_HEADS * GLA_DK
_GLA_V = GLA_HEADS * GLA_DV
_HGRN_K = HGRN_HEADS * HGRN_DK
_HGRN_V = HGRN_HEADS * HGRN_DV
_IN_OFFS = tuple(int(o) for o in np.cumsum(
    [0, _GLA_QK, _GLA_QK, _GLA_V, _GLA_V, GLA_GATE_RANK, _HGRN_K, _HGRN_K, _HGRN_V, _HGRN_V]))


def _win_prep_kernel(wt_ref, og_ref, oh_ref, ogr_ref):
    def cols(part, h, width):
        f0 = _IN_OFFS[part] + h * width
        return wt_ref[f0:f0 + width, :].T.astype(BF16)

    gr = wt_ref[_IN_OFFS[4]:_IN_OFFS[5], :]
    gr = jnp.concatenate([gr, jnp.zeros((LANES - GLA_GATE_RANK, gr.shape[1]), F32)], axis=0)
    ogr_ref[...] = gr.T.astype(BF16)
    for h in range(GLA_HEADS):
        og_ref[h] = jnp.concatenate(
            [cols(0, h, GLA_DK), cols(1, h, GLA_DK), cols(2, h, GLA_DV), cols(3, h, GLA_DV)],
            axis=1)
    for h in range(HGRN_HEADS):
        oh_ref[h] = jnp.concatenate(
            [cols(5, h, HGRN_DK), cols(6, h, HGRN_DK), cols(7, h, HGRN_DV), cols(8, h, HGRN_DV)],
            axis=1)


def _prep_mixer_weights(w_in, layer, w2, b2):
    _, d, in_width = w_in.shape
    assert in_width == _IN_OFFS[-1] and d % WPREP_ROWS == 0
    gla_cols = 2 * GLA_DK + 2 * GLA_DV
    hgrn_cols = 2 * HGRN_DK + 2 * HGRN_DV
    wt = jnp.swapaxes(w_in, 1, 2)
    w_gla, w_hg, w_gr = pl.pallas_call(
        _win_prep_kernel,
        grid=(d // WPREP_ROWS,),
        in_specs=[pl.BlockSpec((None, in_width, WPREP_ROWS), lambda r: (layer, 0, r))],
        out_specs=[pl.BlockSpec((GLA_HEADS, WPREP_ROWS, gla_cols), lambda r: (0, r, 0)),
                   pl.BlockSpec((HGRN_HEADS, WPREP_ROWS, hgrn_cols), lambda r: (0, r, 0)),
                   pl.BlockSpec((WPREP_ROWS, LANES), lambda r: (r, 0))],
        out_shape=[jax.ShapeDtypeStruct((GLA_HEADS, d, gla_cols), BF16),
                   jax.ShapeDtypeStruct((HGRN_HEADS, d, hgrn_cols), BF16),
                   jax.ShapeDtypeStruct((d, LANES), BF16)],
        compiler_params=pltpu.CompilerParams(
            dimension_semantics=("arbitrary",), vmem_limit_bytes=VMEM_LIMIT_BYTES),
        name="win_prep",
    )(wt)
    w2h = jnp.pad(w2.astype(BF16), ((0, LANES - GLA_GATE_RANK), (0, 0)))
    w2h = w2h.reshape(LANES, GLA_HEADS, GLA_DK).transpose(1, 0, 2)
    b2h = b2.reshape(GLA_HEADS, 1, GLA_DK)
    return w_gla, w_hg, w_gr, w2h, b2h


def kernel(x, c, ada_w, ada_b, norm_ffn1_w, ffn1_wi, ffn1_wo, norm_mix_w, w_in, gla_gate_w2,
           gla_gate_b2, gla_norm_w, hgrn_norm_w, hgrn_lower_bounds, w_out, norm_ffn2_w, ffn2_wi,
           ffn2_wo, final_norm_w):
    batch, seq, d = x.shape
    depth = ada_w.shape[0]
    m = batch * seq
    consts = _chunk_constants()
    xc = x.reshape(m, d)
    c_pad = jnp.pad(c, ((0, SUBLANES - batch % SUBLANES), (0, 0))) if batch % SUBLANES else c
    gla_v = GLA_HEADS * GLA_DV
    lb_heads = hgrn_lower_bounds.astype(F32).reshape(depth + 1, HGRN_HEADS, HGRN_DK).transpose(1, 0, 2)

    for l in range(depth):
        mod = _adaln(c_pad, ada_w[l], ada_b[l][None, :])[:batch].reshape(batch, N_MOD, d)
        d_ff_pad = -(-ffn1_wo.shape[1] // FFN_PAD) * FFN_PAD
        wab1, wo1 = _prep_ffn_weights(ffn1_wi[l], ffn1_wo[l], FFN_TF_WIDE)
        w_gla, w_hg, w_gr, w2h, b2h = _prep_mixer_weights(w_in, l, gla_gate_w2[l], gla_gate_b2[l])
        wout = w_out[l].astype(BF16)

        x1, h2, gr = _ffn(xc, mod, norm_ffn1_w[l][None, :], norm_mix_w[l][None, :], wab1, wo1,
                          d_ff_pad=d_ff_pad, tokens_per_batch=seq, mod_base=0, epilogue="prenorm",
                          w_aux=w_gr)
        o_gla, wo2 = _mixer(h2, w_gla, (gr, w2h, b2h), gla_norm_w[l][None, :], consts,
                            kind="gla", layer=l, batch=batch, tokens_per_batch=seq, dk=GLA_DK,
                            dv=GLA_DV, side=("wo", ffn2_wo[l]))
        o_hg, wab2 = _mixer(h2, w_hg, (lb_heads,), hgrn_norm_w[l][None, :], consts,
                            kind="hgrn", layer=l, batch=batch, tokens_per_batch=seq, dk=HGRN_DK,
                            dv=HGRN_DV, side=("wi", ffn2_wi[l]))
        last = l == depth - 1
        nw2 = final_norm_w[None, :] if last else norm_ffn2_w[l][None, :]
        (xc,) = _ffn(x1, mod, norm_ffn2_w[l][None, :], nw2, wab2, wo2, d_ff_pad=d_ff_pad,
                     tokens_per_batch=seq, mod_base=6, epilogue="final" if last else "none",
                     mix=(o_gla, o_hg, wout[:gla_v], wout[gla_v:]), mix_gate_row=5)
    return xc.reshape(batch, seq, d)
```

```python
import functools

import jax
import jax.numpy as jnp
import numpy as np
from jax import lax
from jax.experimental import pallas as pl
from jax.experimental.pallas import tpu as pltpu

F32 = jnp.float32
BF16 = jnp.bfloat16

GLA_HEADS = 4
GLA_DK = 128
GLA_DV = 256
GLA_GATE_RANK = 16
GLA_GATE_NORMALIZER = 16.0
HGRN_HEADS = 8
HGRN_DK = 128
HGRN_DV = 128
CHUNK = 64
MACARON_W = 0.5
N_MOD = 9
EPS = 1e-6

LANES = 128
SUBLANES = 8
MXU_N = 256
VMEM_LIMIT_BYTES = 56 * 1024 * 1024

FFN_TM = 512
FFN_TF = 512
FFN_TF_WIDE = 1024
FFN_PAD = 512
PROLOGUE_ROWS = 256
NORM_ROWS = 16
AUX_ROWS = 128
MIX_TC = 1024
ADALN_TN = 1024
WPREP_ROWS = 256
SIDE_ROWS = 16
ROW_BLK = 16

_NT = (((1,), (1,)), ((), ()))
_TN = (((0,), (0,)), ((), ()))


def _sigmoid(x):
    return jax.nn.sigmoid(x)


def _rms(x):
    return x * lax.rsqrt(jnp.mean(x * x, axis=-1, keepdims=True) + EPS)


def _adaln_kernel(c_ref, w_ref, b_ref, o_ref):
    c = c_ref[...]
    ca = (c * _sigmoid(c)).astype(BF16)
    o_ref[...] = jnp.dot(ca, w_ref[...].astype(BF16), preferred_element_type=F32) + b_ref[...]


def _adaln(c_pad, w, b):
    rows, d = c_pad.shape
    n = w.shape[1]
    assert n % ADALN_TN == 0
    return pl.pallas_call(
        _adaln_kernel,
        grid=(n // ADALN_TN,),
        in_specs=[
            pl.BlockSpec((rows, d), lambda j: (0, 0)),
            pl.BlockSpec((d, ADALN_TN), lambda j: (0, j)),
            pl.BlockSpec((1, ADALN_TN), lambda j: (0, j)),
        ],
        out_specs=pl.BlockSpec((rows, ADALN_TN), lambda j: (0, j)),
        out_shape=jax.ShapeDtypeStruct((rows, n), F32),
        compiler_params=pltpu.CompilerParams(
            dimension_semantics=("arbitrary",), vmem_limit_bytes=VMEM_LIMIT_BYTES),
        name="adaln",
    )(c_pad, w, b)


def _ffn_kernel(*refs, mod_base, epilogue, mix_gate_row, last_cols):
    x_ref, mod_ref, nw_ref, nw2_ref, wab_ref, wo_ref, *rest = refs
    if mix_gate_row is not None:
        oa_ref, ob_ref, wa_ref, wb_ref, *rest = rest
    if epilogue == "prenorm":
        waux_ref, *rest = rest
        xo_ref, ho_ref, aux_ref, h_scr, *rest = rest
    else:
        xo_ref, h_scr, *rest = rest
    xin_ref = rest[0] if mix_gate_row is not None else x_ref
    f = pl.program_id(1)
    last = pl.num_programs(1) - 1
    tm = x_ref.shape[0]

    def prologue():
        shift = mod_ref[mod_base:mod_base + 1, :]
        gain = nw_ref[...] * (1.0 + mod_ref[mod_base + 1:mod_base + 2, :])
        for r0 in range(0, tm, PROLOGUE_ROWS):
            rows = slice(r0, r0 + PROLOGUE_ROWS)
            if mix_gate_row is not None:
                y = (jnp.dot(oa_ref[rows, :], wa_ref[...], preferred_element_type=F32)
                     + jnp.dot(ob_ref[rows, :], wb_ref[...], preferred_element_type=F32))
                xin_ref[rows, :] = x_ref[rows, :] + mod_ref[mix_gate_row:mix_gate_row + 1, :] * y
            for c0 in range(r0, r0 + PROLOGUE_ROWS, NORM_ROWS):
                chunk = slice(c0, c0 + NORM_ROWS)
                h_scr[chunk, :] = (_rms(xin_ref[chunk, :]) * gain + shift).astype(BF16)

    def swiglu_step(first, cols=None):
        cols = wo_ref.shape[0] if cols is None else cols
        zab = jnp.dot(h_scr[...], wab_ref[:, :2 * cols], preferred_element_type=F32)
        pieces = cols // MXU_N
        a = jnp.concatenate([zab[:, (2 * i) * MXU_N:(2 * i + 1) * MXU_N] for i in range(pieces)],
                            axis=1)
        b = jnp.concatenate([zab[:, (2 * i + 1) * MXU_N:(2 * i + 2) * MXU_N]
                             for i in range(pieces)], axis=1)
        act = (a * _sigmoid(a) * b).astype(BF16)
        update = jnp.dot(act, wo_ref[:cols, :], preferred_element_type=F32)
        if first:
            xo_ref[...] = update
        else:
            xo_ref[...] += update

    def finish():
        gate = MACARON_W * mod_ref[mod_base + 2:mod_base + 3, :]
        if epilogue == "prenorm":
            shift2 = mod_ref[mod_base + 3:mod_base + 4, :]
            gain2 = nw2_ref[...] * (1.0 + mod_ref[mod_base + 4:mod_base + 5, :])
        for c0 in range(0, tm, NORM_ROWS):
            chunk = slice(c0, c0 + NORM_ROWS)
            xn = xin_ref[chunk, :] + gate * xo_ref[chunk, :]
            if epilogue == "prenorm":
                xo_ref[chunk, :] = xn
                ho_ref[chunk, :] = (_rms(xn) * gain2 + shift2).astype(BF16)
            elif epilogue == "final":
                xo_ref[chunk, :] = _rms(xn) * nw2_ref[...]
            else:
                xo_ref[chunk, :] = xn
            done = c0 + NORM_ROWS
            if epilogue == "prenorm" and done % AUX_ROWS == 0:
                rows = slice(done - AUX_ROWS, done)
                aux_ref[rows, :] = jnp.dot(ho_ref[rows, :], waux_ref[...],
                                           preferred_element_type=F32).astype(BF16)

    @pl.when(f == 0)
    def _():
        prologue()
        swiglu_step(first=True)

    @pl.when(jnp.logical_and(f > 0, f < last))
    def _():
        swiglu_step(first=False)

    @pl.when(f == last)
    def _():
        swiglu_step(first=False, cols=last_cols)
        finish()


def _ffn(x2d, mod, nw, nw2, wab, wo, *, d_ff_pad, tokens_per_batch, mod_base, epilogue, mix=None,
         mix_gate_row=None, w_aux=None):
    m, d = x2d.shape
    n_f, _, tf2 = wab.shape
    tf = tf2 // 2
    tm = FFN_TM
    assert m % tm == 0 and tokens_per_batch % tm == 0 and wo.shape[0] == n_f * tf
    assert d_ff_pad % MXU_N == 0 and (n_f - 1) * tf < d_ff_pad <= n_f * tf
    last_cols = d_ff_pad - (n_f - 1) * tf
    assert (mix is None) == (mix_gate_row is None) and n_f >= 2
    assert (w_aux is not None) == (epilogue == "prenorm")
    tiles_per_batch = tokens_per_batch // tm
    row_spec = pl.BlockSpec((tm, d), lambda i, f: (i, 0))
    vec_spec = pl.BlockSpec((1, d), lambda i, f: (0, 0))
    in_specs = [
        row_spec,
        pl.BlockSpec((None, N_MOD, d), lambda i, f: (i // tiles_per_batch, 0, 0)),
        vec_spec,
        vec_spec,
        pl.BlockSpec((None, d, 2 * tf), lambda i, f: (f, 0, 0)),
        pl.BlockSpec((tf, d), lambda i, f: (f, 0)),
    ]
    operands = [x2d, mod, nw, nw2, wab, wo]
    scratch = [pltpu.VMEM((tm, d), BF16)]
    if mix is not None:
        oa, ob, wa, wb = mix
        in_specs += [
            pl.BlockSpec((tm, oa.shape[1]), lambda i, f: (i, 0)),
            pl.BlockSpec((tm, ob.shape[1]), lambda i, f: (i, 0)),
            pl.BlockSpec(wa.shape, lambda i, f: (0, 0), pipeline_mode=pl.Buffered(1)),
            pl.BlockSpec(wb.shape, lambda i, f: (0, 0), pipeline_mode=pl.Buffered(1)),
        ]
        operands += [oa, ob, wa, wb]
        scratch.append(pltpu.VMEM((tm, d), F32))
    out_shape = [jax.ShapeDtypeStruct((m, d), F32)]
    out_specs = [row_spec]
    if epilogue == "prenorm":
        in_specs.append(pl.BlockSpec(w_aux.shape, lambda i, f: (0, 0)))
        operands.append(w_aux)
        out_shape += [jax.ShapeDtypeStruct((m, d), BF16),
                      jax.ShapeDtypeStruct((m, w_aux.shape[1]), BF16)]
        out_specs += [row_spec, pl.BlockSpec((tm, w_aux.shape[1]), lambda i, f: (i, 0))]
    return pl.pallas_call(
        functools.partial(_ffn_kernel, mod_base=mod_base, epilogue=epilogue,
                          mix_gate_row=mix_gate_row, last_cols=last_cols),
        grid=(m // tm, n_f),
        in_specs=in_specs,
        out_specs=out_specs,
        out_shape=out_shape,
        scratch_shapes=scratch,
        compiler_params=pltpu.CompilerParams(
            dimension_semantics=("arbitrary", "arbitrary"),
            vmem_limit_bytes=VMEM_LIMIT_BYTES),
        name="ffn_" + epilogue,
    )(*operands)


_LEVEL_HALVES = tuple(CHUNK >> (j + 1) for j in range(CHUNK.bit_length() - 1))
_N_LEVELS = len(_LEVEL_HALVES)
_CS_BLOCKS = _N_LEVELS + 2
_GROUP = 4
_GROUP_ROWS = _GROUP * CHUNK


def _chunk_constants():
    t = np.arange(CHUNK)
    tri = (t[None, :] <= t[:, None]).astype(np.float32)
    blocks = [tri]
    masks = []
    for half in _LEVEL_HALVES:
        ref = (t // (2 * half)) * (2 * half) + half
        if half < SUBLANES // 2:
            blocks.append(tri - tri[ref])
        same_block = (t[:, None] // (2 * half)) == (t[None, :] // (2 * half))
        is_query = (t % (2 * half)) >= half
        masks.append((same_block & is_query[:, None] & ~is_query[None, :]).astype(np.float32))
    masks.append(np.eye(CHUNK, dtype=np.float32))
    cm = np.concatenate(blocks, axis=0)
    cm2 = np.concatenate([cm, cm], axis=1)
    per_half = LANES // CHUNK
    group_masks = np.stack([np.kron(np.eye(per_half, dtype=np.float32), m) for m in masks], axis=0)
    return jnp.asarray(cm2, dtype=BF16), jnp.asarray(group_masks, dtype=F32)


def _mixer_kernel(*refs, tiles_per_head, side_job, **static):
    *io_refs, z_a, z_b, q_scr, k_scr, cs_scr, oi_scr, u_scr, s_scr = refs
    scratch = (q_scr, k_scr, cs_scr, oi_scr, u_scr, s_scr)
    s = pl.program_id(0)
    if side_job is not None:
        *io_refs, side_in, o_ref, side_out = io_refs
        io_refs.append(o_ref)
        side_job(s, side_in, side_out)

    @pl.when(s == 0)
    def _():
        z_b[...] = jnp.zeros_like(z_b)

    @pl.when(lax.rem(jnp.maximum(s - 1, 0), tiles_per_head) == 0)
    def _():
        s_scr[...] = jnp.zeros_like(s_scr)

    @pl.when(lax.rem(s, 2) == 0)
    def _():
        _mixer_tile(io_refs, scratch, z_b, z_a, **static)

    @pl.when(lax.rem(s, 2) == 1)
    def _():
        _mixer_tile(io_refs, scratch, z_a, z_b, **static)


def _mixer_tile(io_refs, scratch, z_scr, z_next, *, kind, layer, dk, dv, tc):
    if kind == "gla":
        h_ref, w_ref, gr_ref, w2_ref, b2_ref, nw_ref, cm_ref, mk_ref, o_ref = io_refs
    else:
        h_ref, w_ref, lb_ref, nw_ref, cm_ref, mk_ref, o_ref = io_refs
    q_scr, k_scr, cs_scr, oi_scr, u_scr, s_scr = scratch
    n_chunks = tc // CHUNK
    v_cols = slice(2 * dk, 2 * dk + dv)
    gate_cols = slice(2 * dk + dv, 2 * dk + 2 * dv)

    ncols = w_ref.shape[1]
    pieces = [slice(c0, min(c0 + MXU_N, ncols)) for c0 in range(0, ncols, MXU_N)]
    n_slots = n_chunks // _GROUP + 2

    def project_pieces(slot):
        for p, cols in enumerate(pieces):
            if p * n_slots // len(pieces) == slot:
                z_next[:, cols] = jnp.dot(h_ref[...], w_ref[:, cols], preferred_element_type=F32)

    project_pieces(0)

    if kind == "gla":
        q_scr[...] = z_scr[:, 0:dk] * (dk ** -0.5)
        k_scr[...] = z_scr[:, dk:2 * dk]
        gp = jnp.dot(gr_ref[...], w2_ref[...], preferred_element_type=F32) + b2_ref[...]
        la = (jnp.minimum(gp, 0.0) - jnp.log(1.0 + jnp.exp(-jnp.abs(gp)))) * (
            1.0 / GLA_GATE_NORMALIZER)
    else:
        raw = lb_ref[...]
        ex = jnp.exp(raw - jnp.max(raw, axis=0, keepdims=True))
        p = ex / jnp.sum(ex, axis=0, keepdims=True)
        lb = jnp.sum(p[0:layer + 1, :], axis=0, keepdims=True)
        hq = z_scr[:, 0:dk]
        fr = z_scr[:, dk:2 * dk]
        q_scr[...] = hq * _sigmoid(hq)
        en = jnp.exp(-jnp.abs(fr))
        one_en = 1.0 + en
        log_sig = jnp.minimum(fr, 0.0) - jnp.log(one_en)
        sig_neg = jnp.where(fr >= 0.0, en, 1.0) / one_en
        la_a = jnp.log(lb)
        la_b = jnp.log(1.0 - lb) + log_sig
        la = jnp.maximum(la_a, la_b) + jnp.log(1.0 + jnp.exp(-jnp.abs(la_a - la_b)))
        k_scr[...] = (1.0 - lb) * sig_neg

    la_wide = jnp.concatenate([la[c * CHUNK:(c + 1) * CHUNK] for c in range(n_chunks)], axis=1)
    la_hi = la_wide.astype(BF16)
    la_lo = (la_wide - la_hi.astype(F32)).astype(BF16)
    cs_mxu = jnp.dot(cm_ref[...], jnp.concatenate([la_hi, la_lo], axis=0),
                     preferred_element_type=F32)
    cs_scr[0:CHUNK, :] = cs_mxu[0:CHUNK]
    mxu_block = 1
    for j, half in enumerate(_LEVEL_HALVES):
        lo = (j + 1) * CHUNK
        if half < SUBLANES // 2:
            cs_scr[lo:lo + CHUNK, :] = cs_mxu[mxu_block * CHUNK:(mxu_block + 1) * CHUNK]
            mxu_block += 1
            continue
        for t0 in range(0, CHUNK, 2 * half):
            cs_scr[lo + t0:lo + t0 + 2 * half, :] = (
                cs_scr[t0:t0 + 2 * half, :] - cs_scr[t0 + half:t0 + half + 1, :])
    lo = (_N_LEVELS + 1) * CHUNK
    cs_scr[lo:lo + CHUNK, :] = cs_scr[CHUNK - 1:CHUNK, :] - cs_scr[0:CHUNK, :]

    def cs_block(block, c):
        return cs_scr[block * CHUNK:(block + 1) * CHUNK, c * dk:(c + 1) * dk]

    for g in range(n_chunks // _GROUP):
        project_pieces(g + 1)
        chunks = range(g * _GROUP, (g + 1) * _GROUP)
        rows = slice(g * _GROUP_ROWS, (g + 1) * _GROUP_ROWS)
        q = q_scr[rows, :]
        k = k_scr[rows, :]
        n_halves = _GROUP_ROWS // LANES
        blks = LANES // ROW_BLK
        qk = jnp.sum(q * k, axis=-1, keepdims=True)
        eye = mk_ref[_N_LEVELS]
        attn = [[eye[r * ROW_BLK:(r + 1) * ROW_BLK] * qk[h * LANES + r * ROW_BLK:
                                                        h * LANES + (r + 1) * ROW_BLK]
                 for r in range(blks)] for h in range(n_halves)]
        for j, half in enumerate(_LEVEL_HALVES):
            d = jnp.concatenate([cs_block(j + 1, c) for c in chunks], axis=0)
            e = jnp.exp(-jnp.abs(d))
            ke = (k * e).astype(BF16)
            mask = mk_ref[j]
            if half >= ROW_BLK:
                q_runs = [r0 for r0 in range(0, _GROUP_ROWS, ROW_BLK) if (r0 // half) % 2 == 1]
                qe = jnp.concatenate([q[r0:r0 + ROW_BLK] * e[r0:r0 + ROW_BLK] for r0 in q_runs],
                                     axis=0).astype(BF16)
                s = lax.dot_general(qe, ke, _NT, preferred_element_type=F32)
                for n, r0 in enumerate(q_runs):
                    h, r = r0 // LANES, (r0 % LANES) // ROW_BLK
                    attn[h][r] = attn[h][r] + (
                        mask[r * ROW_BLK:(r + 1) * ROW_BLK]
                        * s[n * ROW_BLK:(n + 1) * ROW_BLK, h * LANES:(h + 1) * LANES])
            else:
                s = lax.dot_general((q * e).astype(BF16), ke, _NT, preferred_element_type=F32)
                for h in range(n_halves):
                    for r in range(blks):
                        r0 = h * LANES + r * ROW_BLK
                        attn[h][r] = attn[h][r] + (mask[r * ROW_BLK:(r + 1) * ROW_BLK]
                                                   * s[r0:r0 + ROW_BLK, h * LANES:(h + 1) * LANES])
        vb = z_scr[rows, v_cols].astype(BF16)
        for h in range(n_halves):
            a = jnp.concatenate(attn[h], axis=0).astype(BF16)
            oi_scr[rows.start + h * LANES:rows.start + (h + 1) * LANES, :] = jnp.dot(
                a, vb[h * LANES:(h + 1) * LANES], preferred_element_type=F32)
        for i, c in enumerate(chunks):
            crow = slice(c * CHUNK, (c + 1) * CHUNK)
            kd = (k_scr[crow, :] * jnp.exp(cs_block(_N_LEVELS + 1, c))).astype(BF16)
            u_scr[c] = lax.dot_general(vb[i * CHUNK:(i + 1) * CHUNK], kd, _TN,
                                       preferred_element_type=F32)

    project_pieces(n_slots - 1)
    st = s_scr[...]
    for c in range(n_chunks):
        rows = slice(c * CHUNK, (c + 1) * CHUNK)
        b = cs_block(0, c)
        qb = (q_scr[rows, :] * jnp.exp(b)).astype(BF16)
        o = lax.dot_general(qb, st.astype(BF16), _NT, preferred_element_type=F32) + oi_scr[rows, :]
        st = st * jnp.exp(b[CHUNK - 1:CHUNK, :]) + u_scr[c]
        gate = z_scr[rows, gate_cols]
        o_ref[rows, :] = (_rms(o) * nw_ref[...] * (gate * _sigmoid(gate))).astype(BF16)
    s_scr[...] = st


def _side_wi(s, w_ref, o_ref, *, n_slabs, d_ff, n_f, tf):
    @pl.when(s < n_slabs)
    def _():
        _wi_prep_kernel(w_ref, o_ref, d_ff=d_ff, n_f=n_f, tf=tf)


def _side_wo(s, w_ref, o_ref, *, n_src, n_out):
    @pl.when(s < n_src)
    def _():
        o_ref[...] = w_ref[...].astype(BF16)

    @pl.when(jnp.logical_and(s >= n_src, s < n_out))
    def _():
        o_ref[...] = jnp.zeros_like(o_ref)


def _side_job_specs(side, n_steps):
    what, w = side
    if what == "wi":
        d, d_ff = w.shape[0], w.shape[1] // 2
        n_f = -(-d_ff // FFN_TF)
        tf = FFN_TF
        n_slabs = d // SIDE_ROWS
        assert d % SIDE_ROWS == 0 and n_slabs <= n_steps and d_ff % LANES == 0
        slab = lambda s: jnp.minimum(s, n_slabs - 1)
        return (functools.partial(_side_wi, n_slabs=n_slabs, d_ff=d_ff, n_f=n_f, tf=tf),
                pl.BlockSpec((SIDE_ROWS, 2 * d_ff), lambda s: (slab(s), 0)),
                pl.BlockSpec((n_f, SIDE_ROWS, 2 * tf), lambda s: (0, slab(s), 0)),
                jax.ShapeDtypeStruct((n_f, d, 2 * tf), BF16), w)
    assert what == "wo"
    d_ff, d = w.shape
    n_src = d_ff // LANES
    n_out = -(-d_ff // FFN_TF) * FFN_TF // LANES
    assert d_ff % LANES == 0 and n_out <= n_steps
    return (functools.partial(_side_wo, n_src=n_src, n_out=n_out),
            pl.BlockSpec((LANES, d), lambda s: (jnp.minimum(s, n_src - 1), 0)),
            pl.BlockSpec((LANES, d), lambda s: (jnp.minimum(s, n_out - 1), 0)),
            jax.ShapeDtypeStruct((n_out * LANES, d), BF16), w)


def _mixer(h2, w_heads, extra, nw, consts, *, kind, layer, batch, tokens_per_batch, dk, dv,
           side=None):
    m, d = h2.shape
    n_heads, _, ncols = w_heads.shape
    tc = MIX_TC
    assert tokens_per_batch % tc == 0 and tc % _GROUP_ROWS == 0
    nt = tokens_per_batch // tc
    n_tiles = batch * n_heads * nt
    cm, mk = consts

    def coords(tile):
        bb = tile // (n_heads * nt)
        hh = lax.rem(tile // nt, n_heads)
        return bb * nt + lax.rem(tile, nt), hh

    def projected(s):
        return coords(jnp.minimum(s, n_tiles - 1))

    def consumed(s):
        return coords(jnp.maximum(s - 1, 0))

    const2 = lambda s: (0, 0)
    in_specs = [
        pl.BlockSpec((tc, d), lambda s: (projected(s)[0], 0)),
        pl.BlockSpec((None, d, ncols), lambda s: (projected(s)[1], 0, 0)),
    ]
    if kind == "gla":
        gr, w2, b2 = extra
        in_specs += [
            pl.BlockSpec((tc, LANES), lambda s: (consumed(s)[0], 0)),
            pl.BlockSpec((None, LANES, dk), lambda s: (consumed(s)[1], 0, 0)),
            pl.BlockSpec((None, 1, dk), lambda s: (consumed(s)[1], 0, 0)),
        ]
    else:
        (lbraw,) = extra
        in_specs += [pl.BlockSpec((None, lbraw.shape[1], dk), lambda s: (consumed(s)[1], 0, 0))]
    in_specs += [
        pl.BlockSpec((1, dv), const2),
        pl.BlockSpec(cm.shape, const2),
        pl.BlockSpec(mk.shape, lambda s: (0, 0, 0)),
    ]
    operands = [h2, w_heads, *extra, nw, cm, mk]
    out_specs = [pl.BlockSpec((tc, dv), lambda s: consumed(s))]
    out_shape = [jax.ShapeDtypeStruct((m, n_heads * dv), BF16)]
    side_job = None
    if side is not None:
        side_job, side_in_spec, side_out_spec, side_shape, side_operand = _side_job_specs(
            side, n_tiles + 1)
        in_specs.append(side_in_spec)
        operands.append(side_operand)
        out_specs.append(side_out_spec)
        out_shape.append(side_shape)
    return pl.pallas_call(
        functools.partial(_mixer_kernel, tiles_per_head=nt, side_job=side_job, kind=kind,
                          layer=layer, dk=dk, dv=dv, tc=tc),
        grid=(n_tiles + 1,),
        in_specs=in_specs,
        out_specs=out_specs,
        out_shape=out_shape,
        scratch_shapes=[
            pltpu.VMEM((tc, ncols), F32),
            pltpu.VMEM((tc, ncols), F32),
            pltpu.VMEM((tc, dk), F32),
            pltpu.VMEM((tc, dk), F32),
            pltpu.VMEM((_CS_BLOCKS * CHUNK, (tc // CHUNK) * dk), F32),
            pltpu.VMEM((tc, dv), F32),
            pltpu.VMEM((tc // CHUNK, dv, dk), F32),
            pltpu.VMEM((dv, dk), F32),
        ],
        compiler_params=pltpu.CompilerParams(
            dimension_semantics=("arbitrary",), vmem_limit_bytes=VMEM_LIMIT_BYTES),
        name="mixer_" + kind,
    )(*operands)


def _wi_prep_kernel(w_ref, o_ref, *, d_ff, n_f, tf):
    rows = w_ref.shape[0]
    for j in range(n_f):
        for i in range(tf // MXU_N):
            c0 = j * tf + i * MXU_N
            valid = max(0, min(MXU_N, d_ff - c0))
            for half in range(2):
                src = half * d_ff + c0
                parts = []
                if valid:
                    parts.append(w_ref[:, src:src + valid].astype(BF16))
                if valid < MXU_N:
                    parts.append(jnp.zeros((rows, MXU_N - valid), BF16))
                piece = parts[0] if len(parts) == 1 else jnp.concatenate(parts, axis=1)
                o_ref[j, :, (2 * i + half) * MXU_N:(2 * i + half + 1) * MXU_N] = piece


def _wo_prep_kernel(w_ref, *rest, n_full):
    *tail_refs, o_ref = rest
    r = pl.program_id(0)

    @pl.when(r < n_full)
    def _():
        o_ref[...] = w_ref[...].astype(BF16)

    @pl.when(r >= n_full)
    def _():
        rows = 0
        for t_ref in tail_refs:
            o_ref[rows:rows + LANES, :] = t_ref[...].astype(BF16)
            rows += LANES
        o_ref[rows:, :] = jnp.zeros((o_ref.shape[0] - rows, o_ref.shape[1]), BF16)


def _prep_ffn_weights(wi, wo, tf):
    d, d_ff = wi.shape[0], wo.shape[0]
    assert d_ff % LANES == 0 and d % WPREP_ROWS == 0
    n_f = -(-d_ff // tf)
    wab = pl.pallas_call(
        functools.partial(_wi_prep_kernel, d_ff=d_ff, n_f=n_f, tf=tf),
        grid=(d // WPREP_ROWS,),
        in_specs=[pl.BlockSpec((WPREP_ROWS, 2 * d_ff), lambda r: (r, 0))],
        out_specs=pl.BlockSpec((n_f, WPREP_ROWS, 2 * tf), lambda r: (0, r, 0)),
        out_shape=jax.ShapeDtypeStruct((n_f, d, 2 * tf), BF16),
        compiler_params=pltpu.CompilerParams(
            dimension_semantics=("arbitrary",), vmem_limit_bytes=VMEM_LIMIT_BYTES),
        name="wi_prep",
    )(wi)
    n_full = d_ff // tf
    n_tail = (d_ff - n_full * tf) // LANES
    tail0 = n_full * tf // LANES
    tail_specs = [pl.BlockSpec((LANES, d), functools.partial(lambda r, k: (tail0 + k, 0), k=k))
                  for k in range(n_tail)]
    wob = pl.pallas_call(
        functools.partial(_wo_prep_kernel, n_full=n_full),
        grid=(n_f,),
        in_specs=[pl.BlockSpec((tf, d), lambda r: (jnp.minimum(r, n_full - 1), 0))]
        + tail_specs,
        out_specs=pl.BlockSpec((tf, d), lambda r: (r, 0)),
        out_shape=jax.ShapeDtypeStruct((n_f * tf, d), BF16),
        compiler_params=pltpu.CompilerParams(
            dimension_semantics=("arbitrary",), vmem_limit_bytes=VMEM_LIMIT_BYTES),
        name="wo_prep",
    )(wo, *([wo] * n_tail))
    return wab, wob


_GLA_QK = GLA_HEADS * GLA_DK
_GLA_V = GLA_HEADS * GLA_DV
_HGRN_K = HGRN_HEADS * HGRN_DK
_HGRN_V = HGRN_HEADS * HGRN_DV
_IN_OFFS = tuple(int(o) for o in np.cumsum(
    [0, _GLA_QK, _GLA_QK, _GLA_V, _GLA_V, GLA_GATE_RANK, _HGRN_K, _HGRN_K, _HGRN_V, _HGRN_V]))


def _win_prep_kernel(wt_ref, og_ref, oh_ref, ogr_ref):
    def cols(part, h, width):
        f0 = _IN_OFFS[part] + h * width
        return wt_ref[f0:f0 + width, :].T.astype(BF16)

    gr = wt_ref[_IN_OFFS[4]:_IN_OFFS[5], :]
    gr = jnp.concatenate([gr, jnp.zeros((LANES - GLA_GATE_RANK, gr.shape[1]), F32)], axis=0)
    ogr_ref[...] = gr.T.astype(BF16)
    for h in range(GLA_HEADS):
        og_ref[h] = jnp.concatenate(
            [cols(0, h, GLA_DK), cols(1, h, GLA_DK), cols(2, h, GLA_DV), cols(3, h, GLA_DV)],
            axis=1)
    for h in range(HGRN_HEADS):
        oh_ref[h] = jnp.concatenate(
            [cols(5, h, HGRN_DK), cols(6, h, HGRN_DK), cols(7, h, HGRN_DV), cols(8, h, HGRN_DV)],
            axis=1)


def _prep_mixer_weights(w_in, layer, w2, b2):
    _, d, in_width = w_in.shape
    assert in_width == _IN_OFFS[-1] and d % WPREP_ROWS == 0
    gla_cols = 2 * GLA_DK + 2 * GLA_DV
    hgrn_cols = 2 * HGRN_DK + 2 * HGRN_DV
    wt = jnp.swapaxes(w_in, 1, 2)
    w_gla, w_hg, w_gr = pl.pallas_call(
        _win_prep_kernel,
        grid=(d // WPREP_ROWS,),
        in_specs=[pl.BlockSpec((None, in_width, WPREP_ROWS), lambda r: (layer, 0, r))],
        out_specs=[pl.BlockSpec((GLA_HEADS, WPREP_ROWS, gla_cols), lambda r: (0, r, 0)),
                   pl.BlockSpec((HGRN_HEADS, WPREP_ROWS, hgrn_cols), lambda r: (0, r, 0)),
                   pl.BlockSpec((WPREP_ROWS, LANES), lambda r: (r, 0))],
        out_shape=[jax.ShapeDtypeStruct((GLA_HEADS, d, gla_cols), BF16),
                   jax.ShapeDtypeStruct((HGRN_HEADS, d, hgrn_cols), BF16),
                   jax.ShapeDtypeStruct((d, LANES), BF16)],
        compiler_params=pltpu.CompilerParams(
            dimension_semantics=("arbitrary",), vmem_limit_bytes=VMEM_LIMIT_BYTES),
        name="win_prep",
    )(wt)
    w2h = jnp.pad(w2.astype(BF16), ((0, LANES - GLA_GATE_RANK), (0, 0)))
    w2h = w2h.reshape(LANES, GLA_HEADS, GLA_DK).transpose(1, 0, 2)
    b2h = b2.reshape(GLA_HEADS, 1, GLA_DK)
    return w_gla, w_hg, w_gr, w2h, b2h


def kernel(x, c, ada_w, ada_b, norm_ffn1_w, ffn1_wi, ffn1_wo, norm_mix_w, w_in, gla_gate_w2,
           gla_gate_b2, gla_norm_w, hgrn_norm_w, hgrn_lower_bounds, w_out, norm_ffn2_w, ffn2_wi,
           ffn2_wo, final_norm_w):
    batch, seq, d = x.shape
    depth = ada_w.shape[0]
    m = batch * seq
    consts = _chunk_constants()
    xc = x.reshape(m, d)
    c_pad = jnp.pad(c, ((0, SUBLANES - batch % SUBLANES), (0, 0))) if batch % SUBLANES else c
    gla_v = GLA_HEADS * GLA_DV
    lb_heads = hgrn_lower_bounds.astype(F32).reshape(depth + 1, HGRN_HEADS, HGRN_DK).transpose(1, 0, 2)

    for l in range(depth):
        mod = _adaln(c_pad, ada_w[l], ada_b[l][None, :])[:batch].reshape(batch, N_MOD, d)
        d_ff_pad = -(-ffn1_wo.shape[1] // FFN_PAD) * FFN_PAD
        wab1, wo1 = _prep_ffn_weights(ffn1_wi[l], ffn1_wo[l], FFN_TF_WIDE)
        w_gla, w_hg, w_gr, w2h, b2h = _prep_mixer_weights(w_in, l, gla_gate_w2[l], gla_gate_b2[l])
        wout = w_out[l].astype(BF16)

        x1, h2, gr = _ffn(xc, mod, norm_ffn1_w[l][None, :], norm_mix_w[l][None, :], wab1, wo1,
                          d_ff_pad=d_ff_pad, tokens_per_batch=seq, mod_base=0, epilogue="prenorm",
                          w_aux=w_gr)
        o_gla, wo2 = _mixer(h2, w_gla, (gr, w2h, b2h), gla_norm_w[l][None, :], consts,
                            kind="gla", layer=l, batch=batch, tokens_per_batch=seq, dk=GLA_DK,
                            dv=GLA_DV, side=("wo", ffn2_wo[l]))
        o_hg, wab2 = _mixer(h2, w_hg, (lb_heads,), hgrn_norm_w[l][None, :], consts,
                            kind="hgrn", layer=l, batch=batch, tokens_per_batch=seq, dk=HGRN_DK,
                            dv=HGRN_DV, side=("wi", ffn2_wi[l]))
        last = l == depth - 1
        nw2 = final_norm_w[None, :] if last else norm_ffn2_w[l][None, :]
        (xc,) = _ffn(x1, mod, norm_ffn2_w[l][None, :], nw2, wab2, wo2, d_ff_pad=d_ff_pad,
                     tokens_per_batch=seq, mod_base=6, epilogue="final" if last else "none",
                     mix=(o_gla, o_hg, wout[:gla_v], wout[gla_v:]), mix_gate_row=5)
    return xc.reshape(batch, seq, d)
```

```python
import functools

import jax
import jax.numpy as jnp
import numpy as np
from jax import lax
from jax.experimental import pallas as pl
from jax.experimental.pallas import tpu as pltpu

F32 = jnp.float32
BF16 = jnp.bfloat16

GLA_HEADS = 4
GLA_DK = 128
GLA_DV = 256
GLA_GATE_RANK = 16
GLA_GATE_NORMALIZER = 16.0
HGRN_HEADS = 8
HGRN_DK = 128
HGRN_DV = 128
CHUNK = 64
MACARON_W = 0.5
N_MOD = 9
EPS = 1e-6

LANES = 128
SUBLANES = 8
MXU_N = 256
VMEM_LIMIT_BYTES = 56 * 1024 * 1024

FFN_TM = 512
FFN_TF = 512
FFN_TF_WIDE = 1024
FFN_PAD = 512
PROLOGUE_ROWS = 256
NORM_ROWS = 16
AUX_ROWS = 128
MIX_TC = 1024
ADALN_TN = 1024
WPREP_ROWS = 256
SIDE_ROWS = 16
ROW_BLK = 8

_NT = (((1,), (1,)), ((), ()))
_TN = (((0,), (0,)), ((), ()))


def _sigmoid(x):
    return jax.nn.sigmoid(x)


def _rms(x):
    return x * lax.rsqrt(jnp.mean(x * x, axis=-1, keepdims=True) + EPS)


def _adaln_kernel(c_ref, w_ref, b_ref, o_ref):
    c = c_ref[...]
    ca = (c * _sigmoid(c)).astype(BF16)
    o_ref[...] = jnp.dot(ca, w_ref[...].astype(BF16), preferred_element_type=F32) + b_ref[...]


def _adaln(c_pad, w, b):
    rows, d = c_pad.shape
    n = w.shape[1]
    assert n % ADALN_TN == 0
    return pl.pallas_call(
        _adaln_kernel,
        grid=(n // ADALN_TN,),
        in_specs=[
            pl.BlockSpec((rows, d), lambda j: (0, 0)),
            pl.BlockSpec((d, ADALN_TN), lambda j: (0, j)),
            pl.BlockSpec((1, ADALN_TN), lambda j: (0, j)),
        ],
        out_specs=pl.BlockSpec((rows, ADALN_TN), lambda j: (0, j)),
        out_shape=jax.ShapeDtypeStruct((rows, n), F32),
        compiler_params=pltpu.CompilerParams(
            dimension_semantics=("arbitrary",), vmem_limit_bytes=VMEM_LIMIT_BYTES),
        name="adaln",
    )(c_pad, w, b)


def _ffn_kernel(*refs, mod_base, epilogue, mix_gate_row, last_cols):
    x_ref, mod_ref, nw_ref, nw2_ref, wab_ref, wo_ref, *rest = refs
    if mix_gate_row is not None:
        oa_ref, ob_ref, wa_ref, wb_ref, *rest = rest
    if epilogue == "prenorm":
        waux_ref, *rest = rest
        xo_ref, ho_ref, aux_ref, h_scr, *rest = rest
    else:
        xo_ref, h_scr, *rest = rest
    xin_ref = rest[0] if mix_gate_row is not None else x_ref
    f = pl.program_id(1)
    last = pl.num_programs(1) - 1
    tm = x_ref.shape[0]

    def prologue():
        shift = mod_ref[mod_base:mod_base + 1, :]
        gain = nw_ref[...] * (1.0 + mod_ref[mod_base + 1:mod_base + 2, :])
        for r0 in range(0, tm, PROLOGUE_ROWS):
            rows = slice(r0, r0 + PROLOGUE_ROWS)
            if mix_gate_row is not None:
                y = (jnp.dot(oa_ref[rows, :], wa_ref[...], preferred_element_type=F32)
                     + jnp.dot(ob_ref[rows, :], wb_ref[...], preferred_element_type=F32))
                xin_ref[rows, :] = x_ref[rows, :] + mod_ref[mix_gate_row:mix_gate_row + 1, :] * y
            for c0 in range(r0, r0 + PROLOGUE_ROWS, NORM_ROWS):
                chunk = slice(c0, c0 + NORM_ROWS)
                h_scr[chunk, :] = (_rms(xin_ref[chunk, :]) * gain + shift).astype(BF16)

    def swiglu_step(first, cols=None):
        cols = wo_ref.shape[0] if cols is None else cols
        zab = jnp.dot(h_scr[...], wab_ref[:, :2 * cols], preferred_element_type=F32)
        pieces = cols // MXU_N
        a = jnp.concatenate([zab[:, (2 * i) * MXU_N:(2 * i + 1) * MXU_N] for i in range(pieces)],
                            axis=1)
        b = jnp.concatenate([zab[:, (2 * i + 1) * MXU_N:(2 * i + 2) * MXU_N]
                             for i in range(pieces)], axis=1)
        act = (a * _sigmoid(a) * b).astype(BF16)
        update = jnp.dot(act, wo_ref[:cols, :], preferred_element_type=F32)
        if first:
            xo_ref[...] = update
        else:
            xo_ref[...] += update

    def finish():
        gate = MACARON_W * mod_ref[mod_base + 2:mod_base + 3, :]
        if epilogue == "prenorm":
            shift2 = mod_ref[mod_base + 3:mod_base + 4, :]
            gain2 = nw2_ref[...] * (1.0 + mod_ref[mod_base + 4:mod_base + 5, :])
        for c0 in range(0, tm, NORM_ROWS):
            chunk = slice(c0, c0 + NORM_ROWS)
            xn = xin_ref[chunk, :] + gate * xo_ref[chunk, :]
            if epilogue == "prenorm":
                xo_ref[chunk, :] = xn
                ho_ref[chunk, :] = (_rms(xn) * gain2 + shift2).astype(BF16)
            elif epilogue == "final":
                xo_ref[chunk, :] = _rms(xn) * nw2_ref[...]
            else:
                xo_ref[chunk, :] = xn
            done = c0 + NORM_ROWS
            if epilogue == "prenorm" and done % AUX_ROWS == 0:
                rows = slice(done - AUX_ROWS, done)
                aux_ref[rows, :] = jnp.dot(ho_ref[rows, :], waux_ref[...],
                                           preferred_element_type=F32).astype(BF16)

    @pl.when(f == 0)
    def _():
        prologue()
        swiglu_step(first=True)

    @pl.when(jnp.logical_and(f > 0, f < last))
    def _():
        swiglu_step(first=False)

    @pl.when(f == last)
    def _():
        swiglu_step(first=False, cols=last_cols)
        finish()


def _ffn(x2d, mod, nw, nw2, wab, wo, *, d_ff_pad, tokens_per_batch, mod_base, epilogue, mix=None,
         mix_gate_row=None, w_aux=None):
    m, d = x2d.shape
    n_f, _, tf2 = wab.shape
    tf = tf2 // 2
    tm = FFN_TM
    assert m % tm == 0 and tokens_per_batch % tm == 0 and wo.shape[0] == n_f * tf
    assert d_ff_pad % MXU_N == 0 and (n_f - 1) * tf < d_ff_pad <= n_f * tf
    last_cols = d_ff_pad - (n_f - 1) * tf
    assert (mix is None) == (mix_gate_row is None) and n_f >= 2
    assert (w_aux is not None) == (epilogue == "prenorm")
    tiles_per_batch = tokens_per_batch // tm
    row_spec = pl.BlockSpec((tm, d), lambda i, f: (i, 0))
    vec_spec = pl.BlockSpec((1, d), lambda i, f: (0, 0))
    in_specs = [
        row_spec,
        pl.BlockSpec((None, N_MOD, d), lambda i, f: (i // tiles_per_batch, 0, 0)),
        vec_spec,
        vec_spec,
        pl.BlockSpec((None, d, 2 * tf), lambda i, f: (f, 0, 0)),
        pl.BlockSpec((tf, d), lambda i, f: (f, 0)),
    ]
    operands = [x2d, mod, nw, nw2, wab, wo]
    scratch = [pltpu.VMEM((tm, d), BF16)]
    if mix is not None:
        oa, ob, wa, wb = mix
        in_specs += [
            pl.BlockSpec((tm, oa.shape[1]), lambda i, f: (i, 0)),
            pl.BlockSpec((tm, ob.shape[1]), lambda i, f: (i, 0)),
            pl.BlockSpec(wa.shape, lambda i, f: (0, 0), pipeline_mode=pl.Buffered(1)),
            pl.BlockSpec(wb.shape, lambda i, f: (0, 0), pipeline_mode=pl.Buffered(1)),
        ]
        operands += [oa, ob, wa, wb]
        scratch.append(pltpu.VMEM((tm, d), F32))
    out_shape = [jax.ShapeDtypeStruct((m, d), F32)]
    out_specs = [row_spec]
    if epilogue == "prenorm":
        in_specs.append(pl.BlockSpec(w_aux.shape, lambda i, f: (0, 0)))
        operands.append(w_aux)
        out_shape += [jax.ShapeDtypeStruct((m, d), BF16),
                      jax.ShapeDtypeStruct((m, w_aux.shape[1]), BF16)]
        out_specs += [row_spec, pl.BlockSpec((tm, w_aux.shape[1]), lambda i, f: (i, 0))]
    return pl.pallas_call(
        functools.partial(_ffn_kernel, mod_base=mod_base, epilogue=epilogue,
                          mix_gate_row=mix_gate_row, last_cols=last_cols),
        grid=(m // tm, n_f),
        in_specs=in_specs,
        out_specs=out_specs,
        out_shape=out_shape,
        scratch_shapes=scratch,
        compiler_params=pltpu.CompilerParams(
            dimension_semantics=("arbitrary", "arbitrary"),
            vmem_limit_bytes=VMEM_LIMIT_BYTES),
        name="ffn_" + epilogue,
    )(*operands)


_LEVEL_HALVES = tuple(CHUNK >> (j + 1) for j in range(CHUNK.bit_length() - 1))
_N_LEVELS = len(_LEVEL_HALVES)
_CS_BLOCKS = _N_LEVELS + 2
_GROUP = 4
_GROUP_ROWS = _GROUP * CHUNK


def _chunk_constants():
    t = np.arange(CHUNK)
    tri = (t[None, :] <= t[:, None]).astype(np.float32)
    blocks = [tri]
    masks = []
    for half in _LEVEL_HALVES:
        ref = (t // (2 * half)) * (2 * half) + half
        if half < SUBLANES // 2:
            blocks.append(tri - tri[ref])
        same_block = (t[:, None] // (2 * half)) == (t[None, :] // (2 * half))
        is_query = (t % (2 * half)) >= half
        masks.append((same_block & is_query[:, None] & ~is_query[None, :]).astype(np.float32))
    masks.append(np.eye(CHUNK, dtype=np.float32))
    cm = np.concatenate(blocks, axis=0)
    cm2 = np.concatenate([cm, cm], axis=1)
    per_half = LANES // CHUNK
    group_masks = np.stack([np.kron(np.eye(per_half, dtype=np.float32), m) for m in masks], axis=0)
    return jnp.asarray(cm2, dtype=BF16), jnp.asarray(group_masks, dtype=F32)


def _mixer_kernel(*refs, tiles_per_head, side_job, **static):
    *io_refs, z_a, z_b, q_scr, k_scr, cs_scr, oi_scr, u_scr, s_scr = refs
    scratch = (q_scr, k_scr, cs_scr, oi_scr, u_scr, s_scr)
    s = pl.program_id(0)
    if side_job is not None:
        *io_refs, side_in, o_ref, side_out = io_refs
        io_refs.append(o_ref)
        side_job(s, side_in, side_out)

    @pl.when(s == 0)
    def _():
        z_b[...] = jnp.zeros_like(z_b)

    @pl.when(lax.rem(jnp.maximum(s - 1, 0), tiles_per_head) == 0)
    def _():
        s_scr[...] = jnp.zeros_like(s_scr)

    @pl.when(lax.rem(s, 2) == 0)
    def _():
        _mixer_tile(io_refs, scratch, z_b, z_a, **static)

    @pl.when(lax.rem(s, 2) == 1)
    def _():
        _mixer_tile(io_refs, scratch, z_a, z_b, **static)


def _mixer_tile(io_refs, scratch, z_scr, z_next, *, kind, layer, dk, dv, tc):
    if kind == "gla":
        h_ref, w_ref, gr_ref, w2_ref, b2_ref, nw_ref, cm_ref, mk_ref, o_ref = io_refs
    else:
        h_ref, w_ref, lb_ref, nw_ref, cm_ref, mk_ref, o_ref = io_refs
    q_scr, k_scr, cs_scr, oi_scr, u_scr, s_scr = scratch
    n_chunks = tc // CHUNK
    v_cols = slice(2 * dk, 2 * dk + dv)
    gate_cols = slice(2 * dk + dv, 2 * dk + 2 * dv)

    ncols = w_ref.shape[1]
    pieces = [slice(c0, min(c0 + MXU_N, ncols)) for c0 in range(0, ncols, MXU_N)]
    n_slots = n_chunks // _GROUP + 2

    def project_pieces(slot):
        for p, cols in enumerate(pieces):
            if p * n_slots // len(pieces) == slot:
                z_next[:, cols] = jnp.dot(h_ref[...], w_ref[:, cols], preferred_element_type=F32)

    project_pieces(0)

    if kind == "gla":
        q_scr[...] = z_scr[:, 0:dk] * (dk ** -0.5)
        k_scr[...] = z_scr[:, dk:2 * dk]
        gp = jnp.dot(gr_ref[...], w2_ref[...], preferred_element_type=F32) + b2_ref[...]
        la = (jnp.minimum(gp, 0.0) - jnp.log(1.0 + jnp.exp(-jnp.abs(gp)))) * (
            1.0 / GLA_GATE_NORMALIZER)
    else:
        raw = lb_ref[...]
        ex = jnp.exp(raw - jnp.max(raw, axis=0, keepdims=True))
        p = ex / jnp.sum(ex, axis=0, keepdims=True)
        lb = jnp.sum(p[0:layer + 1, :], axis=0, keepdims=True)
        hq = z_scr[:, 0:dk]
        fr = z_scr[:, dk:2 * dk]
        q_scr[...] = hq * _sigmoid(hq)
        en = jnp.exp(-jnp.abs(fr))
        one_en = 1.0 + en
        log_sig = jnp.minimum(fr, 0.0) - jnp.log(one_en)
        sig_neg = jnp.where(fr >= 0.0, en, 1.0) / one_en
        la_a = jnp.log(lb)
        la_b = jnp.log(1.0 - lb) + log_sig
        la = jnp.maximum(la_a, la_b) + jnp.log(1.0 + jnp.exp(-jnp.abs(la_a - la_b)))
        k_scr[...] = (1.0 - lb) * sig_neg

    la_wide = jnp.concatenate([la[c * CHUNK:(c + 1) * CHUNK] for c in range(n_chunks)], axis=1)
    la_hi = la_wide.astype(BF16)
    la_lo = (la_wide - la_hi.astype(F32)).astype(BF16)
    cs_mxu = jnp.dot(cm_ref[...], jnp.concatenate([la_hi, la_lo], axis=0),
                     preferred_element_type=F32)
    cs_scr[0:CHUNK, :] = cs_mxu[0:CHUNK]
    mxu_block = 1
    for j, half in enumerate(_LEVEL_HALVES):
        lo = (j + 1) * CHUNK
        if half < SUBLANES // 2:
            cs_scr[lo:lo + CHUNK, :] = cs_mxu[mxu_block * CHUNK:(mxu_block + 1) * CHUNK]
            mxu_block += 1
            continue
        for t0 in range(0, CHUNK, 2 * half):
            cs_scr[lo + t0:lo + t0 + 2 * half, :] = (
                cs_scr[t0:t0 + 2 * half, :] - cs_scr[t0 + half:t0 + half + 1, :])
    lo = (_N_LEVELS + 1) * CHUNK
    cs_scr[lo:lo + CHUNK, :] = cs_scr[CHUNK - 1:CHUNK, :] - cs_scr[0:CHUNK, :]

    def cs_block(block, c):
        return cs_scr[block * CHUNK:(block + 1) * CHUNK, c * dk:(c + 1) * dk]

    for g in range(n_chunks // _GROUP):
        project_pieces(g + 1)
        chunks = range(g * _GROUP, (g + 1) * _GROUP)
        rows = slice(g * _GROUP_ROWS, (g + 1) * _GROUP_ROWS)
        q = q_scr[rows, :]
        k = k_scr[rows, :]
        n_halves = _GROUP_ROWS // LANES
        blks = LANES // ROW_BLK
        qk = jnp.sum(q * k, axis=-1, keepdims=True)
        eye = mk_ref[_N_LEVELS]
        attn = [[eye[r * ROW_BLK:(r + 1) * ROW_BLK] * qk[h * LANES + r * ROW_BLK:
                                                        h * LANES + (r + 1) * ROW_BLK]
                 for r in range(blks)] for h in range(n_halves)]
        for j, half in enumerate(_LEVEL_HALVES):
            d = jnp.concatenate([cs_block(j + 1, c) for c in chunks], axis=0)
            e = jnp.exp(-jnp.abs(d))
            ke = (k * e).astype(BF16)
            mask = mk_ref[j]
            if half >= ROW_BLK:
                q_runs = [r0 for r0 in range(0, _GROUP_ROWS, ROW_BLK) if (r0 // half) % 2 == 1]
                qe = jnp.concatenate([q[r0:r0 + ROW_BLK] * e[r0:r0 + ROW_BLK] for r0 in q_runs],
                                     axis=0).astype(BF16)
                s = lax.dot_general(qe, ke, _NT, preferred_element_type=F32)
                for n, r0 in enumerate(q_runs):
                    h, r = r0 // LANES, (r0 % LANES) // ROW_BLK
                    attn[h][r] = attn[h][r] + (
                        mask[r * ROW_BLK:(r + 1) * ROW_BLK]
                        * s[n * ROW_BLK:(n + 1) * ROW_BLK, h * LANES:(h + 1) * LANES])
            else:
                s = lax.dot_general((q * e).astype(BF16), ke, _NT, preferred_element_type=F32)
                for h in range(n_halves):
                    for r in range(blks):
                        r0 = h * LANES + r * ROW_BLK
                        attn[h][r] = attn[h][r] + (mask[r * ROW_BLK:(r + 1) * ROW_BLK]
                                                   * s[r0:r0 + ROW_BLK, h * LANES:(h + 1) * LANES])
        vb = z_scr[rows, v_cols].astype(BF16)
        for h in range(n_halves):
            a = jnp.concatenate(attn[h], axis=0).astype(BF16)
            oi_scr[rows.start + h * LANES:rows.start + (h + 1) * LANES, :] = jnp.dot(
                a, vb[h * LANES:(h + 1) * LANES], preferred_element_type=F32)
        for i, c in enumerate(chunks):
            crow = slice(c * CHUNK, (c + 1) * CHUNK)
            kd = (k_scr[crow, :] * jnp.exp(cs_block(_N_LEVELS + 1, c))).astype(BF16)
            u_scr[c] = lax.dot_general(vb[i * CHUNK:(i + 1) * CHUNK], kd, _TN,
                                       preferred_element_type=F32)

    project_pieces(n_slots - 1)
    st = s_scr[...]
    for c in range(n_chunks):
        rows = slice(c * CHUNK, (c + 1) * CHUNK)
        b = cs_block(0, c)
        qb = (q_scr[rows, :] * jnp.exp(b)).astype(BF16)
        o = lax.dot_general(qb, st.astype(BF16), _NT, preferred_element_type=F32) + oi_scr[rows, :]
        st = st * jnp.exp(b[CHUNK - 1:CHUNK, :]) + u_scr[c]
        gate = z_scr[rows, gate_cols]
        o_ref[rows, :] = (_rms(o) * nw_ref[...] * (gate * _sigmoid(gate))).astype(BF16)
    s_scr[...] = st


def _side_wi(s, w_ref, o_ref, *, n_slabs, d_ff, n_f, tf):
    @pl.when(s < n_slabs)
    def _():
        _wi_prep_kernel(w_ref, o_ref, d_ff=d_ff, n_f=n_f, tf=tf)


def _side_wo(s, w_ref, o_ref, *, n_src, n_out):
    @pl.when(s < n_src)
    def _():
        o_ref[...] = w_ref[...].astype(BF16)

    @pl.when(jnp.logical_and(s >= n_src, s < n_out))
    def _():
        o_ref[...] = jnp.zeros_like(o_ref)


def _side_job_specs(side, n_steps):
    what, w = side
    if what == "wi":
        d, d_ff = w.shape[0], w.shape[1] // 2
        n_f = -(-d_ff // FFN_TF)
        tf = FFN_TF
        n_slabs = d // SIDE_ROWS
        assert d % SIDE_ROWS == 0 and n_slabs <= n_steps and d_ff % LANES == 0
        slab = lambda s: jnp.minimum(s, n_slabs - 1)
        return (functools.partial(_side_wi, n_slabs=n_slabs, d_ff=d_ff, n_f=n_f, tf=tf),
                pl.BlockSpec((SIDE_ROWS, 2 * d_ff), lambda s: (slab(s), 0)),
                pl.BlockSpec((n_f, SIDE_ROWS, 2 * tf), lambda s: (0, slab(s), 0)),
                jax.ShapeDtypeStruct((n_f, d, 2 * tf), BF16), w)
    assert what == "wo"
    d_ff, d = w.shape
    n_src = d_ff // LANES
    n_out = -(-d_ff // FFN_TF) * FFN_TF // LANES
    assert d_ff % LANES == 0 and n_out <= n_steps
    return (functools.partial(_side_wo, n_src=n_src, n_out=n_out),
            pl.BlockSpec((LANES, d), lambda s: (jnp.minimum(s, n_src - 1), 0)),
            pl.BlockSpec((LANES, d), lambda s: (jnp.minimum(s, n_out - 1), 0)),
            jax.ShapeDtypeStruct((n_out * LANES, d), BF16), w)


def _mixer(h2, w_heads, extra, nw, consts, *, kind, layer, batch, tokens_per_batch, dk, dv,
           side=None):
    m, d = h2.shape
    n_heads, _, ncols = w_heads.shape
    tc = MIX_TC
    assert tokens_per_batch % tc == 0 and tc % _GROUP_ROWS == 0
    nt = tokens_per_batch // tc
    n_tiles = batch * n_heads * nt
    cm, mk = consts

    def coords(tile):
        bb = tile // (n_heads * nt)
        hh = lax.rem(tile // nt, n_heads)
        return bb * nt + lax.rem(tile, nt), hh

    def projected(s):
        return coords(jnp.minimum(s, n_tiles - 1))

    def consumed(s):
        return coords(jnp.maximum(s - 1, 0))

    const2 = lambda s: (0, 0)
    in_specs = [
        pl.BlockSpec((tc, d), lambda s: (projected(s)[0], 0)),
        pl.BlockSpec((None, d, ncols), lambda s: (projected(s)[1], 0, 0)),
    ]
    if kind == "gla":
        gr, w2, b2 = extra
        in_specs += [
            pl.BlockSpec((tc, LANES), lambda s: (consumed(s)[0], 0)),
            pl.BlockSpec((None, LANES, dk), lambda s: (consumed(s)[1], 0, 0)),
            pl.BlockSpec((None, 1, dk), lambda s: (consumed(s)[1], 0, 0)),
        ]
    else:
        (lbraw,) = extra
        in_specs += [pl.BlockSpec((None, lbraw.shape[1], dk), lambda s: (consumed(s)[1], 0, 0))]
    in_specs += [
        pl.BlockSpec((1, dv), const2),
        pl.BlockSpec(cm.shape, const2),
        pl.BlockSpec(mk.shape, lambda s: (0, 0, 0)),
    ]
    operands = [h2, w_heads, *extra, nw, cm, mk]
    out_specs = [pl.BlockSpec((tc, dv), lambda s: consumed(s))]
    out_shape = [jax.ShapeDtypeStruct((m, n_heads * dv), BF16)]
    side_job = None
    if side is not None:
        side_job, side_in_spec, side_out_spec, side_shape, side_operand = _side_job_specs(
            side, n_tiles + 1)
        in_specs.append(side_in_spec)
        operands.append(side_operand)
        out_specs.append(side_out_spec)
        out_shape.append(side_shape)
    return pl.pallas_call(
        functools.partial(_mixer_kernel, tiles_per_head=nt, side_job=side_job, kind=kind,
                          layer=layer, dk=dk, dv=dv, tc=tc),
        grid=(n_tiles + 1,),
        in_specs=in_specs,
        out_specs=out_specs,
        out_shape=out_shape,
        scratch_shapes=[
            pltpu.VMEM((tc, ncols), F32),
            pltpu.VMEM((tc, ncols), F32),
            pltpu.VMEM((tc, dk), F32),
            pltpu.VMEM((tc, dk), F32),
            pltpu.VMEM((_CS_BLOCKS * CHUNK, (tc // CHUNK) * dk), F32),
            pltpu.VMEM((tc, dv), F32),
            pltpu.VMEM((tc // CHUNK, dv, dk), F32),
            pltpu.VMEM((dv, dk), F32),
        ],
        compiler_params=pltpu.CompilerParams(
            dimension_semantics=("arbitrary",), vmem_limit_bytes=VMEM_LIMIT_BYTES),
        name="mixer_" + kind,
    )(*operands)


def _wi_prep_kernel(w_ref, o_ref, *, d_ff, n_f, tf):
    rows = w_ref.shape[0]
    for j in range(n_f):
        for i in range(tf // MXU_N):
            c0 = j * tf + i * MXU_N
            valid = max(0, min(MXU_N, d_ff - c0))
            for half in range(2):
                src = half * d_ff + c0
                parts = []
                if valid:
                    parts.append(w_ref[:, src:src + valid].astype(BF16))
                if valid < MXU_N:
                    parts.append(jnp.zeros((rows, MXU_N - valid), BF16))
                piece = parts[0] if len(parts) == 1 else jnp.concatenate(parts, axis=1)
                o_ref[j, :, (2 * i + half) * MXU_N:(2 * i + half + 1) * MXU_N] = piece


def _wo_prep_kernel(w_ref, *rest, n_full):
    *tail_refs, o_ref = rest
    r = pl.program_id(0)

    @pl.when(r < n_full)
    def _():
        o_ref[...] = w_ref[...].astype(BF16)

    @pl.when(r >= n_full)
    def _():
        rows = 0
        for t_ref in tail_refs:
            o_ref[rows:rows + LANES, :] = t_ref[...].astype(BF16)
            rows += LANES
        o_ref[rows:, :] = jnp.zeros((o_ref.shape[0] - rows, o_ref.shape[1]), BF16)


def _prep_ffn_weights(wi, wo, tf):
    d, d_ff = wi.shape[0], wo.shape[0]
    assert d_ff % LANES == 0 and d % WPREP_ROWS == 0
    n_f = -(-d_ff // tf)
    wab = pl.pallas_call(
        functools.partial(_wi_prep_kernel, d_ff=d_ff, n_f=n_f, tf=tf),
        grid=(d // WPREP_ROWS,),
        in_specs=[pl.BlockSpec((WPREP_ROWS, 2 * d_ff), lambda r: (r, 0))],
        out_specs=pl.BlockSpec((n_f, WPREP_ROWS, 2 * tf), lambda r: (0, r, 0)),
        out_shape=jax.ShapeDtypeStruct((n_f, d, 2 * tf), BF16),
        compiler_params=pltpu.CompilerParams(
            dimension_semantics=("arbitrary",), vmem_limit_bytes=VMEM_LIMIT_BYTES),
        name="wi_prep",
    )(wi)
    n_full = d_ff // tf
    n_tail = (d_ff - n_full * tf) // LANES
    tail0 = n_full * tf // LANES
    tail_specs = [pl.BlockSpec((LANES, d), functools.partial(lambda r, k: (tail0 + k, 0), k=k))
                  for k in range(n_tail)]
    wob = pl.pallas_call(
        functools.partial(_wo_prep_kernel, n_full=n_full),
        grid=(n_f,),
        in_specs=[pl.BlockSpec((tf, d), lambda r: (jnp.minimum(r, n_full - 1), 0))]
        + tail_specs,
        out_specs=pl.BlockSpec((tf, d), lambda r: (r, 0)),
        out_shape=jax.ShapeDtypeStruct((n_f * tf, d), BF16),
        compiler_params=pltpu.CompilerParams(
            dimension_semantics=("arbitrary",), vmem_limit_bytes=VMEM_LIMIT_BYTES),
        name="wo_prep",
    )(wo, *([wo] * n_tail))
    return wab, wob


_GLA_QK = GLA_HEADS * GLA_DK
_GLA_V = GLA_HEADS * GLA_DV
_HGRN_K = HGRN_HEADS * HGRN_DK
_HGRN_V = HGRN_HEADS * HGRN_DV
_IN_OFFS = tuple(int(o) for o in np.cumsum(
    [0, _GLA_QK, _GLA_QK, _GLA_V, _GLA_V, GLA_GATE_RANK, _HGRN_K, _HGRN_K, _HGRN_V, _HGRN_V]))


def _win_prep_kernel(wt_ref, og_ref, oh_ref, ogr_ref):
    def cols(part, h, width):
        f0 = _IN_OFFS[part] + h * width
        return wt_ref[f0:f0 + width, :].T.astype(BF16)

    gr = wt_ref[_IN_OFFS[4]:_IN_OFFS[5], :]
    gr = jnp.concatenate([gr, jnp.zeros((LANES - GLA_GATE_RANK, gr.shape[1]), F32)], axis=0)
    ogr_ref[...] = gr.T.astype(BF16)
    for h in range(GLA_HEADS):
        og_ref[h] = jnp.concatenate(
            [cols(0, h, GLA_DK), cols(1, h, GLA_DK), cols(2, h, GLA_DV), cols(3, h, GLA_DV)],
            axis=1)
    for h in range(HGRN_HEADS):
        oh_ref[h] = jnp.concatenate(
            [cols(5, h, HGRN_DK), cols(6, h, HGRN_DK), cols(7, h, HGRN_DV), cols(8, h, HGRN_DV)],
            axis=1)


def _prep_mixer_weights(w_in, layer, w2, b2):
    _, d, in_width = w_in.shape
    assert in_width == _IN_OFFS[-1] and d % WPREP_ROWS == 0
    gla_cols = 2 * GLA_DK + 2 * GLA_DV
    hgrn_cols = 2 * HGRN_DK + 2 * HGRN_DV
    wt = jnp.swapaxes(w_in, 1, 2)
    w_gla, w_hg, w_gr = pl.pallas_call(
        _win_prep_kernel,
        grid=(d // WPREP_ROWS,),
        in_specs=[pl.BlockSpec((None, in_width, WPREP_ROWS), lambda r: (layer, 0, r))],
        out_specs=[pl.BlockSpec((GLA_HEADS, WPREP_ROWS, gla_cols), lambda r: (0, r, 0)),
                   pl.BlockSpec((HGRN_HEADS, WPREP_ROWS, hgrn_cols), lambda r: (0, r, 0)),
                   pl.BlockSpec((WPREP_ROWS, LANES), lambda r: (r, 0))],
        out_shape=[jax.ShapeDtypeStruct((GLA_HEADS, d, gla_cols), BF16),
                   jax.ShapeDtypeStruct((HGRN_HEADS, d, hgrn_cols), BF16),
                   jax.ShapeDtypeStruct((d, LANES), BF16)],
        compiler_params=pltpu.CompilerParams(
            dimension_semantics=("arbitrary",), vmem_limit_bytes=VMEM_LIMIT_BYTES),
        name="win_prep",
    )(wt)
    w2h = jnp.pad(w2.astype(BF16), ((0, LANES - GLA_GATE_RANK), (0, 0)))
    w2h = w2h.reshape(LANES, GLA_HEADS, GLA_DK).transpose(1, 0, 2)
    b2h = b2.reshape(GLA_HEADS, 1, GLA_DK)
    return w_gla, w_hg, w_gr, w2h, b2h


def kernel(x, c, ada_w, ada_b, norm_ffn1_w, ffn1_wi, ffn1_wo, norm_mix_w, w_in, gla_gate_w2,
           gla_gate_b2, gla_norm_w, hgrn_norm_w, hgrn_lower_bounds, w_out, norm_ffn2_w, ffn2_wi,
           ffn2_wo, final_norm_w):
    batch, seq, d = x.shape
    depth = ada_w.shape[0]
    m = batch * seq
    consts = _chunk_constants()
    xc = x.reshape(m, d)
    c_pad = jnp.pad(c, ((0, SUBLANES - batch % SUBLANES), (0, 0))) if batch % SUBLANES else c
    gla_v = GLA_HEADS * GLA_DV
    lb_heads = hgrn_lower_bounds.astype(F32).reshape(depth + 1, HGRN_HEADS, HGRN_DK).transpose(1, 0, 2)

    for l in range(depth):
        mod = _adaln(c_pad, ada_w[l], ada_b[l][None, :])[:batch].reshape(batch, N_MOD, d)
        d_ff_pad = -(-ffn1_wo.shape[1] // FFN_PAD) * FFN_PAD
        wab1, wo1 = _prep_ffn_weights(ffn1_wi[l], ffn1_wo[l], FFN_TF_WIDE)
        w_gla, w_hg, w_gr, w2h, b2h = _prep_mixer_weights(w_in, l, gla_gate_w2[l], gla_gate_b2[l])
        wout = w_out[l].astype(BF16)

        x1, h2, gr = _ffn(xc, mod, norm_ffn1_w[l][None, :], norm_mix_w[l][None, :], wab1, wo1,
                          d_ff_pad=d_ff_pad, tokens_per_batch=seq, mod_base=0, epilogue="prenorm",
                          w_aux=w_gr)
        o_gla, wo2 = _mixer(h2, w_gla, (gr, w2h, b2h), gla_norm_w[l][None, :], consts,
                            kind="gla", layer=l, batch=batch, tokens_per_batch=seq, dk=GLA_DK,
                            dv=GLA_DV, side=("wo", ffn2_wo[l]))
        o_hg, wab2 = _mixer(h2, w_hg, (lb_heads,), hgrn_norm_w[l][None, :], consts,
                            kind="hgrn", layer=l, batch=batch, tokens_per_batch=seq, dk=HGRN_DK,
                            dv=HGRN_DV, side=("wi", ffn2_wi[l]))
        last = l == depth - 1
        nw2 = final_norm_w[None, :] if last else norm_ffn2_w[l][None, :]
        (xc,) = _ffn(x1, mod, norm_ffn2_w[l][None, :], nw2, wab2, wo2, d_ff_pad=d_ff_pad,
                     tokens_per_batch=seq, mod_base=6, epilogue="final" if last else "none",
                     mix=(o_gla, o_hg, wout[:gla_v], wout[gla_v:]), mix_gate_row=5)
    return xc.reshape(batch, seq, d)
```

```python
import functools

import jax
import jax.numpy as jnp
import numpy as np
from jax import lax
from jax.experimental import pallas as pl
from jax.experimental.pallas import tpu as pltpu

F32 = jnp.float32
BF16 = jnp.bfloat16

GLA_HEADS = 4
GLA_DK = 128
GLA_DV = 256
GLA_GATE_RANK = 16
GLA_GATE_NORMALIZER = 16.0
HGRN_HEADS = 8
HGRN_DK = 128
HGRN_DV = 128
CHUNK = 64
MACARON_W = 0.5
N_MOD = 9
EPS = 1e-6

LANES = 128
SUBLANES = 8
MXU_N = 256
VMEM_LIMIT_BYTES = 56 * 1024 * 1024

FFN_TM = 512
FFN_TF = 768
FFN_TF_WIDE = 1024
FFN_PAD = 512
PROLOGUE_ROWS = 256
NORM_ROWS = 16
AUX_ROWS = 128
MIX_TC = 1024
ADALN_TN = 1024
WPREP_ROWS = 256
SIDE_ROWS = 16
ROW_BLK = 8

_NT = (((1,), (1,)), ((), ()))
_TN = (((0,), (0,)), ((), ()))


def _sigmoid(x):
    return jax.nn.sigmoid(x)


def _rms(x):
    return x * lax.rsqrt(jnp.mean(x * x, axis=-1, keepdims=True) + EPS)


def _adaln_kernel(c_ref, w_ref, b_ref, o_ref):
    c = c_ref[...]
    ca = (c * _sigmoid(c)).astype(BF16)
    o_ref[...] = jnp.dot(ca, w_ref[...].astype(BF16), preferred_element_type=F32) + b_ref[...]


def _adaln(c_pad, w, b):
    rows, d = c_pad.shape
    n = w.shape[1]
    assert n % ADALN_TN == 0
    return pl.pallas_call(
        _adaln_kernel,
        grid=(n // ADALN_TN,),
        in_specs=[
            pl.BlockSpec((rows, d), lambda j: (0, 0)),
            pl.BlockSpec((d, ADALN_TN), lambda j: (0, j)),
            pl.BlockSpec((1, ADALN_TN), lambda j: (0, j)),
        ],
        out_specs=pl.BlockSpec((rows, ADALN_TN), lambda j: (0, j)),
        out_shape=jax.ShapeDtypeStruct((rows, n), F32),
        compiler_params=pltpu.CompilerParams(
            dimension_semantics=("arbitrary",), vmem_limit_bytes=VMEM_LIMIT_BYTES),
        name="adaln",
    )(c_pad, w, b)


def _ffn_kernel(*refs, mod_base, epilogue, mix_gate_row, last_cols):
    x_ref, mod_ref, nw_ref, nw2_ref, wab_ref, wo_ref, *rest = refs
    if mix_gate_row is not None:
        oa_ref, ob_ref, wa_ref, wb_ref, *rest = rest
    if epilogue == "prenorm":
        waux_ref, *rest = rest
        xo_ref, ho_ref, aux_ref, h_scr, *rest = rest
    else:
        xo_ref, h_scr, *rest = rest
    xin_ref = rest[0] if mix_gate_row is not None else x_ref
    f = pl.program_id(1)
    last = pl.num_programs(1) - 1
    tm = x_ref.shape[0]

    def prologue():
        shift = mod_ref[mod_base:mod_base + 1, :]
        gain = nw_ref[...] * (1.0 + mod_ref[mod_base + 1:mod_base + 2, :])
        for r0 in range(0, tm, PROLOGUE_ROWS):
            rows = slice(r0, r0 + PROLOGUE_ROWS)
            if mix_gate_row is not None:
                y = (jnp.dot(oa_ref[rows, :], wa_ref[...], preferred_element_type=F32)
                     + jnp.dot(ob_ref[rows, :], wb_ref[...], preferred_element_type=F32))
                xin_ref[rows, :] = x_ref[rows, :] + mod_ref[mix_gate_row:mix_gate_row + 1, :] * y
            for c0 in range(r0, r0 + PROLOGUE_ROWS, NORM_ROWS):
                chunk = slice(c0, c0 + NORM_ROWS)
                h_scr[chunk, :] = (_rms(xin_ref[chunk, :]) * gain + shift).astype(BF16)

    def swiglu_step(first, cols=None):
        cols = wo_ref.shape[0] if cols is None else cols
        zab = jnp.dot(h_scr[...], wab_ref[:, :2 * cols], preferred_element_type=F32)
        pieces = cols // MXU_N
        a = jnp.concatenate([zab[:, (2 * i) * MXU_N:(2 * i + 1) * MXU_N] for i in range(pieces)],
                            axis=1)
        b = jnp.concatenate([zab[:, (2 * i + 1) * MXU_N:(2 * i + 2) * MXU_N]
                             for i in range(pieces)], axis=1)
        act = (a * _sigmoid(a) * b).astype(BF16)
        update = jnp.dot(act, wo_ref[:cols, :], preferred_element_type=F32)
        if first:
            xo_ref[...] = update
        else:
            xo_ref[...] += update

    def finish():
        gate = MACARON_W * mod_ref[mod_base + 2:mod_base + 3, :]
        if epilogue == "prenorm":
            shift2 = mod_ref[mod_base + 3:mod_base + 4, :]
            gain2 = nw2_ref[...] * (1.0 + mod_ref[mod_base + 4:mod_base + 5, :])
        for c0 in range(0, tm, NORM_ROWS):
            chunk = slice(c0, c0 + NORM_ROWS)
            xn = xin_ref[chunk, :] + gate * xo_ref[chunk, :]
            if epilogue == "prenorm":
                xo_ref[chunk, :] = xn
                ho_ref[chunk, :] = (_rms(xn) * gain2 + shift2).astype(BF16)
            elif epilogue == "final":
                xo_ref[chunk, :] = _rms(xn) * nw2_ref[...]
            else:
                xo_ref[chunk, :] = xn
            done = c0 + NORM_ROWS
            if epilogue == "prenorm" and done % AUX_ROWS == 0:
                rows = slice(done - AUX_ROWS, done)
                aux_ref[rows, :] = jnp.dot(ho_ref[rows, :], waux_ref[...],
                                           preferred_element_type=F32).astype(BF16)

    @pl.when(f == 0)
    def _():
        prologue()
        swiglu_step(first=True)

    @pl.when(jnp.logical_and(f > 0, f < last))
    def _():
        swiglu_step(first=False)

    @pl.when(f == last)
    def _():
        swiglu_step(first=False, cols=last_cols)
        finish()


def _ffn(x2d, mod, nw, nw2, wab, wo, *, d_ff_pad, tokens_per_batch, mod_base, epilogue, mix=None,
         mix_gate_row=None, w_aux=None):
    m, d = x2d.shape
    n_f, _, tf2 = wab.shape
    tf = tf2 // 2
    tm = FFN_TM
    assert m % tm == 0 and tokens_per_batch % tm == 0 and wo.shape[0] == n_f * tf
    assert d_ff_pad % MXU_N == 0 and (n_f - 1) * tf < d_ff_pad <= n_f * tf
    last_cols = d_ff_pad - (n_f - 1) * tf
    assert (mix is None) == (mix_gate_row is None) and n_f >= 2
    assert (w_aux is not None) == (epilogue == "prenorm")
    tiles_per_batch = tokens_per_batch // tm
    row_spec = pl.BlockSpec((tm, d), lambda i, f: (i, 0))
    vec_spec = pl.BlockSpec((1, d), lambda i, f: (0, 0))
    in_specs = [
        row_spec,
        pl.BlockSpec((None, N_MOD, d), lambda i, f: (i // tiles_per_batch, 0, 0)),
        vec_spec,
        vec_spec,
        pl.BlockSpec((None, d, 2 * tf), lambda i, f: (f, 0, 0)),
        pl.BlockSpec((tf, d), lambda i, f: (f, 0)),
    ]
    operands = [x2d, mod, nw, nw2, wab, wo]
    scratch = [pltpu.VMEM((tm, d), BF16)]
    if mix is not None:
        oa, ob, wa, wb = mix
        in_specs += [
            pl.BlockSpec((tm, oa.shape[1]), lambda i, f: (i, 0)),
            pl.BlockSpec((tm, ob.shape[1]), lambda i, f: (i, 0)),
            pl.BlockSpec(wa.shape, lambda i, f: (0, 0), pipeline_mode=pl.Buffered(1)),
            pl.BlockSpec(wb.shape, lambda i, f: (0, 0), pipeline_mode=pl.Buffered(1)),
        ]
        operands += [oa, ob, wa, wb]
        scratch.append(pltpu.VMEM((tm, d), F32))
    out_shape = [jax.ShapeDtypeStruct((m, d), F32)]
    out_specs = [row_spec]
    if epilogue == "prenorm":
        in_specs.append(pl.BlockSpec(w_aux.shape, lambda i, f: (0, 0)))
        operands.append(w_aux)
        out_shape += [jax.ShapeDtypeStruct((m, d), BF16),
                      jax.ShapeDtypeStruct((m, w_aux.shape[1]), BF16)]
        out_specs += [row_spec, pl.BlockSpec((tm, w_aux.shape[1]), lambda i, f: (i, 0))]
    return pl.pallas_call(
        functools.partial(_ffn_kernel, mod_base=mod_base, epilogue=epilogue,
                          mix_gate_row=mix_gate_row, last_cols=last_cols),
        grid=(m // tm, n_f),
        in_specs=in_specs,
        out_specs=out_specs,
        out_shape=out_shape,
        scratch_shapes=scratch,
        compiler_params=pltpu.CompilerParams(
            dimension_semantics=("arbitrary", "arbitrary"),
            vmem_limit_bytes=VMEM_LIMIT_BYTES),
        name="ffn_" + epilogue,
    )(*operands)


_LEVEL_HALVES = tuple(CHUNK >> (j + 1) for j in range(CHUNK.bit_length() - 1))
_N_LEVELS = len(_LEVEL_HALVES)
_CS_BLOCKS = _N_LEVELS + 2
_GROUP = 4
_GROUP_ROWS = _GROUP * CHUNK


def _chunk_constants():
    t = np.arange(CHUNK)
    tri = (t[None, :] <= t[:, None]).astype(np.float32)
    blocks = [tri]
    masks = []
    for half in _LEVEL_HALVES:
        ref = (t // (2 * half)) * (2 * half) + half
        if half < SUBLANES // 2:
            blocks.append(tri - tri[ref])
        same_block = (t[:, None] // (2 * half)) == (t[None, :] // (2 * half))
        is_query = (t % (2 * half)) >= half
        masks.append((same_block & is_query[:, None] & ~is_query[None, :]).astype(np.float32))
    masks.append(np.eye(CHUNK, dtype=np.float32))
    cm = np.concatenate(blocks, axis=0)
    cm2 = np.concatenate([cm, cm], axis=1)
    per_half = LANES // CHUNK
    group_masks = np.stack([np.kron(np.eye(per_half, dtype=np.float32), m) for m in masks], axis=0)
    return jnp.asarray(cm2, dtype=BF16), jnp.asarray(group_masks, dtype=F32)


def _mixer_kernel(*refs, tiles_per_head, side_job, **static):
    *io_refs, z_a, z_b, q_scr, k_scr, cs_scr, oi_scr, u_scr, s_scr = refs
    scratch = (q_scr, k_scr, cs_scr, oi_scr, u_scr, s_scr)
    s = pl.program_id(0)
    if side_job is not None:
        *io_refs, side_in, o_ref, side_out = io_refs
        io_refs.append(o_ref)
        side_job(s, side_in, side_out)

    @pl.when(s == 0)
    def _():
        z_b[...] = jnp.zeros_like(z_b)

    @pl.when(lax.rem(jnp.maximum(s - 1, 0), tiles_per_head) == 0)
    def _():
        s_scr[...] = jnp.zeros_like(s_scr)

    @pl.when(lax.rem(s, 2) == 0)
    def _():
        _mixer_tile(io_refs, scratch, z_b, z_a, **static)

    @pl.when(lax.rem(s, 2) == 1)
    def _():
        _mixer_tile(io_refs, scratch, z_a, z_b, **static)


def _mixer_tile(io_refs, scratch, z_scr, z_next, *, kind, layer, dk, dv, tc):
    if kind == "gla":
        h_ref, w_ref, gr_ref, w2_ref, b2_ref, nw_ref, cm_ref, mk_ref, o_ref = io_refs
    else:
        h_ref, w_ref, lb_ref, nw_ref, cm_ref, mk_ref, o_ref = io_refs
    q_scr, k_scr, cs_scr, oi_scr, u_scr, s_scr = scratch
    n_chunks = tc // CHUNK
    v_cols = slice(2 * dk, 2 * dk + dv)
    gate_cols = slice(2 * dk + dv, 2 * dk + 2 * dv)

    ncols = w_ref.shape[1]
    pieces = [slice(c0, min(c0 + MXU_N, ncols)) for c0 in range(0, ncols, MXU_N)]
    n_slots = n_chunks // _GROUP + 2

    def project_pieces(slot):
        for p, cols in enumerate(pieces):
            if p * n_slots // len(pieces) == slot:
                z_next[:, cols] = jnp.dot(h_ref[...], w_ref[:, cols], preferred_element_type=F32)

    project_pieces(0)

    if kind == "gla":
        q_scr[...] = z_scr[:, 0:dk] * (dk ** -0.5)
        k_scr[...] = z_scr[:, dk:2 * dk]
        gp = jnp.dot(gr_ref[...], w2_ref[...], preferred_element_type=F32) + b2_ref[...]
        la = (jnp.minimum(gp, 0.0) - jnp.log(1.0 + jnp.exp(-jnp.abs(gp)))) * (
            1.0 / GLA_GATE_NORMALIZER)
    else:
        raw = lb_ref[...]
        ex = jnp.exp(raw - jnp.max(raw, axis=0, keepdims=True))
        p = ex / jnp.sum(ex, axis=0, keepdims=True)
        lb = jnp.sum(p[0:layer + 1, :], axis=0, keepdims=True)
        hq = z_scr[:, 0:dk]
        fr = z_scr[:, dk:2 * dk]
        q_scr[...] = hq * _sigmoid(hq)
        en = jnp.exp(-jnp.abs(fr))
        one_en = 1.0 + en
        log_sig = jnp.minimum(fr, 0.0) - jnp.log(one_en)
        sig_neg = jnp.where(fr >= 0.0, en, 1.0) / one_en
        la_a = jnp.log(lb)
        la_b = jnp.log(1.0 - lb) + log_sig
        la = jnp.maximum(la_a, la_b) + jnp.log(1.0 + jnp.exp(-jnp.abs(la_a - la_b)))
        k_scr[...] = (1.0 - lb) * sig_neg

    la_wide = jnp.concatenate([la[c * CHUNK:(c + 1) * CHUNK] for c in range(n_chunks)], axis=1)
    la_hi = la_wide.astype(BF16)
    la_lo = (la_wide - la_hi.astype(F32)).astype(BF16)
    cs_mxu = jnp.dot(cm_ref[...], jnp.concatenate([la_hi, la_lo], axis=0),
                     preferred_element_type=F32)
    cs_scr[0:CHUNK, :] = cs_mxu[0:CHUNK]
    mxu_block = 1
    for j, half in enumerate(_LEVEL_HALVES):
        lo = (j + 1) * CHUNK
        if half < SUBLANES // 2:
            cs_scr[lo:lo + CHUNK, :] = cs_mxu[mxu_block * CHUNK:(mxu_block + 1) * CHUNK]
            mxu_block += 1
            continue
        for t0 in range(0, CHUNK, 2 * half):
            cs_scr[lo + t0:lo + t0 + 2 * half, :] = (
                cs_scr[t0:t0 + 2 * half, :] - cs_scr[t0 + half:t0 + half + 1, :])
    lo = (_N_LEVELS + 1) * CHUNK
    cs_scr[lo:lo + CHUNK, :] = cs_scr[CHUNK - 1:CHUNK, :] - cs_scr[0:CHUNK, :]

    def cs_block(block, c):
        return cs_scr[block * CHUNK:(block + 1) * CHUNK, c * dk:(c + 1) * dk]

    for g in range(n_chunks // _GROUP):
        project_pieces(g + 1)
        chunks = range(g * _GROUP, (g + 1) * _GROUP)
        rows = slice(g * _GROUP_ROWS, (g + 1) * _GROUP_ROWS)
        q = q_scr[rows, :]
        k = k_scr[rows, :]
        n_halves = _GROUP_ROWS // LANES
        blks = LANES // ROW_BLK
        qk = jnp.sum(q * k, axis=-1, keepdims=True)
        eye = mk_ref[_N_LEVELS]
        attn = [[eye[r * ROW_BLK:(r + 1) * ROW_BLK] * qk[h * LANES + r * ROW_BLK:
                                                        h * LANES + (r + 1) * ROW_BLK]
                 for r in range(blks)] for h in range(n_halves)]
        for j, half in enumerate(_LEVEL_HALVES):
            d = jnp.concatenate([cs_block(j + 1, c) for c in chunks], axis=0)
            e = jnp.exp(-jnp.abs(d))
            ke = (k * e).astype(BF16)
            mask = mk_ref[j]
            if half >= ROW_BLK:
                q_runs = [r0 for r0 in range(0, _GROUP_ROWS, ROW_BLK) if (r0 // half) % 2 == 1]
                qe = jnp.concatenate([q[r0:r0 + ROW_BLK] * e[r0:r0 + ROW_BLK] for r0 in q_runs],
                                     axis=0).astype(BF16)
                s = lax.dot_general(qe, ke, _NT, preferred_element_type=F32)
                for n, r0 in enumerate(q_runs):
                    h, r = r0 // LANES, (r0 % LANES) // ROW_BLK
                    attn[h][r] = attn[h][r] + (
                        mask[r * ROW_BLK:(r + 1) * ROW_BLK]
                        * s[n * ROW_BLK:(n + 1) * ROW_BLK, h * LANES:(h + 1) * LANES])
            else:
                s = lax.dot_general((q * e).astype(BF16), ke, _NT, preferred_element_type=F32)
                for h in range(n_halves):
                    for r in range(blks):
                        r0 = h * LANES + r * ROW_BLK
                        attn[h][r] = attn[h][r] + (mask[r * ROW_BLK:(r + 1) * ROW_BLK]
                                                   * s[r0:r0 + ROW_BLK, h * LANES:(h + 1) * LANES])
        vb = z_scr[rows, v_cols].astype(BF16)
        for h in range(n_halves):
            a = jnp.concatenate(attn[h], axis=0).astype(BF16)
            oi_scr[rows.start + h * LANES:rows.start + (h + 1) * LANES, :] = jnp.dot(
                a, vb[h * LANES:(h + 1) * LANES], preferred_element_type=F32)
        for i, c in enumerate(chunks):
            crow = slice(c * CHUNK, (c + 1) * CHUNK)
            kd = (k_scr[crow, :] * jnp.exp(cs_block(_N_LEVELS + 1, c))).astype(BF16)
            u_scr[c] = lax.dot_general(vb[i * CHUNK:(i + 1) * CHUNK], kd, _TN,
                                       preferred_element_type=F32)

    project_pieces(n_slots - 1)
    st = s_scr[...]
    for c in range(n_chunks):
        rows = slice(c * CHUNK, (c + 1) * CHUNK)
        b = cs_block(0, c)
        qb = (q_scr[rows, :] * jnp.exp(b)).astype(BF16)
        o = lax.dot_general(qb, st.astype(BF16), _NT, preferred_element_type=F32) + oi_scr[rows, :]
        st = st * jnp.exp(b[CHUNK - 1:CHUNK, :]) + u_scr[c]
        gate = z_scr[rows, gate_cols]
        o_ref[rows, :] = (_rms(o) * nw_ref[...] * (gate * _sigmoid(gate))).astype(BF16)
    s_scr[...] = st


def _side_wi(s, w_ref, o_ref, *, n_slabs, d_ff, n_f, tf):
    @pl.when(s < n_slabs)
    def _():
        _wi_prep_kernel(w_ref, o_ref, d_ff=d_ff, n_f=n_f, tf=tf)


def _side_wo(s, w_ref, o_ref, *, n_src, n_out):
    @pl.when(s < n_src)
    def _():
        o_ref[...] = w_ref[...].astype(BF16)

    @pl.when(jnp.logical_and(s >= n_src, s < n_out))
    def _():
        o_ref[...] = jnp.zeros_like(o_ref)


def _side_job_specs(side, n_steps):
    what, w = side
    if what == "wi":
        d, d_ff = w.shape[0], w.shape[1] // 2
        n_f = -(-d_ff // FFN_TF)
        tf = FFN_TF
        n_slabs = d // SIDE_ROWS
        assert d % SIDE_ROWS == 0 and n_slabs <= n_steps and d_ff % LANES == 0
        slab = lambda s: jnp.minimum(s, n_slabs - 1)
        return (functools.partial(_side_wi, n_slabs=n_slabs, d_ff=d_ff, n_f=n_f, tf=tf),
                pl.BlockSpec((SIDE_ROWS, 2 * d_ff), lambda s: (slab(s), 0)),
                pl.BlockSpec((n_f, SIDE_ROWS, 2 * tf), lambda s: (0, slab(s), 0)),
                jax.ShapeDtypeStruct((n_f, d, 2 * tf), BF16), w)
    assert what == "wo"
    d_ff, d = w.shape
    n_src = d_ff // LANES
    n_out = -(-d_ff // FFN_TF) * FFN_TF // LANES
    assert d_ff % LANES == 0 and n_out <= n_steps
    return (functools.partial(_side_wo, n_src=n_src, n_out=n_out),
            pl.BlockSpec((LANES, d), lambda s: (jnp.minimum(s, n_src - 1), 0)),
            pl.BlockSpec((LANES, d), lambda s: (jnp.minimum(s, n_out - 1), 0)),
            jax.ShapeDtypeStruct((n_out * LANES, d), BF16), w)


def _mixer(h2, w_heads, extra, nw, consts, *, kind, layer, batch, tokens_per_batch, dk, dv,
           side=None):
    m, d = h2.shape
    n_heads, _, ncols = w_heads.shape
    tc = MIX_TC
    assert tokens_per_batch % tc == 0 and tc % _GROUP_ROWS == 0
    nt = tokens_per_batch // tc
    n_tiles = batch * n_heads * nt
    cm, mk = consts

    def coords(tile):
        bb = tile // (n_heads * nt)
        hh = lax.rem(tile // nt, n_heads)
        return bb * nt + lax.rem(tile, nt), hh

    def projected(s):
        return coords(jnp.minimum(s, n_tiles - 1))

    def consumed(s):
        return coords(jnp.maximum(s - 1, 0))

    const2 = lambda s: (0, 0)
    in_specs = [
        pl.BlockSpec((tc, d), lambda s: (projected(s)[0], 0)),
        pl.BlockSpec((None, d, ncols), lambda s: (projected(s)[1], 0, 0)),
    ]
    if kind == "gla":
        gr, w2, b2 = extra
        in_specs += [
            pl.BlockSpec((tc, LANES), lambda s: (consumed(s)[0], 0)),
            pl.BlockSpec((None, LANES, dk), lambda s: (consumed(s)[1], 0, 0)),
            pl.BlockSpec((None, 1, dk), lambda s: (consumed(s)[1], 0, 0)),
        ]
    else:
        (lbraw,) = extra
        in_specs += [pl.BlockSpec((None, lbraw.shape[1], dk), lambda s: (consumed(s)[1], 0, 0))]
    in_specs += [
        pl.BlockSpec((1, dv), const2),
        pl.BlockSpec(cm.shape, const2),
        pl.BlockSpec(mk.shape, lambda s: (0, 0, 0)),
    ]
    operands = [h2, w_heads, *extra, nw, cm, mk]
    out_specs = [pl.BlockSpec((tc, dv), lambda s: consumed(s))]
    out_shape = [jax.ShapeDtypeStruct((m, n_heads * dv), BF16)]
    side_job = None
    if side is not None:
        side_job, side_in_spec, side_out_spec, side_shape, side_operand = _side_job_specs(
            side, n_tiles + 1)
        in_specs.append(side_in_spec)
        operands.append(side_operand)
        out_specs.append(side_out_spec)
        out_shape.append(side_shape)
    return pl.pallas_call(
        functools.partial(_mixer_kernel, tiles_per_head=nt, side_job=side_job, kind=kind,
                          layer=layer, dk=dk, dv=dv, tc=tc),
        grid=(n_tiles + 1,),
        in_specs=in_specs,
        out_specs=out_specs,
        out_shape=out_shape,
        scratch_shapes=[
            pltpu.VMEM((tc, ncols), F32),
            pltpu.VMEM((tc, ncols), F32),
            pltpu.VMEM((tc, dk), F32),
            pltpu.VMEM((tc, dk), F32),
            pltpu.VMEM((_CS_BLOCKS * CHUNK, (tc // CHUNK) * dk), F32),
            pltpu.VMEM((tc, dv), F32),
            pltpu.VMEM((tc // CHUNK, dv, dk), F32),
            pltpu.VMEM((dv, dk), F32),
        ],
        compiler_params=pltpu.CompilerParams(
            dimension_semantics=("arbitrary",), vmem_limit_bytes=VMEM_LIMIT_BYTES),
        name="mixer_" + kind,
    )(*operands)


def _wi_prep_kernel(w_ref, o_ref, *, d_ff, n_f, tf):
    rows = w_ref.shape[0]
    for j in range(n_f):
        for i in range(tf // MXU_N):
            c0 = j * tf + i * MXU_N
            valid = max(0, min(MXU_N, d_ff - c0))
            for half in range(2):
                src = half * d_ff + c0
                parts = []
                if valid:
                    parts.append(w_ref[:, src:src + valid].astype(BF16))
                if valid < MXU_N:
                    parts.append(jnp.zeros((rows, MXU_N - valid), BF16))
                piece = parts[0] if len(parts) == 1 else jnp.concatenate(parts, axis=1)
                o_ref[j, :, (2 * i + half) * MXU_N:(2 * i + half + 1) * MXU_N] = piece


def _wo_prep_kernel(w_ref, *rest, n_full):
    *tail_refs, o_ref = rest
    r = pl.program_id(0)

    @pl.when(r < n_full)
    def _():
        o_ref[...] = w_ref[...].astype(BF16)

    @pl.when(r >= n_full)
    def _():
        rows = 0
        for t_ref in tail_refs:
            o_ref[rows:rows + LANES, :] = t_ref[...].astype(BF16)
            rows += LANES
        o_ref[rows:, :] = jnp.zeros((o_ref.shape[0] - rows, o_ref.shape[1]), BF16)


def _prep_ffn_weights(wi, wo, tf):
    d, d_ff = wi.shape[0], wo.shape[0]
    assert d_ff % LANES == 0 and d % WPREP_ROWS == 0
    n_f = -(-d_ff // tf)
    wab = pl.pallas_call(
        functools.partial(_wi_prep_kernel, d_ff=d_ff, n_f=n_f, tf=tf),
        grid=(d // WPREP_ROWS,),
        in_specs=[pl.BlockSpec((WPREP_ROWS, 2 * d_ff), lambda r: (r, 0))],
        out_specs=pl.BlockSpec((n_f, WPREP_ROWS, 2 * tf), lambda r: (0, r, 0)),
        out_shape=jax.ShapeDtypeStruct((n_f, d, 2 * tf), BF16),
        compiler_params=pltpu.CompilerParams(
            dimension_semantics=("arbitrary",), vmem_limit_bytes=VMEM_LIMIT_BYTES),
        name="wi_prep",
    )(wi)
    n_full = d_ff // tf
    n_tail = (d_ff - n_full * tf) // LANES
    tail0 = n_full * tf // LANES
    tail_specs = [pl.BlockSpec((LANES, d), functools.partial(lambda r, k: (tail0 + k, 0), k=k))
                  for k in range(n_tail)]
    wob = pl.pallas_call(
        functools.partial(_wo_prep_kernel, n_full=n_full),
        grid=(n_f,),
        in_specs=[pl.BlockSpec((tf, d), lambda r: (jnp.minimum(r, n_full - 1), 0))]
        + tail_specs,
        out_specs=pl.BlockSpec((tf, d), lambda r: (r, 0)),
        out_shape=jax.ShapeDtypeStruct((n_f * tf, d), BF16),
        compiler_params=pltpu.CompilerParams(
            dimension_semantics=("arbitrary",), vmem_limit_bytes=VMEM_LIMIT_BYTES),
        name="wo_prep",
    )(wo, *([wo] * n_tail))
    return wab, wob


_GLA_QK = GLA_HEADS * GLA_DK
_GLA_V = GLA_HEADS * GLA_DV
_HGRN_K = HGRN_HEADS * HGRN_DK
_HGRN_V = HGRN_HEADS * HGRN_DV
_IN_OFFS = tuple(int(o) for o in np.cumsum(
    [0, _GLA_QK, _GLA_QK, _GLA_V, _GLA_V, GLA_GATE_RANK, _HGRN_K, _HGRN_K, _HGRN_V, _HGRN_V]))


def _win_prep_kernel(wt_ref, og_ref, oh_ref, ogr_ref):
    def cols(part, h, width):
        f0 = _IN_OFFS[part] + h * width
        return wt_ref[f0:f0 + width, :].T.astype(BF16)

    gr = wt_ref[_IN_OFFS[4]:_IN_OFFS[5], :]
    gr = jnp.concatenate([gr, jnp.zeros((LANES - GLA_GATE_RANK, gr.shape[1]), F32)], axis=0)
    ogr_ref[...] = gr.T.astype(BF16)
    for h in range(GLA_HEADS):
        og_ref[h] = jnp.concatenate(
            [cols(0, h, GLA_DK), cols(1, h, GLA_DK), cols(2, h, GLA_DV), cols(3, h, GLA_DV)],
            axis=1)
    for h in range(HGRN_HEADS):
        oh_ref[h] = jnp.concatenate(
            [cols(5, h, HGRN_DK), cols(6, h, HGRN_DK), cols(7, h, HGRN_DV), cols(8, h, HGRN_DV)],
            axis=1)


def _prep_mixer_weights(w_in, layer, w2, b2):
    _, d, in_width = w_in.shape
    assert in_width == _IN_OFFS[-1] and d % WPREP_ROWS == 0
    gla_cols = 2 * GLA_DK + 2 * GLA_DV
    hgrn_cols = 2 * HGRN_DK + 2 * HGRN_DV
    wt = jnp.swapaxes(w_in, 1, 2)
    w_gla, w_hg, w_gr = pl.pallas_call(
        _win_prep_kernel,
        grid=(d // WPREP_ROWS,),
        in_specs=[pl.BlockSpec((None, in_width, WPREP_ROWS), lambda r: (layer, 0, r))],
        out_specs=[pl.BlockSpec((GLA_HEADS, WPREP_ROWS, gla_cols), lambda r: (0, r, 0)),
                   pl.BlockSpec((HGRN_HEADS, WPREP_ROWS, hgrn_cols), lambda r: (0, r, 0)),
                   pl.BlockSpec((WPREP_ROWS, LANES), lambda r: (r, 0))],
        out_shape=[jax.ShapeDtypeStruct((GLA_HEADS, d, gla_cols), BF16),
                   jax.ShapeDtypeStruct((HGRN_HEADS, d, hgrn_cols), BF16),
                   jax.ShapeDtypeStruct((d, LANES), BF16)],
        compiler_params=pltpu.CompilerParams(
            dimension_semantics=("arbitrary",), vmem_limit_bytes=VMEM_LIMIT_BYTES),
        name="win_prep",
    )(wt)
    w2h = jnp.pad(w2.astype(BF16), ((0, LANES - GLA_GATE_RANK), (0, 0)))
    w2h = w2h.reshape(LANES, GLA_HEADS, GLA_DK).transpose(1, 0, 2)
    b2h = b2.reshape(GLA_HEADS, 1, GLA_DK)
    return w_gla, w_hg, w_gr, w2h, b2h


def kernel(x, c, ada_w, ada_b, norm_ffn1_w, ffn1_wi, ffn1_wo, norm_mix_w, w_in, gla_gate_w2,
           gla_gate_b2, gla_norm_w, hgrn_norm_w, hgrn_lower_bounds, w_out, norm_ffn2_w, ffn2_wi,
           ffn2_wo, final_norm_w):
    batch, seq, d = x.shape
    depth = ada_w.shape[0]
    m = batch * seq
    consts = _chunk_constants()
    xc = x.reshape(m, d)
    c_pad = jnp.pad(c, ((0, SUBLANES - batch % SUBLANES), (0, 0))) if batch % SUBLANES else c
    gla_v = GLA_HEADS * GLA_DV
    lb_heads = hgrn_lower_bounds.astype(F32).reshape(depth + 1, HGRN_HEADS, HGRN_DK).transpose(1, 0, 2)

    for l in range(depth):
        mod = _adaln(c_pad, ada_w[l], ada_b[l][None, :])[:batch].reshape(batch, N_MOD, d)
        d_ff_pad = -(-ffn1_wo.shape[1] // FFN_PAD) * FFN_PAD
        wab1, wo1 = _prep_ffn_weights(ffn1_wi[l], ffn1_wo[l], FFN_TF_WIDE)
        w_gla, w_hg, w_gr, w2h, b2h = _prep_mixer_weights(w_in, l, gla_gate_w2[l], gla_gate_b2[l])
        wout = w_out[l].astype(BF16)

        x1, h2, gr = _ffn(xc, mod, norm_ffn1_w[l][None, :], norm_mix_w[l][None, :], wab1, wo1,
                          d_ff_pad=d_ff_pad, tokens_per_batch=seq, mod_base=0, epilogue="prenorm",
                          w_aux=w_gr)
        o_gla, wo2 = _mixer(h2, w_gla, (gr, w2h, b2h), gla_norm_w[l][None, :], consts,
                            kind="gla", layer=l, batch=batch, tokens_per_batch=seq, dk=GLA_DK,
                            dv=GLA_DV, side=("wo", ffn2_wo[l]))
        o_hg, wab2 = _mixer(h2, w_hg, (lb_heads,), hgrn_norm_w[l][None, :], consts,
                            kind="hgrn", layer=l, batch=batch, tokens_per_batch=seq, dk=HGRN_DK,
                            dv=HGRN_DV, side=("wi", ffn2_wi[l]))
        last = l == depth - 1
        nw2 = final_norm_w[None, :] if last else norm_ffn2_w[l][None, :]
        (xc,) = _ffn(x1, mod, norm_ffn2_w[l][None, :], nw2, wab2, wo2, d_ff_pad=d_ff_pad,
                     tokens_per_batch=seq, mod_base=6, epilogue="final" if last else "none",
                     mix=(o_gla, o_hg, wout[:gla_v], wout[gla_v:]), mix_gate_row=5)
    return xc.reshape(batch, seq, d)
```

```python
import functools

import jax
import jax.numpy as jnp
import numpy as np
from jax import lax
from jax.experimental import pallas as pl
from jax.experimental.pallas import tpu as pltpu

F32 = jnp.float32
BF16 = jnp.bfloat16

GLA_HEADS = 4
GLA_DK = 128
GLA_DV = 256
GLA_GATE_RANK = 16
GLA_GATE_NORMALIZER = 16.0
HGRN_HEADS = 8
HGRN_DK = 128
HGRN_DV = 128
CHUNK = 64
MACARON_W = 0.5
N_MOD = 9
EPS = 1e-6

LANES = 128
SUBLANES = 8
MXU_N = 256
VMEM_LIMIT_BYTES = 56 * 1024 * 1024

FFN_TM = 512
FFN_TF = 512
FFN_TF_WIDE = 1024
FFN_PAD = 512
PROLOGUE_ROWS = 256
NORM_ROWS = 16
AUX_ROWS = 128
MIX_TC = 1024
ADALN_TN = 1024
WPREP_ROWS = 256
SIDE_ROWS = 16
ROW_BLK = 8

_NT = (((1,), (1,)), ((), ()))
_TN = (((0,), (0,)), ((), ()))


def _sigmoid(x):
    return jax.nn.sigmoid(x)


def _rms(x):
    return x * lax.rsqrt(jnp.mean(x * x, axis=-1, keepdims=True) + EPS)


def _adaln_kernel(c_ref, w_ref, b_ref, o_ref):
    c = c_ref[...]
    ca = (c * _sigmoid(c)).astype(BF16)
    o_ref[...] = jnp.dot(ca, w_ref[...].astype(BF16), preferred_element_type=F32) + b_ref[...]


def _adaln(c_pad, w, b):
    rows, d = c_pad.shape
    n = w.shape[1]
    assert n % ADALN_TN == 0
    return pl.pallas_call(
        _adaln_kernel,
        grid=(n // ADALN_TN,),
        in_specs=[
            pl.BlockSpec((rows, d), lambda j: (0, 0)),
            pl.BlockSpec((d, ADALN_TN), lambda j: (0, j)),
            pl.BlockSpec((1, ADALN_TN), lambda j: (0, j)),
        ],
        out_specs=pl.BlockSpec((rows, ADALN_TN), lambda j: (0, j)),
        out_shape=jax.ShapeDtypeStruct((rows, n), F32),
        compiler_params=pltpu.CompilerParams(
            dimension_semantics=("arbitrary",), vmem_limit_bytes=VMEM_LIMIT_BYTES),
        name="adaln",
    )(c_pad, w, b)


def _ffn_kernel(*refs, mod_base, epilogue, mix_gate_row, last_cols):
    x_ref, mod_ref, nw_ref, nw2_ref, wab_ref, wo_ref, *rest = refs
    if mix_gate_row is not None:
        oa_ref, ob_ref, wa_ref, wb_ref, *rest = rest
    if epilogue == "prenorm":
        waux_ref, *rest = rest
        xo_ref, ho_ref, aux_ref, h_scr, *rest = rest
    else:
        xo_ref, h_scr, *rest = rest
    xin_ref = rest[0] if mix_gate_row is not None else x_ref
    f = pl.program_id(1)
    last = pl.num_programs(1) - 1
    tm = x_ref.shape[0]

    def prologue():
        shift = mod_ref[mod_base:mod_base + 1, :]
        gain = nw_ref[...] * (1.0 + mod_ref[mod_base + 1:mod_base + 2, :])
        for r0 in range(0, tm, PROLOGUE_ROWS):
            rows = slice(r0, r0 + PROLOGUE_ROWS)
            if mix_gate_row is not None:
                y = (jnp.dot(oa_ref[rows, :], wa_ref[...], preferred_element_type=F32)
                     + jnp.dot(ob_ref[rows, :], wb_ref[...], preferred_element_type=F32))
                xin_ref[rows, :] = x_ref[rows, :] + mod_ref[mix_gate_row:mix_gate_row + 1, :] * y
            for c0 in range(r0, r0 + PROLOGUE_ROWS, NORM_ROWS):
                chunk = slice(c0, c0 + NORM_ROWS)
                h_scr[chunk, :] = (_rms(xin_ref[chunk, :]) * gain + shift).astype(BF16)

    def swiglu_step(first, cols=None):
        cols = wo_ref.shape[0] if cols is None else cols
        zab = jnp.dot(h_scr[...], wab_ref[:, :2 * cols], preferred_element_type=F32)
        pieces = cols // MXU_N
        a = jnp.concatenate([zab[:, (2 * i) * MXU_N:(2 * i + 1) * MXU_N] for i in range(pieces)],
                            axis=1)
        b = jnp.concatenate([zab[:, (2 * i + 1) * MXU_N:(2 * i + 2) * MXU_N]
                             for i in range(pieces)], axis=1)
        act = (a * _sigmoid(a) * b).astype(BF16)
        update = jnp.dot(act, wo_ref[:cols, :], preferred_element_type=F32)
        if first:
            xo_ref[...] = update
        else:
            xo_ref[...] += update

    def finish():
        gate = MACARON_W * mod_ref[mod_base + 2:mod_base + 3, :]
        if epilogue == "prenorm":
            shift2 = mod_ref[mod_base + 3:mod_base + 4, :]
            gain2 = nw2_ref[...] * (1.0 + mod_ref[mod_base + 4:mod_base + 5, :])
        for c0 in range(0, tm, NORM_ROWS):
            chunk = slice(c0, c0 + NORM_ROWS)
            xn = xin_ref[chunk, :] + gate * xo_ref[chunk, :]
            if epilogue == "prenorm":
                xo_ref[chunk, :] = xn
                ho_ref[chunk, :] = (_rms(xn) * gain2 + shift2).astype(BF16)
            elif epilogue == "final":
                xo_ref[chunk, :] = _rms(xn) * nw2_ref[...]
            else:
                xo_ref[chunk, :] = xn
            done = c0 + NORM_ROWS
            if epilogue == "prenorm" and done % AUX_ROWS == 0:
                rows = slice(done - AUX_ROWS, done)
                aux_ref[rows, :] = jnp.dot(ho_ref[rows, :], waux_ref[...],
                                           preferred_element_type=F32).astype(BF16)

    @pl.when(f == 0)
    def _():
        prologue()
        swiglu_step(first=True)

    @pl.when(jnp.logical_and(f > 0, f < last))
    def _():
        swiglu_step(first=False)

    @pl.when(f == last)
    def _():
        swiglu_step(first=False, cols=last_cols)
        finish()


def _ffn(x2d, mod, nw, nw2, wab, wo, *, d_ff_pad, tokens_per_batch, mod_base, epilogue, mix=None,
         mix_gate_row=None, w_aux=None):
    m, d = x2d.shape
    n_f, _, tf2 = wab.shape
    tf = tf2 // 2
    tm = FFN_TM
    assert m % tm == 0 and tokens_per_batch % tm == 0 and wo.shape[0] == n_f * tf
    assert d_ff_pad % MXU_N == 0 and (n_f - 1) * tf < d_ff_pad <= n_f * tf
    last_cols = d_ff_pad - (n_f - 1) * tf
    assert (mix is None) == (mix_gate_row is None) and n_f >= 2
    assert (w_aux is not None) == (epilogue == "prenorm")
    tiles_per_batch = tokens_per_batch // tm
    row_spec = pl.BlockSpec((tm, d), lambda i, f: (i, 0))
    vec_spec = pl.BlockSpec((1, d), lambda i, f: (0, 0))
    in_specs = [
        row_spec,
        pl.BlockSpec((None, N_MOD, d), lambda i, f: (i // tiles_per_batch, 0, 0)),
        vec_spec,
        vec_spec,
        pl.BlockSpec((None, d, 2 * tf), lambda i, f: (f, 0, 0)),
        pl.BlockSpec((tf, d), lambda i, f: (f, 0)),
    ]
    operands = [x2d, mod, nw, nw2, wab, wo]
    scratch = [pltpu.VMEM((tm, d), BF16)]
    if mix is not None:
        oa, ob, wa, wb = mix
        in_specs += [
            pl.BlockSpec((tm, oa.shape[1]), lambda i, f: (i, 0)),
            pl.BlockSpec((tm, ob.shape[1]), lambda i, f: (i, 0)),
            pl.BlockSpec(wa.shape, lambda i, f: (0, 0), pipeline_mode=pl.Buffered(1)),
            pl.BlockSpec(wb.shape, lambda i, f: (0, 0), pipeline_mode=pl.Buffered(1)),
        ]
        operands += [oa, ob, wa, wb]
        scratch.append(pltpu.VMEM((tm, d), F32))
    out_shape = [jax.ShapeDtypeStruct((m, d), F32)]
    out_specs = [row_spec]
    if epilogue == "prenorm":
        in_specs.append(pl.BlockSpec(w_aux.shape, lambda i, f: (0, 0)))
        operands.append(w_aux)
        out_shape += [jax.ShapeDtypeStruct((m, d), BF16),
                      jax.ShapeDtypeStruct((m, w_aux.shape[1]), BF16)]
        out_specs += [row_spec, pl.BlockSpec((tm, w_aux.shape[1]), lambda i, f: (i, 0))]
    return pl.pallas_call(
        functools.partial(_ffn_kernel, mod_base=mod_base, epilogue=epilogue,
                          mix_gate_row=mix_gate_row, last_cols=last_cols),
        grid=(m // tm, n_f),
        in_specs=in_specs,
        out_specs=out_specs,
        out_shape=out_shape,
        scratch_shapes=scratch,
        compiler_params=pltpu.CompilerParams(
            dimension_semantics=("arbitrary", "arbitrary"),
            vmem_limit_bytes=VMEM_LIMIT_BYTES),
        name="ffn_" + epilogue,
    )(*operands)


_LEVEL_HALVES = tuple(CHUNK >> (j + 1) for j in range(CHUNK.bit_length() - 1))
_N_LEVELS = len(_LEVEL_HALVES)
_CS_BLOCKS = _N_LEVELS + 2
_GROUP = 4
_GROUP_ROWS = _GROUP * CHUNK


def _chunk_constants():
    t = np.arange(CHUNK)
    tri = (t[None, :] <= t[:, None]).astype(np.float32)
    blocks = [tri]
    masks = []
    for half in _LEVEL_HALVES:
        ref = (t // (2 * half)) * (2 * half) + half
        if half < SUBLANES // 2:
            blocks.append(tri - tri[ref])
        same_block = (t[:, None] // (2 * half)) == (t[None, :] // (2 * half))
        is_query = (t % (2 * half)) >= half
        masks.append((same_block & is_query[:, None] & ~is_query[None, :]).astype(np.float32))
    masks.append(np.eye(CHUNK, dtype=np.float32))
    cm = np.concatenate(blocks, axis=0)
    cm2 = np.concatenate([cm, cm], axis=1)
    per_half = LANES // CHUNK
    group_masks = np.stack([np.kron(np.eye(per_half, dtype=np.float32), m) for m in masks], axis=0)
    return jnp.asarray(cm2, dtype=BF16), jnp.asarray(group_masks, dtype=F32)


def _mixer_kernel(*refs, tiles_per_head, side_job, **static):
    *io_refs, z_a, z_b, q_scr, k_scr, cs_scr, oi_scr, u_scr, s_scr = refs
    scratch = (q_scr, k_scr, cs_scr, oi_scr, u_scr, s_scr)
    s = pl.program_id(0)
    if side_job is not None:
        *io_refs, side_in, o_ref, side_out = io_refs
        io_refs.append(o_ref)
        side_job(s, side_in, side_out)

    @pl.when(s == 0)
    def _():
        z_b[...] = jnp.zeros_like(z_b)

    @pl.when(lax.rem(jnp.maximum(s - 1, 0), tiles_per_head) == 0)
    def _():
        s_scr[...] = jnp.zeros_like(s_scr)

    @pl.when(lax.rem(s, 2) == 0)
    def _():
        _mixer_tile(io_refs, scratch, z_b, z_a, **static)

    @pl.when(lax.rem(s, 2) == 1)
    def _():
        _mixer_tile(io_refs, scratch, z_a, z_b, **static)


def _mixer_tile(io_refs, scratch, z_scr, z_next, *, kind, layer, dk, dv, tc):
    if kind == "gla":
        h_ref, w_ref, gr_ref, w2_ref, b2_ref, nw_ref, cm_ref, mk_ref, o_ref = io_refs
    else:
        h_ref, w_ref, lb_ref, nw_ref, cm_ref, mk_ref, o_ref = io_refs
    q_scr, k_scr, cs_scr, oi_scr, u_scr, s_scr = scratch
    n_chunks = tc // CHUNK
    v_cols = slice(2 * dk, 2 * dk + dv)
    gate_cols = slice(2 * dk + dv, 2 * dk + 2 * dv)

    ncols = w_ref.shape[1]
    pieces = [slice(c0, min(c0 + MXU_N, ncols)) for c0 in range(0, ncols, MXU_N)]
    n_slots = n_chunks // _GROUP + 2

    def project_pieces(slot):
        for p, cols in enumerate(pieces):
            if p * n_slots // len(pieces) == slot:
                z_next[:, cols] = jnp.dot(h_ref[...], w_ref[:, cols], preferred_element_type=F32)

    project_pieces(0)

    if kind == "hgrn":
        raw = lb_ref[...]
        ex = jnp.exp(raw - jnp.max(raw, axis=0, keepdims=True))
        p = ex / jnp.sum(ex, axis=0, keepdims=True)
        lb = jnp.sum(p[0:layer + 1, :], axis=0, keepdims=True)

    def gates_and_cumsums(g):
        rows = slice(g * _GROUP_ROWS, (g + 1) * _GROUP_ROWS)
        cols = slice(g * _GROUP * dk, (g + 1) * _GROUP * dk)
        if kind == "gla":
            q_scr[rows, :] = z_scr[rows, 0:dk] * (dk ** -0.5)
            k_scr[rows, :] = z_scr[rows, dk:2 * dk]
            gp = jnp.dot(gr_ref[rows, :], w2_ref[...], preferred_element_type=F32) + b2_ref[...]
            la = (jnp.minimum(gp, 0.0) - jnp.log(1.0 + jnp.exp(-jnp.abs(gp)))) * (
                1.0 / GLA_GATE_NORMALIZER)
        else:
            hq = z_scr[rows, 0:dk]
            fr = z_scr[rows, dk:2 * dk]
            q_scr[rows, :] = hq * _sigmoid(hq)
            en = jnp.exp(-jnp.abs(fr))
            one_en = 1.0 + en
            log_sig = jnp.minimum(fr, 0.0) - jnp.log(one_en)
            sig_neg = jnp.where(fr >= 0.0, en, 1.0) / one_en
            la_a = jnp.log(lb)
            la_b = jnp.log(1.0 - lb) + log_sig
            la = jnp.maximum(la_a, la_b) + jnp.log(1.0 + jnp.exp(-jnp.abs(la_a - la_b)))
            k_scr[rows, :] = (1.0 - lb) * sig_neg

        la_wide = jnp.concatenate([la[i * CHUNK:(i + 1) * CHUNK] for i in range(_GROUP)], axis=1)
        la_hi = la_wide.astype(BF16)
        la_lo = (la_wide - la_hi.astype(F32)).astype(BF16)
        cs_mxu = jnp.dot(cm_ref[...], jnp.concatenate([la_hi, la_lo], axis=0),
                         preferred_element_type=F32)
        cs_scr[0:CHUNK, cols] = cs_mxu[0:CHUNK]
        mxu_block = 1
        for j, half in enumerate(_LEVEL_HALVES):
            lo = (j + 1) * CHUNK
            if half < SUBLANES // 2:
                cs_scr[lo:lo + CHUNK, cols] = cs_mxu[mxu_block * CHUNK:(mxu_block + 1) * CHUNK]
                mxu_block += 1
                continue
            for t0 in range(0, CHUNK, 2 * half):
                cs_scr[lo + t0:lo + t0 + 2 * half, cols] = (
                    cs_scr[t0:t0 + 2 * half, cols] - cs_scr[t0 + half:t0 + half + 1, cols])
        lo = (_N_LEVELS + 1) * CHUNK
        cs_scr[lo:lo + CHUNK, cols] = cs_scr[CHUNK - 1:CHUNK, cols] - cs_scr[0:CHUNK, cols]

    def cs_block(block, c):
        return cs_scr[block * CHUNK:(block + 1) * CHUNK, c * dk:(c + 1) * dk]

    for g in range(n_chunks // _GROUP):
        gates_and_cumsums(g)
        project_pieces(g + 1)
        chunks = range(g * _GROUP, (g + 1) * _GROUP)
        rows = slice(g * _GROUP_ROWS, (g + 1) * _GROUP_ROWS)
        q = q_scr[rows, :]
        k = k_scr[rows, :]
        n_halves = _GROUP_ROWS // LANES
        blks = LANES // ROW_BLK
        qk = jnp.sum(q * k, axis=-1, keepdims=True)
        eye = mk_ref[_N_LEVELS]
        attn = [[eye[r * ROW_BLK:(r + 1) * ROW_BLK] * qk[h * LANES + r * ROW_BLK:
                                                        h * LANES + (r + 1) * ROW_BLK]
                 for r in range(blks)] for h in range(n_halves)]
        for j, half in enumerate(_LEVEL_HALVES):
            d = jnp.concatenate([cs_block(j + 1, c) for c in chunks], axis=0)
            e = jnp.exp(-jnp.abs(d))
            ke = (k * e).astype(BF16)
            mask = mk_ref[j]
            if half >= ROW_BLK:
                q_runs = [r0 for r0 in range(0, _GROUP_ROWS, ROW_BLK) if (r0 // half) % 2 == 1]
                qe = jnp.concatenate([q[r0:r0 + ROW_BLK] * e[r0:r0 + ROW_BLK] for r0 in q_runs],
                                     axis=0).astype(BF16)
                s = lax.dot_general(qe, ke, _NT, preferred_element_type=F32)
                for n, r0 in enumerate(q_runs):
                    h, r = r0 // LANES, (r0 % LANES) // ROW_BLK
                    attn[h][r] = attn[h][r] + (
                        mask[r * ROW_BLK:(r + 1) * ROW_BLK]
                        * s[n * ROW_BLK:(n + 1) * ROW_BLK, h * LANES:(h + 1) * LANES])
            else:
                s = lax.dot_general((q * e).astype(BF16), ke, _NT, preferred_element_type=F32)
                for h in range(n_halves):
                    for r in range(blks):
                        r0 = h * LANES + r * ROW_BLK
                        attn[h][r] = attn[h][r] + (mask[r * ROW_BLK:(r + 1) * ROW_BLK]
                                                   * s[r0:r0 + ROW_BLK, h * LANES:(h + 1) * LANES])
        vb = z_scr[rows, v_cols].astype(BF16)
        for h in range(n_halves):
            a = jnp.concatenate(attn[h], axis=0).astype(BF16)
            oi_scr[rows.start + h * LANES:rows.start + (h + 1) * LANES, :] = jnp.dot(
                a, vb[h * LANES:(h + 1) * LANES], preferred_element_type=F32)
        for i, c in enumerate(chunks):
            crow = slice(c * CHUNK, (c + 1) * CHUNK)
            kd = (k_scr[crow, :] * jnp.exp(cs_block(_N_LEVELS + 1, c))).astype(BF16)
            u_scr[c] = lax.dot_general(vb[i * CHUNK:(i + 1) * CHUNK], kd, _TN,
                                       preferred_element_type=F32)

    project_pieces(n_slots - 1)
    st = s_scr[...]
    for c in range(n_chunks):
        rows = slice(c * CHUNK, (c + 1) * CHUNK)
        b = cs_block(0, c)
        qb = (q_scr[rows, :] * jnp.exp(b)).astype(BF16)
        o = lax.dot_general(qb, st.astype(BF16), _NT, preferred_element_type=F32) + oi_scr[rows, :]
        st = st * jnp.exp(b[CHUNK - 1:CHUNK, :]) + u_scr[c]
        gate = z_scr[rows, gate_cols]
        o_ref[rows, :] = (_rms(o) * nw_ref[...] * (gate * _sigmoid(gate))).astype(BF16)
    s_scr[...] = st


def _side_wi(s, w_ref, o_ref, *, n_slabs, d_ff, n_f, tf):
    @pl.when(s < n_slabs)
    def _():
        _wi_prep_kernel(w_ref, o_ref, d_ff=d_ff, n_f=n_f, tf=tf)


def _side_wo(s, w_ref, o_ref, *, n_src, n_out):
    @pl.when(s < n_src)
    def _():
        o_ref[...] = w_ref[...].astype(BF16)

    @pl.when(jnp.logical_and(s >= n_src, s < n_out))
    def _():
        o_ref[...] = jnp.zeros_like(o_ref)


def _side_job_specs(side, n_steps):
    what, w = side
    if what == "wi":
        d, d_ff = w.shape[0], w.shape[1] // 2
        n_f = -(-d_ff // FFN_TF)
        tf = FFN_TF
        n_slabs = d // SIDE_ROWS
        assert d % SIDE_ROWS == 0 and n_slabs <= n_steps and d_ff % LANES == 0
        slab = lambda s: jnp.minimum(s, n_slabs - 1)
        return (functools.partial(_side_wi, n_slabs=n_slabs, d_ff=d_ff, n_f=n_f, tf=tf),
                pl.BlockSpec((SIDE_ROWS, 2 * d_ff), lambda s: (slab(s), 0)),
                pl.BlockSpec((n_f, SIDE_ROWS, 2 * tf), lambda s: (0, slab(s), 0)),
                jax.ShapeDtypeStruct((n_f, d, 2 * tf), BF16), w)
    assert what == "wo"
    d_ff, d = w.shape
    n_src = d_ff // LANES
    n_out = -(-d_ff // FFN_TF) * FFN_TF // LANES
    assert d_ff % LANES == 0 and n_out <= n_steps
    return (functools.partial(_side_wo, n_src=n_src, n_out=n_out),
            pl.BlockSpec((LANES, d), lambda s: (jnp.minimum(s, n_src - 1), 0)),
            pl.BlockSpec((LANES, d), lambda s: (jnp.minimum(s, n_out - 1), 0)),
            jax.ShapeDtypeStruct((n_out * LANES, d), BF16), w)


def _mixer(h2, w_heads, extra, nw, consts, *, kind, layer, batch, tokens_per_batch, dk, dv,
           side=None):
    m, d = h2.shape
    n_heads, _, ncols = w_heads.shape
    tc = MIX_TC
    assert tokens_per_batch % tc == 0 and tc % _GROUP_ROWS == 0
    nt = tokens_per_batch // tc
    n_tiles = batch * n_heads * nt
    cm, mk = consts

    def coords(tile):
        bb = tile // (n_heads * nt)
        hh = lax.rem(tile // nt, n_heads)
        return bb * nt + lax.rem(tile, nt), hh

    def projected(s):
        return coords(jnp.minimum(s, n_tiles - 1))

    def consumed(s):
        return coords(jnp.maximum(s - 1, 0))

    const2 = lambda s: (0, 0)
    in_specs = [
        pl.BlockSpec((tc, d), lambda s: (projected(s)[0], 0)),
        pl.BlockSpec((None, d, ncols), lambda s: (projected(s)[1], 0, 0)),
    ]
    if kind == "gla":
        gr, w2, b2 = extra
        in_specs += [
            pl.BlockSpec((tc, LANES), lambda s: (consumed(s)[0], 0)),
            pl.BlockSpec((None, LANES, dk), lambda s: (consumed(s)[1], 0, 0)),
            pl.BlockSpec((None, 1, dk), lambda s: (consumed(s)[1], 0, 0)),
        ]
    else:
        (lbraw,) = extra
        in_specs += [pl.BlockSpec((None, lbraw.shape[1], dk), lambda s: (consumed(s)[1], 0, 0))]
    in_specs += [
        pl.BlockSpec((1, dv), const2),
        pl.BlockSpec(cm.shape, const2),
        pl.BlockSpec(mk.shape, lambda s: (0, 0, 0)),
    ]
    operands = [h2, w_heads, *extra, nw, cm, mk]
    out_specs = [pl.BlockSpec((tc, dv), lambda s: consumed(s))]
    out_shape = [jax.ShapeDtypeStruct((m, n_heads * dv), BF16)]
    side_job = None
    if side is not None:
        side_job, side_in_spec, side_out_spec, side_shape, side_operand = _side_job_specs(
            side, n_tiles + 1)
        in_specs.append(side_in_spec)
        operands.append(side_operand)
        out_specs.append(side_out_spec)
        out_shape.append(side_shape)
    return pl.pallas_call(
        functools.partial(_mixer_kernel, tiles_per_head=nt, side_job=side_job, kind=kind,
                          layer=layer, dk=dk, dv=dv, tc=tc),
        grid=(n_tiles + 1,),
        in_specs=in_specs,
        out_specs=out_specs,
        out_shape=out_shape,
        scratch_shapes=[
            pltpu.VMEM((tc, ncols), F32),
            pltpu.VMEM((tc, ncols), F32),
            pltpu.VMEM((tc, dk), F32),
            pltpu.VMEM((tc, dk), F32),
            pltpu.VMEM((_CS_BLOCKS * CHUNK, (tc // CHUNK) * dk), F32),
            pltpu.VMEM((tc, dv), F32),
            pltpu.VMEM((tc // CHUNK, dv, dk), F32),
            pltpu.VMEM((dv, dk), F32),
        ],
        compiler_params=pltpu.CompilerParams(
            dimension_semantics=("arbitrary",), vmem_limit_bytes=VMEM_LIMIT_BYTES),
        name="mixer_" + kind,
    )(*operands)


def _wi_prep_kernel(w_ref, o_ref, *, d_ff, n_f, tf):
    rows = w_ref.shape[0]
    for j in range(n_f):
        for i in range(tf // MXU_N):
            c0 = j * tf + i * MXU_N
            valid = max(0, min(MXU_N, d_ff - c0))
            for half in range(2):
                src = half * d_ff + c0
                parts = []
                if valid:
                    parts.append(w_ref[:, src:src + valid].astype(BF16))
                if valid < MXU_N:
                    parts.append(jnp.zeros((rows, MXU_N - valid), BF16))
                piece = parts[0] if len(parts) == 1 else jnp.concatenate(parts, axis=1)
                o_ref[j, :, (2 * i + half) * MXU_N:(2 * i + half + 1) * MXU_N] = piece


def _wo_prep_kernel(w_ref, *rest, n_full):
    *tail_refs, o_ref = rest
    r = pl.program_id(0)

    @pl.when(r < n_full)
    def _():
        o_ref[...] = w_ref[...].astype(BF16)

    @pl.when(r >= n_full)
    def _():
        rows = 0
        for t_ref in tail_refs:
            o_ref[rows:rows + LANES, :] = t_ref[...].astype(BF16)
            rows += LANES
        o_ref[rows:, :] = jnp.zeros((o_ref.shape[0] - rows, o_ref.shape[1]), BF16)


def _prep_ffn_weights(wi, wo, tf):
    d, d_ff = wi.shape[0], wo.shape[0]
    assert d_ff % LANES == 0 and d % WPREP_ROWS == 0
    n_f = -(-d_ff // tf)
    wab = pl.pallas_call(
        functools.partial(_wi_prep_kernel, d_ff=d_ff, n_f=n_f, tf=tf),
        grid=(d // WPREP_ROWS,),
        in_specs=[pl.BlockSpec((WPREP_ROWS, 2 * d_ff), lambda r: (r, 0))],
        out_specs=pl.BlockSpec((n_f, WPREP_ROWS, 2 * tf), lambda r: (0, r, 0)),
        out_shape=jax.ShapeDtypeStruct((n_f, d, 2 * tf), BF16),
        compiler_params=pltpu.CompilerParams(
            dimension_semantics=("arbitrary",), vmem_limit_bytes=VMEM_LIMIT_BYTES),
        name="wi_prep",
    )(wi)
    n_full = d_ff // tf
    n_tail = (d_ff - n_full * tf) // LANES
    tail0 = n_full * tf // LANES
    tail_specs = [pl.BlockSpec((LANES, d), functools.partial(lambda r, k: (tail0 + k, 0), k=k))
                  for k in range(n_tail)]
    wob = pl.pallas_call(
        functools.partial(_wo_prep_kernel, n_full=n_full),
        grid=(n_f,),
        in_specs=[pl.BlockSpec((tf, d), lambda r: (jnp.minimum(r, n_full - 1), 0))]
        + tail_specs,
        out_specs=pl.BlockSpec((tf, d), lambda r: (r, 0)),
        out_shape=jax.ShapeDtypeStruct((n_f * tf, d), BF16),
        compiler_params=pltpu.CompilerParams(
            dimension_semantics=("arbitrary",), vmem_limit_bytes=VMEM_LIMIT_BYTES),
        name="wo_prep",
    )(wo, *([wo] * n_tail))
    return wab, wob


_GLA_QK = GLA_HEADS * GLA_DK
_GLA_V = GLA_HEADS * GLA_DV
_HGRN_K = HGRN_HEADS * HGRN_DK
_HGRN_V = HGRN_HEADS * HGRN_DV
_IN_OFFS = tuple(int(o) for o in np.cumsum(
    [0, _GLA_QK, _GLA_QK, _GLA_V, _GLA_V, GLA_GATE_RANK, _HGRN_K, _HGRN_K, _HGRN_V, _HGRN_V]))


def _win_prep_kernel(wt_ref, og_ref, oh_ref, ogr_ref):
    def cols(part, h, width):
        f0 = _IN_OFFS[part] + h * width
        return wt_ref[f0:f0 + width, :].T.astype(BF16)

    gr = wt_ref[_IN_OFFS[4]:_IN_OFFS[5], :]
    gr = jnp.concatenate([gr, jnp.zeros((LANES - GLA_GATE_RANK, gr.shape[1]), F32)], axis=0)
    ogr_ref[...] = gr.T.astype(BF16)
    for h in range(GLA_HEADS):
        og_ref[h] = jnp.concatenate(
            [cols(0, h, GLA_DK), cols(1, h, GLA_DK), cols(2, h, GLA_DV), cols(3, h, GLA_DV)],
            axis=1)
    for h in range(HGRN_HEADS):
        oh_ref[h] = jnp.concatenate(
            [cols(5, h, HGRN_DK), cols(6, h, HGRN_DK), cols(7, h, HGRN_DV), cols(8, h, HGRN_DV)],
            axis=1)


def _prep_mixer_weights(w_in, layer, w2, b2):
    _, d, in_width = w_in.shape
    assert in_width == _IN_OFFS[-1] and d % WPREP_ROWS == 0
    gla_cols = 2 * GLA_DK + 2 * GLA_DV
    hgrn_cols = 2 * HGRN_DK + 2 * HGRN_DV
    wt = jnp.swapaxes(w_in, 1, 2)
    w_gla, w_hg, w_gr = pl.pallas_call(
        _win_prep_kernel,
        grid=(d // WPREP_ROWS,),
        in_specs=[pl.BlockSpec((None, in_width, WPREP_ROWS), lambda r: (layer, 0, r))],
        out_specs=[pl.BlockSpec((GLA_HEADS, WPREP_ROWS, gla_cols), lambda r: (0, r, 0)),
                   pl.BlockSpec((HGRN_HEADS, WPREP_ROWS, hgrn_cols), lambda r: (0, r, 0)),
                   pl.BlockSpec((WPREP_ROWS, LANES), lambda r: (r, 0))],
        out_shape=[jax.ShapeDtypeStruct((GLA_HEADS, d, gla_cols), BF16),
                   jax.ShapeDtypeStruct((HGRN_HEADS, d, hgrn_cols), BF16),
                   jax.ShapeDtypeStruct((d, LANES), BF16)],
        compiler_params=pltpu.CompilerParams(
            dimension_semantics=("arbitrary",), vmem_limit_bytes=VMEM_LIMIT_BYTES),
        name="win_prep",
    )(wt)
    w2h = jnp.pad(w2.astype(BF16), ((0, LANES - GLA_GATE_RANK), (0, 0)))
    w2h = w2h.reshape(LANES, GLA_HEADS, GLA_DK).transpose(1, 0, 2)
    b2h = b2.reshape(GLA_HEADS, 1, GLA_DK)
    return w_gla, w_hg, w_gr, w2h, b2h


def kernel(x, c, ada_w, ada_b, norm_ffn1_w, ffn1_wi, ffn1_wo, norm_mix_w, w_in, gla_gate_w2,
           gla_gate_b2, gla_norm_w, hgrn_norm_w, hgrn_lower_bounds, w_out, norm_ffn2_w, ffn2_wi,
           ffn2_wo, final_norm_w):
    batch, seq, d = x.shape
    depth = ada_w.shape[0]
    m = batch * seq
    consts = _chunk_constants()
    xc = x.reshape(m, d)
    c_pad = jnp.pad(c, ((0, SUBLANES - batch % SUBLANES), (0, 0))) if batch % SUBLANES else c
    gla_v = GLA_HEADS * GLA_DV
    lb_heads = hgrn_lower_bounds.astype(F32).reshape(depth + 1, HGRN_HEADS, HGRN_DK).transpose(1, 0, 2)

    for l in range(depth):
        mod = _adaln(c_pad, ada_w[l], ada_b[l][None, :])[:batch].reshape(batch, N_MOD, d)
        d_ff_pad = -(-ffn1_wo.shape[1] // FFN_PAD) * FFN_PAD
        wab1, wo1 = _prep_ffn_weights(ffn1_wi[l], ffn1_wo[l], FFN_TF_WIDE)
        w_gla, w_hg, w_gr, w2h, b2h = _prep_mixer_weights(w_in, l, gla_gate_w2[l], gla_gate_b2[l])
        wout = w_out[l].astype(BF16)

        x1, h2, gr = _ffn(xc, mod, norm_ffn1_w[l][None, :], norm_mix_w[l][None, :], wab1, wo1,
                          d_ff_pad=d_ff_pad, tokens_per_batch=seq, mod_base=0, epilogue="prenorm",
                          w_aux=w_gr)
        o_gla, wo2 = _mixer(h2, w_gla, (gr, w2h, b2h), gla_norm_w[l][None, :], consts,
                            kind="gla", layer=l, batch=batch, tokens_per_batch=seq, dk=GLA_DK,
                            dv=GLA_DV, side=("wo", ffn2_wo[l]))
        o_hg, wab2 = _mixer(h2, w_hg, (lb_heads,), hgrn_norm_w[l][None, :], consts,
                            kind="hgrn", layer=l, batch=batch, tokens_per_batch=seq, dk=HGRN_DK,
                            dv=HGRN_DV, side=("wi", ffn2_wi[l]))
        last = l == depth - 1
        nw2 = final_norm_w[None, :] if last else norm_ffn2_w[l][None, :]
        (xc,) = _ffn(x1, mod, norm_ffn2_w[l][None, :], nw2, wab2, wo2, d_ff_pad=d_ff_pad,
                     tokens_per_batch=seq, mod_base=6, epilogue="final" if last else "none",
                     mix=(o_gla, o_hg, wout[:gla_v], wout[gla_v:]), mix_gate_row=5)
    return xc.reshape(batch, seq, d)
```
